```python
import math
import jax, jax.numpy as jnp
from jax import lax
import numpy as np

D_MODEL = 1024
BATCH = 32
SEQ = 256
DEPTH = 1
DEC_BATCH = 2
DEC_SEQ = 4096
PAST_LEN = 256

GRID_W = 64
D_A = D_MODEL
H_A = 8
DK = D_A // H_A
DV = D_A // H_A
CHUNK = 32
D_B = D_MODEL
SHORT_W = 3
FILT_EMB = 33
FILT_BANDS = (FILT_EMB - 1) // 2
FILT_ORDER = 64
DECAY_FAST = 0.3
DECAY_SLOW = 1.5
DECAY_TARGET = 1e-2
DECAY_SHIFT = 0.05
D_FF = ((8 * D_MODEL // 3 + 255) // 256) * 256
N_MOD = 6
W_IN_COLS = 5 * D_A + 3 * D_B + 2 * D_MODEL
SPLIT_IDX = (D_A, 2 * D_A, 3 * D_A, 4 * D_A, 5 * D_A, 5 * D_A + 3 * D_B)
ALPHA = (2.0 * DEPTH) ** 0.25
BETA = (8.0 * DEPTH) ** -0.25
LN_EPS = 1e-5
RMS_EPS = 1e-6

kernel_name = "hybrid_hgrn2_hyena_prefix_dit_step"


def layer_norm(x, g, b):
    xf = x.astype(jnp.float32)
    mu = xf.mean(-1, keepdims=True)
    var = jnp.square(xf - mu).mean(-1, keepdims=True)
    return ((xf - mu) * lax.rsqrt(var + LN_EPS) * g.astype(jnp.float32) + b.astype(jnp.float32)).astype(x.dtype)


def hgrn_chunk_scan(q, k, v, logf, s0):
    bsz, L = q.shape[0], q.shape[1]
    n = L // CHUNK
    q = q.reshape(bsz, n, CHUNK, H_A, DK)
    k = k.reshape(bsz, n, CHUNK, H_A, DK)
    v = v.reshape(bsz, n, CHUNK, H_A, DV)
    b = jnp.cumsum(logf.reshape(bsz, n, CHUNK, H_A, DK), axis=2)
    b_last = b[:, :, -1:]
    q_in = q * jnp.exp(b)
    k_in = k * jnp.exp(-b)
    mask = jnp.tril(jnp.ones((CHUNK, CHUNK), dtype=bool))
    scores = jnp.where(mask, jnp.einsum('bncha,bnsha->bnhcs', q_in, k_in), 0.0)
    o_intra = jnp.einsum('bnhcs,bnshv->bnchv', scores, v)
    k_st = k * jnp.exp(b_last - b)
    u = jnp.einsum('bncha,bnchv->bnhav', k_st, v)
    decay = jnp.exp(b_last[:, :, 0])

    def step(s, inp):
        dec, du = inp
        return dec[..., None] * s + du, s

    s_final, s_starts = lax.scan(step, s0, (jnp.moveaxis(decay, 1, 0), jnp.moveaxis(u, 1, 0)))
    s_starts = jnp.moveaxis(s_starts, 0, 1)
    o_inter = jnp.einsum('bncha,bnhav->bnchv', q_in, s_starts)
    return (o_intra + o_inter).reshape(bsz, L, H_A, DV), s_final


def hgrn2_mixer(q_raw, f_fwd, f_bwd, i_raw, g_raw, lb, norm_w, s0_fwd, s0_bwd):
    f32 = jnp.float32
    bsz, L, _ = q_raw.shape

    def heads(t):
        return t.astype(f32).reshape(bsz, L, H_A, -1)

    q = heads(jax.nn.silu(q_raw.astype(f32)))
    v = heads(i_raw)

    def gates(z, lb_d):
        fg = lb_d + (1.0 - lb_d) * jax.nn.sigmoid(z.astype(f32))
        return heads(1.0 - fg), heads(jnp.log(fg))

    k_f, lf_f = gates(f_fwd, lb[0])
    k_b, lf_b = gates(f_bwd, lb[1])
    flip = lambda t: jnp.flip(t, axis=1)
    o_f, s_f = hgrn_chunk_scan(q, k_f, v, lf_f, s0_fwd.astype(f32))
    o_b, s_b = hgrn_chunk_scan(flip(q), flip(k_b), flip(v), flip(lf_b), s0_bwd.astype(f32))
    o = o_f + flip(o_b)
    o = o * lax.rsqrt(jnp.mean(jnp.square(o), axis=-1, keepdims=True) + RMS_EPS) * norm_w.astype(f32)
    o = o.reshape(bsz, L, D_A) * jax.nn.silu(g_raw.astype(f32))
    return o.astype(q_raw.dtype), s_f, s_b


def short_conv(u, w, b, row_len):
    bsz, L, ch = u.shape
    n_rows = L // row_len
    u4 = u.reshape(bsz, n_rows, row_len, ch)
    up = jnp.pad(u4, ((0, 0), (0, 0), (1, 1), (0, 0)))
    y = up[:, :, :-2] * w[0] + up[:, :, 1:-1] * w[1] + up[:, :, 2:] * w[2] + b
    return y.reshape(bsz, L, ch)


def hyena_filters(L, w1, b1, w2, b2, w3, b3, freq, w4):
    f32 = jnp.float32
    t = jnp.linspace(0.0, 1.0, L, dtype=f32)[:, None]
    wpos = (2.0 * math.pi / L) * jnp.arange(L, dtype=f32)[:, None]
    bands = jnp.linspace(1e-4, FILT_BANDS - 1, FILT_BANDS, dtype=f32)[None, :]
    z = jnp.concatenate([t, jnp.cos(bands * wpos), -jnp.sin(bands * wpos)], axis=-1)
    freq = freq.astype(f32)
    h = jnp.sin(freq[0] * (z @ w1.astype(f32) + b1.astype(f32)))
    h = jnp.sin(freq[1] * (h @ w2.astype(f32) + b2.astype(f32)))
    h = jnp.sin(freq[2] * (h @ w3.astype(f32) + b3.astype(f32)))
    h = h @ w4.astype(f32)
    max_decay = math.log(DECAY_TARGET) / DECAY_FAST
    min_decay = math.log(DECAY_TARGET) / DECAY_SLOW
    deltas = jnp.linspace(min_decay, max_decay, D_B, dtype=f32)
    window = jnp.exp(-t * jnp.abs(deltas)) + DECAY_SHIFT
    h = h.reshape(L, 2, D_B) * window[:, None, :]
    return h[:, 0], h[:, 1]


def long_conv(u, h_fwd, h_bwd, skip):
    f32 = jnp.float32
    L = u.shape[1]
    kern = jnp.concatenate([h_fwd, jnp.zeros((1, D_B), f32), jnp.flip(h_bwd[1:], axis=0)], axis=0)
    uf = u.astype(f32)
    U = jnp.fft.rfft(uf, n=2 * L, axis=1)
    K = jnp.fft.rfft(kern, n=2 * L, axis=0)
    y = jnp.fft.irfft(U * K[None], n=2 * L, axis=1)[:, :L]
    return (y + uf * skip.astype(f32)).astype(u.dtype)


def hyena_mixer(u, p, row_len):
    u = short_conv(u, p['hy_conv_w'], p['hy_conv_b'], row_len)
    x0, x1, v = jnp.split(u, 3, axis=-1)
    h_f, h_b = hyena_filters(u.shape[1], p['filt_w1'], p['filt_b1'], p['filt_w2'], p['filt_b2'],
                             p['filt_w3'], p['filt_b3'], p['filt_freq'], p['filt_w4'])
    return x0 * long_conv(v * x1, h_f, h_b, p['hy_skip'])


def trunk_layer(x, cond, row_len, s0_fwd, s0_bwd, lb, p):
    mod = (jax.nn.silu(cond) @ p['ada_w'] + p['ada_b']).reshape(cond.shape[0], N_MOD, D_MODEL)
    shift1, scale1, gate1 = mod[:, None, 0], mod[:, None, 1], mod[:, None, 2]
    shift2, scale2, gate2 = mod[:, None, 3], mod[:, None, 4], mod[:, None, 5]

    h = x * (1.0 + scale1) + shift1
    proj = h @ p['w_in']
    q, f_fwd, f_bwd, i_raw, g_raw, hy, mg = jnp.split(proj, SPLIT_IDX, axis=-1)
    o_a, s_f, s_b = hgrn2_mixer(q, f_fwd, f_bwd, i_raw, g_raw, lb, p['hgrn_norm_w'], s0_fwd, s0_bwd)
    o_b = hyena_mixer(hy, p, row_len)
    gate_a, gate_b = jnp.split(jax.nn.sigmoid(mg), 2, axis=-1)
    mix = (gate_a * (o_a @ p['proj_a']) + gate_b * (o_b @ p['proj_b'])) @ p['w_out']
    x = layer_norm(ALPHA * x + gate1 * mix, p['ln1_g'], p['ln1_b'])

    h = x * (1.0 + scale2) + shift2
    gt, up = jnp.split(h @ p['ffn_w_in'], 2, axis=-1)
    ff = (jax.nn.silu(gt) * up) @ p['ffn_w_out']
    x = layer_norm(ALPHA * x + gate2 * ff, p['ln2_g'], p['ln2_b'])
    return x, s_f, s_b


def setup_inputs(seed: int = 0) -> dict:
    key = jax.random.key(seed)
    ks = jax.random.split(key, 32)
    f32 = jnp.float32

    def nrm(k, shape, scale):
        return jax.random.normal(k, shape, f32) * scale

    return {
        "x_prompt": nrm(ks[0], (BATCH, SEQ, D_MODEL), 1.0),
        "x_sample": nrm(ks[1], (DEC_BATCH, DEC_SEQ, D_MODEL), 1.0),
        "state_hgrn": nrm(ks[2], (DEC_BATCH, DEPTH, 2, H_A, DK, DV), 0.5),
        "c": nrm(ks[3], (DEC_BATCH, D_MODEL), 1.0),
        "c_ctx": nrm(ks[4], (D_MODEL,), 1.0),
        "ada_w": nrm(ks[5], (DEPTH, D_MODEL, N_MOD * D_MODEL), 0.5 * D_MODEL ** -0.5),
        "ada_b": nrm(ks[6], (DEPTH, N_MOD * D_MODEL), 0.02),
        "w_in": nrm(ks[7], (DEPTH, D_MODEL, W_IN_COLS), D_MODEL ** -0.5),
        "hgrn_lb_logits": nrm(ks[8], (DEPTH + 1, 2, D_A), 0.1),
        "hgrn_norm_w": 1.0 + nrm(ks[9], (DEPTH, DV), 0.05),
        "hy_conv_w": nrm(ks[10], (DEPTH, SHORT_W, 3 * D_B), SHORT_W ** -0.5),
        "hy_conv_b": nrm(ks[11], (DEPTH, 3 * D_B), 0.02),
        "filt_w1": nrm(ks[12], (DEPTH, FILT_EMB, FILT_ORDER), FILT_EMB ** -0.5),
        "filt_b1": nrm(ks[13], (DEPTH, FILT_ORDER), 0.1),
        "filt_w2": nrm(ks[14], (DEPTH, FILT_ORDER, FILT_ORDER), FILT_ORDER ** -0.5),
        "filt_b2": nrm(ks[15], (DEPTH, FILT_ORDER), 0.1),
        "filt_w3": nrm(ks[16], (DEPTH, FILT_ORDER, FILT_ORDER), FILT_ORDER ** -0.5),
        "filt_b3": nrm(ks[17], (DEPTH, FILT_ORDER), 0.1),
        "filt_freq": 1.0 + nrm(ks[18], (DEPTH, 3, FILT_ORDER), 0.05),
        "filt_w4": nrm(ks[19], (DEPTH, FILT_ORDER, 2 * D_B), FILT_ORDER ** -0.5),
        "hy_skip": nrm(ks[20], (DEPTH, D_B), 0.5),
        "proj_a": nrm(ks[21], (DEPTH, D_A, D_MODEL), D_A ** -0.5),
        "proj_b": nrm(ks[22], (DEPTH, D_B, D_MODEL), D_B ** -0.5),
        "w_out": nrm(ks[23], (DEPTH, D_MODEL, D_MODEL), BETA * D_MODEL ** -0.5),
        "ln1_g": 1.0 + nrm(ks[24], (DEPTH, D_MODEL), 0.05),
        "ln1_b": nrm(ks[25], (DEPTH, D_MODEL), 0.02),
        "ffn_w_in": nrm(ks[26], (DEPTH, D_MODEL, 2 * D_FF), D_MODEL ** -0.5),
        "ffn_w_out": nrm(ks[27], (DEPTH, D_FF, D_MODEL), BETA * D_FF ** -0.5),
        "ln2_g": 1.0 + nrm(ks[28], (DEPTH, D_MODEL), 0.05),
        "ln2_b": nrm(ks[29], (DEPTH, D_MODEL), 0.02),
    }


def reference(x_prompt, x_sample, state_hgrn, c, c_ctx, ada_w, ada_b, w_in, hgrn_lb_logits, hgrn_norm_w,
              hy_conv_w, hy_conv_b, filt_w1, filt_b1, filt_w2, filt_b2, filt_w3, filt_b3, filt_freq, filt_w4,
              hy_skip, proj_a, proj_b, w_out, ln1_g, ln1_b, ffn_w_in, ffn_w_out, ln2_g, ln2_b):
    lb_all = jnp.cumsum(jax.nn.softmax(hgrn_lb_logits.astype(jnp.float32), axis=0), axis=0)
    ctx_len = x_prompt.shape[1]
    rows = x_sample.shape[1] // GRID_W
    lat_row_len = x_sample.shape[1] // rows
    zero_state = jnp.zeros((x_prompt.shape[0], H_A, DK, DV), jnp.float32)
    y_prompt = x_prompt
    y_sample = x_sample
    ctx_states = []
    for l in range(DEPTH):
        p = dict(ada_w=ada_w[l], ada_b=ada_b[l], w_in=w_in[l], hgrn_norm_w=hgrn_norm_w[l],
                 hy_conv_w=hy_conv_w[l], hy_conv_b=hy_conv_b[l], filt_w1=filt_w1[l], filt_b1=filt_b1[l],
                 filt_w2=filt_w2[l], filt_b2=filt_b2[l], filt_w3=filt_w3[l], filt_b3=filt_b3[l],
                 filt_freq=filt_freq[l], filt_w4=filt_w4[l], hy_skip=hy_skip[l], proj_a=proj_a[l],
                 proj_b=proj_b[l], w_out=w_out[l], ln1_g=ln1_g[l], ln1_b=ln1_b[l],
                 ffn_w_in=ffn_w_in[l], ffn_w_out=ffn_w_out[l], ln2_g=ln2_g[l], ln2_b=ln2_b[l])
        lb = lb_all[l]
        y_prompt, s_f, s_b = trunk_layer(y_prompt, c_ctx[None, :], ctx_len, zero_state, zero_state, lb, p)
        ctx_states.append(jnp.stack([s_f, s_b], axis=1).astype(x_prompt.dtype))
        y_sample, _, _ = trunk_layer(y_sample, c, lat_row_len, state_hgrn[:, l, 0], state_hgrn[:, l, 1], lb, p)
    new_state_hgrn = jnp.stack(ctx_states, axis=1)
    return (y_prompt, y_sample, new_state_hgrn)
```

```python
import functools
import math

import numpy as np
import jax
import jax.numpy as jnp
from jax import lax
from jax.experimental import pallas as pl
from jax.experimental.pallas import tpu as pltpu

F32 = jnp.float32
BF16 = jnp.bfloat16

D_MODEL = 1024
DEPTH = 1
GRID_W = 64
H_A = 8
DK = 128
DV = 128
D_B = 1024
FILT_EMB = 33
FILT_BANDS = 16
FILT_ORDER = 64
DECAY_FAST = 0.3
DECAY_SLOW = 1.5
DECAY_TARGET = 1e-2
DECAY_SHIFT = 0.05
D_FF = 2816
N_MOD = 6
W_IN_COLS = 10 * D_MODEL
ALPHA = (2.0 * DEPTH) ** 0.25
LN_EPS = 1e-5
RMS_EPS = 1e-6

LANE = 128
BLK = 256
NFREQ = 2 * BLK
CHUNK = 32
NCHUNK = BLK // CHUNK
VMEM_LIMIT = 56 * 1024 * 1024

CB_Q, CB_FF, CB_FB, CB_I, CB_G, CB_X0, CB_X1, CB_V, CB_GA, CB_GB = range(10)


def _sigmoid(x):
    return 1.0 / (1.0 + jnp.exp(-x))


def _dot(a, b):
    return jnp.dot(a, b, preferred_element_type=F32)


def _dot_nt(a, b):
    return lax.dot_general(a, b, (((1,), (1,)), ((), ())), preferred_element_type=F32)


def _dot_tn(a, b):
    return lax.dot_general(a, b, (((0,), (0,)), ((), ())), preferred_element_type=F32)


def _dot_hi(a, b):
    return jnp.dot(a, b, preferred_element_type=F32, precision=lax.Precision.HIGHEST)


def _params(*sem):
    return pltpu.CompilerParams(dimension_semantics=sem, vmem_limit_bytes=VMEM_LIMIT)


@functools.lru_cache(maxsize=None)
def _dft_consts():
    n = np.arange(BLK, dtype=np.float64)
    f = np.arange(BLK, dtype=np.float64)
    ang = 2.0 * np.pi * np.outer(f, n) / NFREQ
    fwd = np.zeros((NFREQ, BLK), np.float64)
    fwd[:BLK] = np.cos(ang)
    fwd[BLK + 1:] = -np.sin(ang[1:])
    fwd[BLK] = np.cos(np.pi * n)
    inv = np.zeros((BLK, NFREQ), np.float64)
    scale = np.full((BLK,), 2.0)
    scale[0] = 1.0
    inv[:, :BLK] = np.cos(ang.T) * scale[None, :]
    inv[:, BLK + 1:] = -2.0 * np.sin(ang.T[:, 1:])
    inv[:, BLK] = np.cos(np.pi * n)
    inv /= NFREQ
    fr = np.arange(NFREQ)
    freq_of_row = np.where(fr < BLK, fr, np.where(fr == BLK, BLK, fr - BLK))
    sgn = np.where(freq_of_row % 2 == 0, 1.0, -1.0)[:, None]
    return fwd.astype(np.float32), inv.astype(np.float32), sgn.astype(np.float32)


@functools.lru_cache(maxsize=None)
def _scan_consts():
    t = np.arange(BLK)
    ct = t // CHUNK
    same = ct[:, None] == ct[None, :]
    tri_f = (same & (t[None, :] <= t[:, None])).astype(np.float32)
    tri_b = (same & (t[None, :] >= t[:, None])).astype(np.float32)

    def levels(p, diag):
        x = p[:, None] ^ p[None, :]
        lvl = np.zeros_like(x)
        for bit in range(1, NCHUNK.bit_length()):
            lvl = np.where(x >= (1 << (bit - 1)), bit, lvl)
        lv = np.where(p[:, None] > p[None, :], lvl, -1)
        return np.where(same, np.where(diag, 0, -1), lv).astype(np.int32)

    lv_f = levels(ct, t[None, :] <= t[:, None])
    lv_b = levels(NCHUNK - 1 - ct, t[None, :] >= t[:, None])
    return tri_f, tri_b, lv_f, lv_b


@functools.lru_cache(maxsize=None)
def _filter_positions(seq_len):
    f32 = np.float32
    j = np.arange(-seq_len, seq_len)
    p = np.abs(j)
    valid = (j > -seq_len)
    pc = np.minimum(p, seq_len - 1)
    t = np.linspace(0.0, 1.0, seq_len, dtype=f32)[pc]
    wpos = (f32(2.0 * math.pi / seq_len) * np.arange(seq_len, dtype=f32))[pc]
    bands = np.linspace(1e-4, FILT_BANDS - 1, FILT_BANDS, dtype=f32)
    arg = (bands[None, :] * wpos[:, None]).astype(f32)
    z = np.zeros((2 * seq_len, LANE), f32)
    z[:, 0] = t
    z[:, 1:1 + FILT_BANDS] = np.cos(arg)
    z[:, 1 + FILT_BANDS:FILT_EMB] = -np.sin(arg)
    z[:, FILT_EMB] = valid.astype(f32)
    return z


@functools.lru_cache(maxsize=None)
def _decay_rates():
    max_decay = math.log(DECAY_TARGET) / DECAY_FAST
    min_decay = math.log(DECAY_TARGET) / DECAY_SLOW
    return np.abs(np.linspace(min_decay, max_decay, D_B, dtype=np.float32))[None, :]


def _mod_kernel(c_ref, w_ref, b_ref, o_ref):
    c = c_ref[...]
    s = (c * _sigmoid(c)).astype(BF16)
    o_ref[...] = _dot(s, w_ref[...].astype(BF16)) + b_ref[...]


def _modulation(cond8, ada_w, ada_b):
    tn = 1536
    n = N_MOD * D_MODEL
    return pl.pallas_call(
        _mod_kernel,
        grid=(n // tn,),
        in_specs=[pl.BlockSpec((8, D_MODEL), lambda j: (0, 0)),
                  pl.BlockSpec((D_MODEL, tn), lambda j: (0, j)),
                  pl.BlockSpec((1, tn), lambda j: (0, j))],
        out_specs=pl.BlockSpec((8, tn), lambda j: (0, j)),
        out_shape=jax.ShapeDtypeStruct((8, n), F32),
        compiler_params=_params("parallel"),
        name="modulation",
    )(cond8, ada_w, ada_b)


def _inproj_kernel(x_ref, mod_ref, w_ref, o_ref, h_ref):
    j = pl.program_id(1)

    @pl.when(j == 0)
    def _():
        h = x_ref[...] * (1.0 + mod_ref[0, 1:2, :]) + mod_ref[0, 0:1, :]
        h_ref[...] = h.astype(BF16)

    acc = _dot(h_ref[...], w_ref[...])
    sig = _sigmoid(acc)
    is_silu = jnp.logical_or(j == CB_Q, j == CB_G)
    is_sig = j >= CB_GA
    o_ref[...] = jnp.where(is_silu, acc * sig, jnp.where(is_sig, sig, acc))


def _inproj(x, mod3, w_bf, cond_row):
    tm = 1024
    t = x.shape[0]
    return pl.pallas_call(
        _inproj_kernel,
        grid=(t // tm, W_IN_COLS // D_MODEL),
        in_specs=[pl.BlockSpec((tm, D_MODEL), lambda i, j: (i, 0)),
                  pl.BlockSpec((1, N_MOD, D_MODEL), lambda i, j: (cond_row(i * tm), 0, 0)),
                  pl.BlockSpec((D_MODEL, D_MODEL), lambda i, j: (0, j))],
        out_specs=pl.BlockSpec((tm, D_MODEL), lambda i, j: (i, j)),
        out_shape=jax.ShapeDtypeStruct((t, W_IN_COLS), F32),
        scratch_shapes=[pltpu.VMEM((tm, D_MODEL), BF16)],
        compiler_params=_params("parallel", "arbitrary"),
        name="inproj",
    )(x, mod3, w_bf)


def _hgrn_direction(q, z, v, lb, st, lv, tri, reverse, use_state):
    fg = lb + (1.0 - lb) * _sigmoid(z)
    k = 1.0 - fg
    lf = jnp.log(fg)
    lf_hi = lf.astype(BF16)
    lf_lo = (lf - lf_hi.astype(F32)).astype(BF16)
    b = _dot(tri, lf_hi) + _dot(tri, lf_lo)
    qe = q * jnp.exp(b)
    k0 = k * jnp.exp(-b)

    order = [NCHUNK - 1 - i for i in range(NCHUNK)] if reverse else list(range(NCHUNK))
    chunk_of = {p: i for i, p in enumerate(order)}
    sl = lambda i: slice(i * CHUNK, (i + 1) * CHUNK)
    last_row = lambda i: (i * CHUNK) if reverse else (i * CHUNK + CHUNK - 1)
    c = [None] * NCHUNK
    for p in range(NCHUNK):
        r = last_row(chunk_of[p])
        c[p] = b[r:r + 1, :]
    cum = [jnp.zeros_like(c[0])]
    for p in range(NCHUNK):
        cum.append(cum[p] + c[p])
    total = cum[NCHUNK]

    qe_c, ke_c = {}, {}
    for p in range(NCHUNK):
        i = chunk_of[p]
        qe_c[p] = qe[sl(i), :]
        ke_c[p] = k[sl(i), :] * jnp.exp(c[p] - b[sl(i), :])

    def assemble(parts):
        return jnp.concatenate([parts[order[i]] for i in range(NCHUNK)], axis=0).astype(BF16)

    s_all = _dot_nt(qe.astype(BF16), k0.astype(BF16))
    s_mat = jnp.where(lv == 0, s_all, 0.0)
    zero = jnp.zeros((CHUNK, LANE), F32)
    nlev = NCHUNK.bit_length() - 1
    for lev in range(1, nlev + 1):
        half = 1 << (lev - 1)
        qp, kp = {}, {}
        for p in range(NCHUNK):
            pm = ((p >> lev) << lev) + half
            if p >= pm:
                qp[p] = qe_c[p] * jnp.exp(cum[p] - cum[pm])
                kp[p] = zero
            else:
                qp[p] = zero
                kp[p] = ke_c[p] * jnp.exp(cum[pm] - cum[p + 1])
        s_lev = _dot_nt(assemble(qp), assemble(kp))
        s_mat = jnp.where(lv == lev, s_lev, s_mat)

    v_bf = v.astype(BF16)
    out = _dot(s_mat.astype(BF16), v_bf)
    if use_state:
        q_start = assemble({p: qe_c[p] * jnp.exp(cum[p]) for p in range(NCHUNK)})
        out = out + _dot_nt(q_start, st.astype(BF16))
    k_end = assemble({p: ke_c[p] * jnp.exp(total - cum[p + 1]) for p in range(NCHUNK)})
    upd = _dot_tn(v_bf, k_end)
    new_st = st * jnp.exp(total) + upd if use_state else upd
    return out, new_st


def _hgrn_kernel(*refs, nb, zero_init, emit_state):
    it = iter(refs)
    qf_ref, zf_ref, vf_ref, qb_ref, zb_ref, vb_ref, lbl_ref = [next(it) for _ in range(7)]
    s0_ref = None if zero_init else next(it)
    lvf_ref, lvb_ref, trif_ref, trib_ref = [next(it) for _ in range(4)]
    of_ref, ob_ref = next(it), next(it)
    so_ref = next(it) if emit_state else None
    stf_ref, stb_ref = next(it), next(it)
    i = pl.program_id(2)

    l0 = lbl_ref[0]
    l1 = lbl_ref[1]
    m = jnp.maximum(l0, l1)
    e0 = jnp.exp(l0 - m)
    e1 = jnp.exp(l1 - m)
    lb = e0 / (e0 + e1)

    use_state = not (zero_init and nb == 1)
    if use_state:
        @pl.when(i == 0)
        def _():
            if zero_init:
                stf_ref[...] = jnp.zeros((DV, DK), F32)
                stb_ref[...] = jnp.zeros((DV, DK), F32)
            else:
                stf_ref[...] = s0_ref[0, 0, 0, 0].T
                stb_ref[...] = s0_ref[0, 0, 1, 0].T

    of, stf = _hgrn_direction(qf_ref[...], zf_ref[...], vf_ref[...], lb[0:1, :], stf_ref[...],
                              lvf_ref[...], trif_ref[...], False, use_state)
    ob, stb = _hgrn_direction(qb_ref[...], zb_ref[...], vb_ref[...], lb[1:2, :], stb_ref[...],
                              lvb_ref[...], trib_ref[...], True, use_state)
    of_ref[...] = of
    ob_ref[...] = ob
    if nb > 1:
        stf_ref[...] = stf
        stb_ref[...] = stb
    if emit_state:
        @pl.when(i == nb - 1)
        def _():
            so_ref[0, 0, 0, 0] = stf.T
            so_ref[0, 0, 1, 0] = stb.T


def _hgrn(proj, lb_logits, state, nseq, nb, emit_state):
    zero_init = state is None
    t = proj.shape[0]
    tri_f, tri_b, lv_f, lv_b = _scan_consts()
    hb = D_MODEL // LANE
    fwd = lambda cb: pl.BlockSpec((BLK, LANE), lambda b, h, i, cb=cb: (b * nb + i, cb * hb + h))
    bwd = lambda cb: pl.BlockSpec((BLK, LANE), lambda b, h, i, cb=cb: (b * nb + nb - 1 - i, cb * hb + h))
    const = lambda: pl.BlockSpec((BLK, BLK), lambda b, h, i: (0, 0))
    st_spec = pl.BlockSpec((1, 1, 2, 1, DK, DV), lambda b, h, i: (b, 0, 0, h, 0, 0))
    in_specs = [fwd(CB_Q), fwd(CB_FF), fwd(CB_I), bwd(CB_Q), bwd(CB_FB), bwd(CB_I),
                pl.BlockSpec((2, 2, LANE), lambda b, h, i: (0, 0, h))]
    args = [proj] * 6 + [lb_logits]
    if not zero_init:
        in_specs.append(st_spec)
        args.append(state)
    in_specs += [const(), const(), const(), const()]
    args += [jnp.asarray(lv_f), jnp.asarray(lv_b), jnp.asarray(tri_f, BF16), jnp.asarray(tri_b, BF16)]
    out_specs = [pl.BlockSpec((BLK, LANE), lambda b, h, i: (b * nb + i, h)),
                 pl.BlockSpec((BLK, LANE), lambda b, h, i: (b * nb + nb - 1 - i, h))]
    out_shape = [jax.ShapeDtypeStruct((t, D_MODEL), F32), jax.ShapeDtypeStruct((t, D_MODEL), F32)]
    if emit_state:
        out_specs.append(st_spec)
        out_shape.append(jax.ShapeDtypeStruct((nseq, DEPTH, 2, H_A, DK, DV), F32))
    return pl.pallas_call(
        functools.partial(_hgrn_kernel, nb=nb, zero_init=zero_init, emit_state=emit_state),
        grid=(nseq, H_A, nb),
        in_specs=in_specs,
        out_specs=out_specs,
        out_shape=out_shape,
        scratch_shapes=[pltpu.VMEM((DV, DK), F32), pltpu.VMEM((DV, DK), F32)],
        compiler_params=_params("parallel", "parallel", "arbitrary"),
        name="hgrn_scan",
    )(*args)


def _filter_kernel(z_ref, w1_ref, b1_ref, w2_ref, b2_ref, w3_ref, b3_ref, fq_ref, w4_ref, dec_ref,
                   ftop_ref, sgn_ref, o_ref, prev_ref):
    zp = z_ref[...]
    h = jnp.sin(fq_ref[0:1, :] * (_dot_hi(zp, w1_ref[...]) + b1_ref[...]))
    h = jnp.sin(fq_ref[1:2, :] * (_dot_hi(h, w2_ref[...]) + b2_ref[...]))
    h = jnp.sin(fq_ref[2:3, :] * (_dot_hi(h, w3_ref[...]) + b3_ref[...]))
    a = _dot_hi(h, w4_ref[...])
    window = jnp.exp(-zp[:, 0:1] * dec_ref[...]) + DECAY_SHIFT
    a = a * window * zp[:, FILT_EMB:FILT_EMB + 1]
    ah = _dot_hi(ftop_ref[...], a)
    o_ref[0] = ah + sgn_ref[...] * prev_ref[...]
    prev_ref[...] = ah


def _filter_spectra(seq_len, w1, b1, w2, b2, w3, b3, freq, w4):
    nb = seq_len // BLK
    fwd_dft, _, sgn = _dft_consts()
    zpos = jnp.asarray(_filter_positions(seq_len))
    pad2 = lambda w: jnp.pad(w, ((0, LANE - w.shape[0]), (0, LANE - w.shape[1])))
    padv = lambda b: jnp.pad(b, (0, LANE - b.shape[0]))[None, :]
    w1p, w2p, w3p = pad2(w1), pad2(w2), pad2(w3)
    b1p, b2p, b3p = padv(b1), padv(b2), padv(b3)
    fqp = jnp.pad(freq, ((0, 0), (0, LANE - freq.shape[1])))
    w4p = jnp.pad(w4, ((0, LANE - w4.shape[0]), (0, 0)))
    small = lambda shape: pl.BlockSpec(shape, lambda m: (0, 0))
    return pl.pallas_call(
        _filter_kernel,
        grid=(2 * nb,),
        in_specs=[pl.BlockSpec((BLK, LANE), lambda m: (m, 0)),
                  small((LANE, LANE)), small((1, LANE)), small((LANE, LANE)), small((1, LANE)),
                  small((LANE, LANE)), small((1, LANE)), small((3, LANE)),
                  pl.BlockSpec((LANE, D_B), lambda m: (0, jnp.where(m < nb, 1, 0))),
                  small((1, D_B)), small((NFREQ, BLK)), small((NFREQ, 1))],
        out_specs=pl.BlockSpec((1, NFREQ, D_B), lambda m: (jnp.maximum(m - 1, 0), 0, 0)),
        out_shape=jax.ShapeDtypeStruct((2 * nb - 1, NFREQ, D_B), F32),
        scratch_shapes=[pltpu.VMEM((NFREQ, D_B), F32)],
        compiler_params=_params("arbitrary"),
        name="hyena_filter",
    )(zpos, w1p, b1p, w2p, b2p, w3p, b3p, fqp, w4p, jnp.asarray(_decay_rates()),
      jnp.asarray(fwd_dft), jnp.asarray(sgn))


def _hy1_kernel(x0_ref, x1_ref, v_ref, cw_ref, cb_ref, f_ref, z_ref, x0o_ref, zh_ref, *, row_len):
    t = lax.broadcasted_iota(jnp.int32, (BLK, 1), 0)
    first = (t % row_len) == 0
    last = (t % row_len) == (row_len - 1)

    def conv(u, cb):
        lo = cb * D_B
        w = cw_ref[:, lo:lo + D_B]
        up = jnp.where(first, 0.0, pltpu.roll(u, 1, 0))
        dn = jnp.where(last, 0.0, pltpu.roll(u, BLK - 1, 0))
        return up * w[0:1, :] + u * w[1:2, :] + dn * w[2:3, :] + cb_ref[:, lo:lo + D_B]

    x0 = conv(x0_ref[...], 0)
    z = conv(v_ref[...], 2) * conv(x1_ref[...], 1)
    x0o_ref[...] = x0
    z_ref[...] = z
    zh_ref[0] = _dot(f_ref[...], z.astype(BF16))


def _hyena_front(proj, conv_w, conv_b, row_len):
    t = proj.shape[0]
    fwd_dft, _, _ = _dft_consts()
    col = lambda cb: pl.BlockSpec((BLK, D_B), lambda i, cb=cb: (i, cb))
    return pl.pallas_call(
        functools.partial(_hy1_kernel, row_len=row_len),
        grid=(t // BLK,),
        in_specs=[col(CB_X0), col(CB_X1), col(CB_V),
                  pl.BlockSpec((3, 3 * D_B), lambda i: (0, 0)),
                  pl.BlockSpec((1, 3 * D_B), lambda i: (0, 0)),
                  pl.BlockSpec((NFREQ, BLK), lambda i: (0, 0))],
        out_specs=[pl.BlockSpec((BLK, D_B), lambda i: (i, 0)),
                   pl.BlockSpec((BLK, D_B), lambda i: (i, 0)),
                   pl.BlockSpec((1, NFREQ, D_B), lambda i: (i, 0, 0))],
        out_shape=[jax.ShapeDtypeStruct((t, D_B), F32), jax.ShapeDtypeStruct((t, D_B), F32),
                   jax.ShapeDtypeStruct((t // BLK, NFREQ, D_B), F32)],
        compiler_params=_params("parallel"),
        name="hyena_front",
    )(proj, proj, proj, conv_w, conv_b[None, :], jnp.asarray(fwd_dft).astype(BF16))


ROWG = 64


def _hy2_kernel(zh_ref, kh_ref, z_ref, x0_ref, skip_ref, g_ref, o_ref, yh_ref, *, nb):
    i = pl.program_id(2)
    dt = o_ref.shape[1]
    row0 = lax.broadcasted_iota(jnp.int32, (ROWG, 1), 0) == 0
    for rg in range(BLK // ROWG):
        re = pl.ds(rg * ROWG, ROWG)
        im = pl.ds(BLK + rg * ROWG, ROWG)

        def body(j, acc):
            p, q, r = acc
            kidx = i - j + (nb - 1)
            zr = zh_ref[0, j, re, :]
            zi = zh_ref[0, j, im, :]
            kr = kh_ref[kidx, re, :]
            ki = kh_ref[kidx, im, :]
            return (p + zr * kr, q + zi * ki, r + (zr * ki + zi * kr))

        zeros = jnp.zeros((ROWG, dt), F32)
        p, q, r = lax.fori_loop(0, nb, body, (zeros, zeros, zeros))
        if rg == 0:
            yr = jnp.where(row0, p, p - q)
            yi = jnp.where(row0, q, r)
        else:
            yr = p - q
            yi = r
        yh_ref[re, :] = yr.astype(BF16)
        yh_ref[im, :] = yi.astype(BF16)
    z = z_ref[...]
    y = _dot(g_ref[...], yh_ref[...]) + z * skip_ref[...]
    o_ref[...] = x0_ref[...] * y


def _hyena_back(zh, khat, z, x0, skip, nseq, nb):
    t = z.shape[0]
    dt = LANE
    _, inv_dft, _ = _dft_consts()
    zh4 = zh.reshape(nseq, nb, NFREQ, D_B)
    tok = pl.BlockSpec((BLK, dt), lambda d, b, i: (b * nb + i, d))
    return pl.pallas_call(
        functools.partial(_hy2_kernel, nb=nb),
        grid=(D_B // dt, nseq, nb),
        in_specs=[pl.BlockSpec((1, nb, NFREQ, dt), lambda d, b, i: (b, 0, 0, d)),
                  pl.BlockSpec((2 * nb - 1, NFREQ, dt), lambda d, b, i: (0, 0, d)),
                  tok, tok,
                  pl.BlockSpec((1, dt), lambda d, b, i: (0, d)),
                  pl.BlockSpec((BLK, NFREQ), lambda d, b, i: (0, 0))],
        out_specs=tok,
        out_shape=jax.ShapeDtypeStruct((t, D_B), F32),
        scratch_shapes=[pltpu.VMEM((NFREQ, dt), BF16)],
        compiler_params=_params("parallel", "parallel", "arbitrary"),
        name="hyena_back",
    )(zh4, khat, z, x0, skip[None, :], jnp.asarray(inv_dft).astype(BF16))


def _layer_norm(y, g, b):
    mu = jnp.mean(y, axis=-1, keepdims=True)
    yc = y - mu
    var = jnp.mean(yc * yc, axis=-1, keepdims=True)
    return yc * lax.rsqrt(var + LN_EPS) * g + b


def _merge_kernel(of_ref, ob_ref, g_ref, ga_ref, gb_ref, hy_ref, x_ref, mod_ref, nw_ref,
                  pa_ref, pb_ref, wo_ref, lg_ref, lb_ref, o_ref):
    o = of_ref[...] + ob_ref[...]
    nw = nw_ref[...]
    parts = []
    for h in range(H_A):
        oh = o[:, h * DV:(h + 1) * DV]
        ms = jnp.mean(oh * oh, axis=-1, keepdims=True)
        parts.append(oh * lax.rsqrt(ms + RMS_EPS) * nw)
    oa = jnp.concatenate(parts, axis=1) * g_ref[...]
    a = _dot(oa.astype(BF16), pa_ref[...])
    b = _dot(hy_ref[...].astype(BF16), pb_ref[...])
    merged = ga_ref[...] * a + gb_ref[...] * b
    mix = _dot(merged.astype(BF16), wo_ref[...])
    y = ALPHA * x_ref[...] + mod_ref[0, 2:3, :] * mix
    o_ref[...] = _layer_norm(y, lg_ref[...], lb_ref[...])


def _merge(o_f, o_b, proj, o_hy, x, mod3, cond_row, norm_w, pa, pb, wo, ln_g, ln_b):
    tm = 256
    t = x.shape[0]
    tok = pl.BlockSpec((tm, D_MODEL), lambda i: (i, 0))
    col = lambda cb: pl.BlockSpec((tm, D_MODEL), lambda i, cb=cb: (i, cb))
    mat = pl.BlockSpec((D_MODEL, D_MODEL), lambda i: (0, 0))
    vec = pl.BlockSpec((1, D_MODEL), lambda i: (0, 0))
    return pl.pallas_call(
        _merge_kernel,
        grid=(t // tm,),
        in_specs=[tok, tok, col(CB_G), col(CB_GA), col(CB_GB), tok, tok,
                  pl.BlockSpec((1, N_MOD, D_MODEL), lambda i: (cond_row(i * tm), 0, 0)),
                  pl.BlockSpec((1, DV), lambda i: (0, 0)),
                  mat, mat, mat, vec, vec],
        out_specs=tok,
        out_shape=jax.ShapeDtypeStruct((t, D_MODEL), F32),
        compiler_params=_params("parallel"),
        name="merge",
    )(o_f, o_b, proj, proj, proj, o_hy, x, mod3, norm_w[None, :], pa, pb, wo, ln_g[None, :], ln_b[None, :])


def _ffn_kernel(x_ref, mod_ref, wg_ref, wu_ref, wo_ref, lg_ref, lb_ref, o_ref):
    x = x_ref[...]
    h = (x * (1.0 + mod_ref[0, 4:5, :]) + mod_ref[0, 3:4, :]).astype(BF16)
    gt = _dot(h, wg_ref[...])
    up = _dot(h, wu_ref[...])
    act = (gt * _sigmoid(gt) * up).astype(BF16)
    ff = _dot(act, wo_ref[...])
    y = ALPHA * x + mod_ref[0, 5:6, :] * ff
    o_ref[...] = _layer_norm(y, lg_ref[...], lb_ref[...])


def _ffn(x, mod3, cond_row, w_in, w_out, ln_g, ln_b):
    tm = 256
    t = x.shape[0]
    tok = pl.BlockSpec((tm, D_MODEL), lambda i: (i, 0))
    vec = pl.BlockSpec((1, D_MODEL), lambda i: (0, 0))
    return pl.pallas_call(
        _ffn_kernel,
        grid=(t // tm,),
        in_specs=[tok,
                  pl.BlockSpec((1, N_MOD, D_MODEL), lambda i: (cond_row(i * tm), 0, 0)),
                  pl.BlockSpec((D_MODEL, D_FF), lambda i: (0, 0)),
                  pl.BlockSpec((D_MODEL, D_FF), lambda i: (0, 1)),
                  pl.BlockSpec((D_FF, D_MODEL), lambda i: (0, 0)),
                  vec, vec],
        out_specs=tok,
        out_shape=jax.ShapeDtypeStruct((t, D_MODEL), F32),
        compiler_params=_params("parallel"),
        name="ffn",
    )(x, mod3, w_in, w_in, w_out, ln_g[None, :], ln_b[None, :])


def _trunk(x, mod3, cond_row, nseq, seq_len, row_len, state, emit_state, w, lb_logits):
    nb = seq_len // BLK
    proj = _inproj(x, mod3, w["w_in"], cond_row)
    hg = _hgrn(proj, lb_logits, state, nseq, nb, emit_state)
    o_f, o_b = hg[0], hg[1]
    khat = _filter_spectra(seq_len, w["filt_w1"], w["filt_b1"], w["filt_w2"], w["filt_b2"],
                           w["filt_w3"], w["filt_b3"], w["filt_freq"], w["filt_w4"])
    z, x0, zh = _hyena_front(proj, w["hy_conv_w"], w["hy_conv_b"], row_len)
    o_hy = _hyena_back(zh, khat, z, x0, w["hy_skip"], nseq, nb)
    x1 = _merge(o_f, o_b, proj, o_hy, x, mod3, cond_row, w["hgrn_norm_w"], w["proj_a"], w["proj_b"],
                w["w_out"], w["ln1_g"], w["ln1_b"])
    x2 = _ffn(x1, mod3, cond_row, w["ffn_w_in"], w["ffn_w_out"], w["ln2_g"], w["ln2_b"])
    return x2, (hg[2] if emit_state else None)


def kernel(x_prompt, x_sample, state_hgrn, c, c_ctx, ada_w, ada_b, w_in, hgrn_lb_logits, hgrn_norm_w,
           hy_conv_w, hy_conv_b, filt_w1, filt_b1, filt_w2, filt_b2, filt_w3, filt_b3, filt_freq, filt_w4,
           hy_skip, proj_a, proj_b, w_out, ln1_g, ln1_b, ffn_w_in, ffn_w_out, ln2_g, ln2_b):
    assert ada_w.shape[0] == DEPTH == 1
    batch, seq, _ = x_prompt.shape
    dec_batch, dec_seq, _ = x_sample.shape
    assert seq % BLK == 0 and dec_seq % BLK == 0 and BLK % GRID_W == 0 and dec_batch + 1 <= 8

    w = dict(w_in=w_in[0].astype(BF16), hy_conv_w=hy_conv_w[0], hy_conv_b=hy_conv_b[0],
             filt_w1=filt_w1[0], filt_b1=filt_b1[0], filt_w2=filt_w2[0], filt_b2=filt_b2[0],
             filt_w3=filt_w3[0], filt_b3=filt_b3[0], filt_freq=filt_freq[0], filt_w4=filt_w4[0],
             hy_skip=hy_skip[0], hgrn_norm_w=hgrn_norm_w[0], proj_a=proj_a[0].astype(BF16),
             proj_b=proj_b[0].astype(BF16), w_out=w_out[0].astype(BF16), ln1_g=ln1_g[0], ln1_b=ln1_b[0],
             ffn_w_in=ffn_w_in[0].astype(BF16), ffn_w_out=ffn_w_out[0].astype(BF16),
             ln2_g=ln2_g[0], ln2_b=ln2_b[0])

    cond8 = jnp.zeros((8, D_MODEL), F32).at[0].set(c_ctx).at[1:1 + dec_batch].set(c)
    mod3 = _modulation(cond8, ada_w[0], ada_b[0][None, :]).reshape(8, N_MOD, D_MODEL)

    xp = x_prompt.reshape(batch * seq, D_MODEL)
    xs = x_sample.reshape(dec_batch * dec_seq, D_MODEL)
    yp, new_state = _trunk(xp, mod3, lambda tok: 0, batch, seq, seq, None, True, w, hgrn_lb_logits)
    ys, _ = _trunk(xs, mod3, lambda tok: 1 + tok // dec_seq, dec_batch, dec_seq, GRID_W, state_hgrn, False,
                   w, hgrn_lb_logits)
    return (yp.reshape(batch, seq, D_MODEL), ys.reshape(dec_batch, dec_seq, D_MODEL), new_state)
```

```python
import functools
import math

import numpy as np
import jax
import jax.numpy as jnp
from jax import lax
from jax.experimental import pallas as pl
from jax.experimental.pallas import tpu as pltpu

F32 = jnp.float32
BF16 = jnp.bfloat16

D_MODEL = 1024
DEPTH = 1
GRID_W = 64
H_A = 8
DK = 128
DV = 128
D_B = 1024
FILT_EMB = 33
FILT_BANDS = 16
FILT_ORDER = 64
DECAY_FAST = 0.3
DECAY_SLOW = 1.5
DECAY_TARGET = 1e-2
DECAY_SHIFT = 0.05
D_FF = 2816
N_MOD = 6
W_IN_COLS = 10 * D_MODEL
ALPHA = (2.0 * DEPTH) ** 0.25
LN_EPS = 1e-5
RMS_EPS = 1e-6

LANE = 128
BLK = 256
NFREQ = 2 * BLK
CHUNK = 32
NCHUNK = BLK // CHUNK
HPS = 2
VMEM_LIMIT = 56 * 1024 * 1024

CB_Q, CB_FF, CB_FB, CB_I, CB_G, CB_X0, CB_X1, CB_V, CB_GA, CB_GB = range(10)


def _sigmoid(x):
    return 1.0 / (1.0 + jnp.exp(-x))


def _dot(a, b):
    return jnp.dot(a, b, preferred_element_type=F32)


def _dot_nt(a, b):
    return lax.dot_general(a, b, (((1,), (1,)), ((), ())), preferred_element_type=F32)


def _dot_tn(a, b):
    return lax.dot_general(a, b, (((0,), (0,)), ((), ())), preferred_element_type=F32)


def _dot_hi(a, b):
    return jnp.dot(a, b, preferred_element_type=F32, precision=lax.Precision.HIGHEST)


def _params(*sem):
    return pltpu.CompilerParams(dimension_semantics=sem, vmem_limit_bytes=VMEM_LIMIT)


@functools.lru_cache(maxsize=None)
def _dft_consts():
    n = np.arange(BLK, dtype=np.float64)
    f = np.arange(BLK, dtype=np.float64)
    ang = 2.0 * np.pi * np.outer(f, n) / NFREQ
    fwd = np.zeros((NFREQ, BLK), np.float64)
    fwd[:BLK] = np.cos(ang)
    fwd[BLK + 1:] = -np.sin(ang[1:])
    fwd[BLK] = np.cos(np.pi * n)
    inv = np.zeros((BLK, NFREQ), np.float64)
    scale = np.full((BLK,), 2.0)
    scale[0] = 1.0
    inv[:, :BLK] = np.cos(ang.T) * scale[None, :]
    inv[:, BLK + 1:] = -2.0 * np.sin(ang.T[:, 1:])
    inv[:, BLK] = np.cos(np.pi * n)
    inv /= NFREQ
    fr = np.arange(NFREQ)
    freq_of_row = np.where(fr < BLK, fr, np.where(fr == BLK, BLK, fr - BLK))
    sgn = np.where(freq_of_row % 2 == 0, 1.0, -1.0)[:, None]
    return fwd.astype(np.float32), inv.astype(np.float32), sgn.astype(np.float32)


@functools.lru_cache(maxsize=None)
def _scan_consts():
    t = np.arange(BLK)
    ct = t // CHUNK
    same = ct[:, None] == ct[None, :]
    tri_f = (same & (t[None, :] <= t[:, None])).astype(np.float32)
    tri_b = (same & (t[None, :] >= t[:, None])).astype(np.float32)

    def levels(p, diag):
        x = p[:, None] ^ p[None, :]
        lvl = np.zeros_like(x)
        for bit in range(1, NCHUNK.bit_length()):
            lvl = np.where(x >= (1 << (bit - 1)), bit, lvl)
        lv = np.where(p[:, None] > p[None, :], lvl, -1)
        return np.where(same, np.where(diag, 0, -1), lv).astype(np.int32)

    lv_f = levels(ct, t[None, :] <= t[:, None])
    lv_b = levels(NCHUNK - 1 - ct, t[None, :] >= t[:, None])
    return tri_f, tri_b, lv_f, lv_b


@functools.lru_cache(maxsize=None)
def _filter_positions(seq_len):
    f32 = np.float32
    j = np.arange(-seq_len, seq_len)
    p = np.abs(j)
    valid = (j > -seq_len)
    pc = np.minimum(p, seq_len - 1)
    t = np.linspace(0.0, 1.0, seq_len, dtype=f32)[pc]
    wpos = (f32(2.0 * math.pi / seq_len) * np.arange(seq_len, dtype=f32))[pc]
    bands = np.linspace(1e-4, FILT_BANDS - 1, FILT_BANDS, dtype=f32)
    arg = (bands[None, :] * wpos[:, None]).astype(f32)
    z = np.zeros((2 * seq_len, LANE), f32)
    z[:, 0] = t
    z[:, 1:1 + FILT_BANDS] = np.cos(arg)
    z[:, 1 + FILT_BANDS:FILT_EMB] = -np.sin(arg)
    z[:, FILT_EMB] = valid.astype(f32)
    return z


@functools.lru_cache(maxsize=None)
def _decay_rates():
    max_decay = math.log(DECAY_TARGET) / DECAY_FAST
    min_decay = math.log(DECAY_TARGET) / DECAY_SLOW
    return np.abs(np.linspace(min_decay, max_decay, D_B, dtype=np.float32))[None, :]


def _mod_kernel(c_ref, w_ref, b_ref, o_ref):
    c = c_ref[...]
    s = (c * _sigmoid(c)).astype(BF16)
    o_ref[...] = _dot(s, w_ref[...].astype(BF16)) + b_ref[...]


def _modulation(cond8, ada_w, ada_b):
    tn = 1536
    n = N_MOD * D_MODEL
    return pl.pallas_call(
        _mod_kernel,
        grid=(n // tn,),
        in_specs=[pl.BlockSpec((8, D_MODEL), lambda j: (0, 0)),
                  pl.BlockSpec((D_MODEL, tn), lambda j: (0, j)),
                  pl.BlockSpec((1, tn), lambda j: (0, j))],
        out_specs=pl.BlockSpec((8, tn), lambda j: (0, j)),
        out_shape=jax.ShapeDtypeStruct((8, n), F32),
        compiler_params=_params("parallel"),
        name="modulation",
    )(cond8, ada_w, ada_b)


def _lower_bounds(lbl_ref):
    l0 = lbl_ref[0]
    l1 = lbl_ref[1]
    m = jnp.maximum(l0, l1)
    e0 = jnp.exp(l0 - m)
    e1 = jnp.exp(l1 - m)
    return e0 / (e0 + e1)


def _inproj_kernel(x_ref, mod_ref, w_ref, lbl_ref, o_ref, h_ref):
    j = pl.program_id(1)

    @pl.when(j == 0)
    def _():
        h = x_ref[...] * (1.0 + mod_ref[0, 1:2, :]) + mod_ref[0, 0:1, :]
        h_ref[...] = h.astype(BF16)

    def project(act):
        for r in range(x_ref.shape[0] // BLK):
            rows = pl.ds(r * BLK, BLK)
            o_ref[rows, :] = act(_dot(h_ref[rows, :], w_ref[...]))

    is_silu = jnp.logical_or(j == CB_Q, j == CB_G)
    is_sig = j >= CB_GA
    is_gate = jnp.logical_or(j == CB_FF, j == CB_FB)

    @pl.when(is_silu)
    def _():
        project(lambda a: a * _sigmoid(a))

    @pl.when(is_sig)
    def _():
        project(_sigmoid)

    @pl.when(is_gate)
    def _():
        lb2 = _lower_bounds(lbl_ref)
        lb = jnp.where(j == CB_FF, lb2[0:1, :], lb2[1:2, :])
        project(lambda a: jnp.log(lb + (1.0 - lb) * _sigmoid(a)))

    @pl.when(jnp.logical_not(jnp.logical_or(jnp.logical_or(is_silu, is_sig), is_gate)))
    def _():
        project(lambda a: a)


def _inproj(x, mod3, w_bf, lb_logits, cond_row):
    tm = 1024
    t = x.shape[0]
    return pl.pallas_call(
        _inproj_kernel,
        grid=(t // tm, W_IN_COLS // D_MODEL),
        in_specs=[pl.BlockSpec((tm, D_MODEL), lambda i, j: (i, 0)),
                  pl.BlockSpec((1, N_MOD, D_MODEL), lambda i, j: (cond_row(i * tm), 0, 0)),
                  pl.BlockSpec((D_MODEL, D_MODEL), lambda i, j: (0, j)),
                  pl.BlockSpec((2, 2, D_MODEL), lambda i, j: (0, 0, 0))],
        out_specs=pl.BlockSpec((tm, D_MODEL), lambda i, j: (i, j)),
        out_shape=jax.ShapeDtypeStruct((t, W_IN_COLS), F32),
        scratch_shapes=[pltpu.VMEM((tm, D_MODEL), BF16)],
        compiler_params=_params("parallel", "arbitrary"),
        name="inproj",
    )(x, mod3, w_bf, lb_logits)


def _chunk_cumsum(lf, tri):
    lf_hi = lf.astype(BF16)
    lf_lo = (lf - lf_hi.astype(F32)).astype(BF16)
    return _dot(tri, lf_hi) + _dot(tri, lf_lo)


def _hgrn_direction(q, lf, b, v, st, lv, reverse, use_state):
    k = 1.0 - jnp.exp(lf)
    qe = q * jnp.exp(b)
    k0 = k * jnp.exp(-b)

    order = [NCHUNK - 1 - i for i in range(NCHUNK)] if reverse else list(range(NCHUNK))
    chunk_of = {p: i for i, p in enumerate(order)}
    sl = lambda i: slice(i * CHUNK, (i + 1) * CHUNK)
    last_row = lambda i: (i * CHUNK) if reverse else (i * CHUNK + CHUNK - 1)
    c = [None] * NCHUNK
    for p in range(NCHUNK):
        r = last_row(chunk_of[p])
        c[p] = b[r:r + 1, :]
    cum = [jnp.zeros_like(c[0])]
    for p in range(NCHUNK):
        cum.append(cum[p] + c[p])
    total = cum[NCHUNK]

    qe_c, ke_c = {}, {}
    for p in range(NCHUNK):
        i = chunk_of[p]
        qe_c[p] = qe[sl(i), :]
        ke_c[p] = k0[sl(i), :] * jnp.exp(c[p])

    def assemble(parts):
        return jnp.concatenate([parts[order[i]] for i in range(NCHUNK)], axis=0).astype(BF16)

    s_all = _dot_nt(qe.astype(BF16), k0.astype(BF16))
    s_mat = jnp.where(lv == 0, s_all, 0.0)
    zero = jnp.zeros((CHUNK, LANE), F32)
    nlev = NCHUNK.bit_length() - 1
    for lev in range(1, nlev + 1):
        half = 1 << (lev - 1)
        qp, kp = {}, {}
        for p in range(NCHUNK):
            pm = ((p >> lev) << lev) + half
            if p >= pm:
                qp[p] = qe_c[p] * jnp.exp(cum[p] - cum[pm])
                kp[p] = zero
            else:
                qp[p] = zero
                kp[p] = ke_c[p] * jnp.exp(cum[pm] - cum[p + 1])
        s_lev = _dot_nt(assemble(qp), assemble(kp))
        s_mat = jnp.where(lv == lev, s_lev, s_mat)

    v_bf = v.astype(BF16)
    out = _dot(s_mat.astype(BF16), v_bf)
    if use_state:
        q_start = assemble({p: qe_c[p] * jnp.exp(cum[p]) for p in range(NCHUNK)})
        out = out + _dot_nt(q_start, st.astype(BF16))
    k_end = assemble({p: ke_c[p] * jnp.exp(total - cum[p + 1]) for p in range(NCHUNK)})
    upd = _dot_tn(v_bf, k_end)
    new_st = st * jnp.exp(total) + upd if use_state else upd
    return out, new_st


def _hgrn_kernel(*refs, nb, zero_init, emit_state):
    it = iter(refs)
    qf_ref, lff_ref, vf_ref, qb_ref, lfb_ref, vb_ref = [next(it) for _ in range(6)]
    s0_ref = None if zero_init else next(it)
    lvf_ref, lvb_ref, trif_ref, trib_ref = [next(it) for _ in range(4)]
    of_ref, ob_ref = next(it), next(it)
    so_ref = next(it) if emit_state else None
    st_ref = next(it)
    i = pl.program_id(2)

    use_state = not (zero_init and nb == 1)
    if use_state:
        @pl.when(i == 0)
        def _():
            for d in range(2):
                for h in range(HPS):
                    st_ref[d, h] = jnp.zeros((DV, DK), F32) if zero_init else s0_ref[0, 0, d, h].T

    lf_f = lff_ref[...]
    lf_b = lfb_ref[...]
    b_f = _chunk_cumsum(lf_f, trif_ref[...])
    b_b = _chunk_cumsum(lf_b, trib_ref[...])
    lv_f = lvf_ref[...]
    lv_b = lvb_ref[...]
    for h in range(HPS):
        hs = slice(h * LANE, (h + 1) * LANE)
        of, stf = _hgrn_direction(qf_ref[:, hs], lf_f[:, hs], b_f[:, hs], vf_ref[:, hs], st_ref[0, h],
                                  lv_f, False, use_state)
        ob, stb = _hgrn_direction(qb_ref[:, hs], lf_b[:, hs], b_b[:, hs], vb_ref[:, hs], st_ref[1, h],
                                  lv_b, True, use_state)
        of_ref[:, hs] = of
        ob_ref[:, hs] = ob
        if nb > 1:
            st_ref[0, h] = stf
            st_ref[1, h] = stb
        if emit_state:
            @pl.when(i == nb - 1)
            def _():
                so_ref[0, 0, 0, h] = stf.T
                so_ref[0, 0, 1, h] = stb.T


def _hgrn(proj, state, nseq, nb, emit_state):
    zero_init = state is None
    t = proj.shape[0]
    tri_f, tri_b, lv_f, lv_b = _scan_consts()
    wid = HPS * LANE
    per = D_MODEL // wid
    fwd = lambda cb: pl.BlockSpec((BLK, wid), lambda b, h, i, cb=cb: (b * nb + i, cb * per + h))
    bwd = lambda cb: pl.BlockSpec((BLK, wid), lambda b, h, i, cb=cb: (b * nb + nb - 1 - i, cb * per + h))
    const = lambda: pl.BlockSpec((BLK, BLK), lambda b, h, i: (0, 0))
    st_spec = pl.BlockSpec((1, 1, 2, HPS, DK, DV), lambda b, h, i: (b, 0, 0, h, 0, 0))
    in_specs = [fwd(CB_Q), fwd(CB_FF), fwd(CB_I), bwd(CB_Q), bwd(CB_FB), bwd(CB_I)]
    args = [proj] * 6
    if not zero_init:
        in_specs.append(st_spec)
        args.append(state)
    in_specs += [const(), const(), const(), const()]
    args += [jnp.asarray(lv_f), jnp.asarray(lv_b), jnp.asarray(tri_f, BF16), jnp.asarray(tri_b, BF16)]
    out_specs = [pl.BlockSpec((BLK, wid), lambda b, h, i: (b * nb + i, h)),
                 pl.BlockSpec((BLK, wid), lambda b, h, i: (b * nb + nb - 1 - i, h))]
    out_shape = [jax.ShapeDtypeStruct((t, D_MODEL), F32), jax.ShapeDtypeStruct((t, D_MODEL), F32)]
    if emit_state:
        out_specs.append(st_spec)
        out_shape.append(jax.ShapeDtypeStruct((nseq, DEPTH, 2, H_A, DK, DV), F32))
    return pl.pallas_call(
        functools.partial(_hgrn_kernel, nb=nb, zero_init=zero_init, emit_state=emit_state),
        grid=(nseq, H_A // HPS, nb),
        in_specs=in_specs,
        out_specs=out_specs,
        out_shape=out_shape,
        scratch_shapes=[pltpu.VMEM((2, HPS, DV, DK), F32)],
        compiler_params=_params("parallel", "parallel", "arbitrary"),
        name="hgrn_scan",
    )(*args)


def _filter_kernel(z_ref, w1_ref, b1_ref, w2_ref, b2_ref, w3_ref, b3_ref, fq_ref, w4_ref, dec_ref,
                   fhi_ref, flo_ref, sgn_ref, o_ref, prev_ref):
    zp = z_ref[...]
    h = jnp.sin(fq_ref[0:1, :] * (_dot_hi(zp, w1_ref[...]) + b1_ref[...]))
    h = jnp.sin(fq_ref[1:2, :] * (_dot_hi(h, w2_ref[...]) + b2_ref[...]))
    h = jnp.sin(fq_ref[2:3, :] * (_dot_hi(h, w3_ref[...]) + b3_ref[...]))
    a = _dot_hi(h, w4_ref[...])
    window = jnp.exp(-zp[:, 0:1] * dec_ref[...]) + DECAY_SHIFT
    a = a * window * zp[:, FILT_EMB:FILT_EMB + 1]
    a_hi = a.astype(BF16)
    a_lo = (a - a_hi.astype(F32)).astype(BF16)
    f_hi = fhi_ref[...]
    ah = _dot(f_hi, a_hi) + (_dot(f_hi, a_lo) + _dot(flo_ref[...], a_hi))
    o_ref[0] = ah + sgn_ref[...] * prev_ref[...]
    prev_ref[...] = ah


def _filter_spectra(seq_len, w1, b1, w2, b2, w3, b3, freq, w4):
    nb = seq_len // BLK
    fwd_dft, _, sgn = _dft_consts()
    zpos = jnp.asarray(_filter_positions(seq_len))
    f_full = jnp.asarray(fwd_dft)
    f_hi = f_full.astype(BF16)
    f_lo = (f_full - f_hi.astype(F32)).astype(BF16)
    pad2 = lambda w: jnp.pad(w, ((0, LANE - w.shape[0]), (0, LANE - w.shape[1])))
    padv = lambda b: jnp.pad(b, (0, LANE - b.shape[0]))[None, :]
    w1p, w2p, w3p = pad2(w1), pad2(w2), pad2(w3)
    b1p, b2p, b3p = padv(b1), padv(b2), padv(b3)
    fqp = jnp.pad(freq, ((0, 0), (0, LANE - freq.shape[1])))
    w4p = jnp.pad(w4, ((0, LANE - w4.shape[0]), (0, 0)))
    small = lambda shape: pl.BlockSpec(shape, lambda m: (0, 0))
    return pl.pallas_call(
        _filter_kernel,
        grid=(2 * nb,),
        in_specs=[pl.BlockSpec((BLK, LANE), lambda m: (m, 0)),
                  small((LANE, LANE)), small((1, LANE)), small((LANE, LANE)), small((1, LANE)),
                  small((LANE, LANE)), small((1, LANE)), small((3, LANE)),
                  pl.BlockSpec((LANE, D_B), lambda m: (0, jnp.where(m < nb, 1, 0))),
                  small((1, D_B)), small((NFREQ, BLK)), small((NFREQ, BLK)), small((NFREQ, 1))],
        out_specs=pl.BlockSpec((1, NFREQ, D_B), lambda m: (jnp.maximum(m - 1, 0), 0, 0)),
        out_shape=jax.ShapeDtypeStruct((2 * nb - 1, NFREQ, D_B), F32),
        scratch_shapes=[pltpu.VMEM((NFREQ, D_B), F32)],
        compiler_params=_params("arbitrary"),
        name="hyena_filter",
    )(zpos, w1p, b1p, w2p, b2p, w3p, b3p, fqp, w4p, jnp.asarray(_decay_rates()),
      f_hi, f_lo, jnp.asarray(sgn))


def _short_conv_gate(x0_ref, x1_ref, v_ref, cw_ref, cb_ref, row_len):
    t = lax.broadcasted_iota(jnp.int32, (BLK, 1), 0)
    first = (t % row_len) == 0
    last = (t % row_len) == (row_len - 1)

    def conv(u, cb):
        lo = cb * D_B
        w = cw_ref[:, lo:lo + D_B]
        up = jnp.where(first, 0.0, pltpu.roll(u, 1, 0))
        dn = jnp.where(last, 0.0, pltpu.roll(u, BLK - 1, 0))
        return up * w[0:1, :] + u * w[1:2, :] + dn * w[2:3, :] + cb_ref[:, lo:lo + D_B]

    return conv(x0_ref[...], 0), conv(v_ref[...], 2) * conv(x1_ref[...], 1)


def _hy1_kernel(x0_ref, x1_ref, v_ref, cw_ref, cb_ref, f_ref, z_ref, x0o_ref, zh_ref, *, row_len):
    x0, z = _short_conv_gate(x0_ref, x1_ref, v_ref, cw_ref, cb_ref, row_len)
    x0o_ref[...] = x0
    z_ref[...] = z
    zh_ref[0] = _dot(f_ref[...], z.astype(BF16))


def _hy_single_kernel(x0_ref, x1_ref, v_ref, cw_ref, cb_ref, f_ref, kh_ref, skip_ref, g_ref, o_ref, *, row_len):
    x0, z = _short_conv_gate(x0_ref, x1_ref, v_ref, cw_ref, cb_ref, row_len)
    zh = _dot(f_ref[...], z.astype(BF16))
    zr, zi = zh[:BLK, :], zh[BLK:, :]
    kr, ki = kh_ref[0, :BLK, :], kh_ref[0, BLK:, :]
    p = zr * kr
    q = zi * ki
    r = zr * ki + zi * kr
    row0 = lax.broadcasted_iota(jnp.int32, (BLK, 1), 0) == 0
    yh = jnp.concatenate([jnp.where(row0, p, p - q), jnp.where(row0, q, r)], axis=0).astype(BF16)
    y = _dot(g_ref[...], yh) + z * skip_ref[...]
    o_ref[...] = x0 * y


def _hyena_single(proj, khat, conv_w, conv_b, skip, row_len):
    t = proj.shape[0]
    fwd_dft, inv_dft, _ = _dft_consts()
    col = lambda cb: pl.BlockSpec((BLK, D_B), lambda i, cb=cb: (i, cb))
    return pl.pallas_call(
        functools.partial(_hy_single_kernel, row_len=row_len),
        grid=(t // BLK,),
        in_specs=[col(CB_X0), col(CB_X1), col(CB_V),
                  pl.BlockSpec((3, 3 * D_B), lambda i: (0, 0)),
                  pl.BlockSpec((1, 3 * D_B), lambda i: (0, 0)),
                  pl.BlockSpec((NFREQ, BLK), lambda i: (0, 0)),
                  pl.BlockSpec((1, NFREQ, D_B), lambda i: (0, 0, 0)),
                  pl.BlockSpec((1, D_B), lambda i: (0, 0)),
                  pl.BlockSpec((BLK, NFREQ), lambda i: (0, 0))],
        out_specs=pl.BlockSpec((BLK, D_B), lambda i: (i, 0)),
        out_shape=jax.ShapeDtypeStruct((t, D_B), F32),
        compiler_params=_params("parallel"),
        name="hyena_single",
    )(proj, proj, proj, conv_w, conv_b[None, :], jnp.asarray(fwd_dft).astype(BF16), khat, skip[None, :],
      jnp.asarray(inv_dft).astype(BF16))


def _hyena_front(proj, conv_w, conv_b, row_len):
    t = proj.shape[0]
    fwd_dft, _, _ = _dft_consts()
    col = lambda cb: pl.BlockSpec((BLK, D_B), lambda i, cb=cb: (i, cb))
    return pl.pallas_call(
        functools.partial(_hy1_kernel, row_len=row_len),
        grid=(t // BLK,),
        in_specs=[col(CB_X0), col(CB_X1), col(CB_V),
                  pl.BlockSpec((3, 3 * D_B), lambda i: (0, 0)),
                  pl.BlockSpec((1, 3 * D_B), lambda i: (0, 0)),
                  pl.BlockSpec((NFREQ, BLK), lambda i: (0, 0))],
        out_specs=[pl.BlockSpec((BLK, D_B), lambda i: (i, 0)),
                   pl.BlockSpec((BLK, D_B), lambda i: (i, 0)),
                   pl.BlockSpec((1, NFREQ, D_B), lambda i: (i, 0, 0))],
        out_shape=[jax.ShapeDtypeStruct((t, D_B), F32), jax.ShapeDtypeStruct((t, D_B), F32),
                   jax.ShapeDtypeStruct((t // BLK, NFREQ, D_B), F32)],
        compiler_params=_params("parallel"),
        name="hyena_front",
    )(proj, proj, proj, conv_w, conv_b[None, :], jnp.asarray(fwd_dft).astype(BF16))


ROWG = 64


def _hy2_kernel(zh_ref, kh_ref, z_ref, x0_ref, skip_ref, g_ref, o_ref, yh_ref, *, nb):
    i = pl.program_id(2)
    dt = o_ref.shape[1]
    row0 = lax.broadcasted_iota(jnp.int32, (ROWG, 1), 0) == 0
    for rg in range(BLK // ROWG):
        re = pl.ds(rg * ROWG, ROWG)
        im = pl.ds(BLK + rg * ROWG, ROWG)

        def body(j, acc):
            p, q, r = acc
            kidx = i - j + (nb - 1)
            zr = zh_ref[0, j, re, :]
            zi = zh_ref[0, j, im, :]
            kr = kh_ref[kidx, re, :]
            ki = kh_ref[kidx, im, :]
            return (p + zr * kr, q + zi * ki, r + (zr * ki + zi * kr))

        zeros = jnp.zeros((ROWG, dt), F32)
        p, q, r = lax.fori_loop(0, nb, body, (zeros, zeros, zeros), unroll=True)
        if rg == 0:
            yr = jnp.where(row0, p, p - q)
            yi = jnp.where(row0, q, r)
        else:
            yr = p - q
            yi = r
        yh_ref[re, :] = yr.astype(BF16)
        yh_ref[im, :] = yi.astype(BF16)
    z = z_ref[...]
    y = _dot(g_ref[...], yh_ref[...]) + z * skip_ref[...]
    o_ref[...] = x0_ref[...] * y


def _hyena_back(zh, khat, z, x0, skip, nseq, nb):
    t = z.shape[0]
    dt = LANE
    _, inv_dft, _ = _dft_consts()
    zh4 = zh.reshape(nseq, nb, NFREQ, D_B)
    tok = pl.BlockSpec((BLK, dt), lambda d, b, i: (b * nb + i, d))
    return pl.pallas_call(
        functools.partial(_hy2_kernel, nb=nb),
        grid=(D_B // dt, nseq, nb),
        in_specs=[pl.BlockSpec((1, nb, NFREQ, dt), lambda d, b, i: (b, 0, 0, d)),
                  pl.BlockSpec((2 * nb - 1, NFREQ, dt), lambda d, b, i: (0, 0, d)),
                  tok, tok,
                  pl.BlockSpec((1, dt), lambda d, b, i: (0, d)),
                  pl.BlockSpec((BLK, NFREQ), lambda d, b, i: (0, 0))],
        out_specs=tok,
        out_shape=jax.ShapeDtypeStruct((t, D_B), F32),
        scratch_shapes=[pltpu.VMEM((NFREQ, dt), BF16)],
        compiler_params=_params("parallel", "parallel", "arbitrary"),
        name="hyena_back",
    )(zh4, khat, z, x0, skip[None, :], jnp.asarray(inv_dft).astype(BF16))


def _layer_norm(y, g, b):
    mu = jnp.mean(y, axis=-1, keepdims=True)
    yc = y - mu
    var = jnp.mean(yc * yc, axis=-1, keepdims=True)
    return yc * lax.rsqrt(var + LN_EPS) * g + b


def _merge_kernel(of_ref, ob_ref, g_ref, ga_ref, gb_ref, hy_ref, x_ref, mod_ref, nw_ref,
                  pa_ref, pb_ref, wo_ref, lg_ref, lb_ref, o_ref):
    o = of_ref[...] + ob_ref[...]
    nw = nw_ref[...]
    parts = []
    for h in range(H_A):
        oh = o[:, h * DV:(h + 1) * DV]
        ms = jnp.mean(oh * oh, axis=-1, keepdims=True)
        parts.append(oh * lax.rsqrt(ms + RMS_EPS) * nw)
    oa = jnp.concatenate(parts, axis=1) * g_ref[...]
    a = _dot(oa.astype(BF16), pa_ref[...])
    b = _dot(hy_ref[...].astype(BF16), pb_ref[...])
    merged = ga_ref[...] * a + gb_ref[...] * b
    mix = _dot(merged.astype(BF16), wo_ref[...])
    y = ALPHA * x_ref[...] + mod_ref[0, 2:3, :] * mix
    o_ref[...] = _layer_norm(y, lg_ref[...], lb_ref[...])


def _merge(o_f, o_b, proj, o_hy, x, mod3, cond_row, norm_w, pa, pb, wo, ln_g, ln_b):
    tm = 256
    t = x.shape[0]
    tok = pl.BlockSpec((tm, D_MODEL), lambda i: (i, 0))
    col = lambda cb: pl.BlockSpec((tm, D_MODEL), lambda i, cb=cb: (i, cb))
    mat = pl.BlockSpec((D_MODEL, D_MODEL), lambda i: (0, 0))
    vec = pl.BlockSpec((1, D_MODEL), lambda i: (0, 0))
    return pl.pallas_call(
        _merge_kernel,
        grid=(t // tm,),
        in_specs=[tok, tok, col(CB_G), col(CB_GA), col(CB_GB), tok, tok,
                  pl.BlockSpec((1, N_MOD, D_MODEL), lambda i: (cond_row(i * tm), 0, 0)),
                  pl.BlockSpec((1, DV), lambda i: (0, 0)),
                  mat, mat, mat, vec, vec],
        out_specs=tok,
        out_shape=jax.ShapeDtypeStruct((t, D_MODEL), F32),
        compiler_params=_params("parallel"),
        name="merge",
    )(o_f, o_b, proj, proj, proj, o_hy, x, mod3, norm_w[None, :], pa, pb, wo, ln_g[None, :], ln_b[None, :])


def _ffn_kernel(x_ref, mod_ref, wg_ref, wu_ref, wo_ref, lg_ref, lb_ref, o_ref):
    x = x_ref[...]
    h = (x * (1.0 + mod_ref[0, 4:5, :]) + mod_ref[0, 3:4, :]).astype(BF16)
    gt = _dot(h, wg_ref[...])
    up = _dot(h, wu_ref[...])
    act = (gt * _sigmoid(gt) * up).astype(BF16)
    ff = _dot(act, wo_ref[...])
    y = ALPHA * x + mod_ref[0, 5:6, :] * ff
    o_ref[...] = _layer_norm(y, lg_ref[...], lb_ref[...])


def _ffn(x, mod3, cond_row, w_in, w_out, ln_g, ln_b):
    tm = 256
    t = x.shape[0]
    tok = pl.BlockSpec((tm, D_MODEL), lambda i: (i, 0))
    vec = pl.BlockSpec((1, D_MODEL), lambda i: (0, 0))
    return pl.pallas_call(
        _ffn_kernel,
        grid=(t // tm,),
        in_specs=[tok,
                  pl.BlockSpec((1, N_MOD, D_MODEL), lambda i: (cond_row(i * tm), 0, 0)),
                  pl.BlockSpec((D_MODEL, D_FF), lambda i: (0, 0)),
                  pl.BlockSpec((D_MODEL, D_FF), lambda i: (0, 1)),
                  pl.BlockSpec((D_FF, D_MODEL), lambda i: (0, 0)),
                  vec, vec],
        out_specs=tok,
        out_shape=jax.ShapeDtypeStruct((t, D_MODEL), F32),
        compiler_params=_params("parallel"),
        name="ffn",
    )(x, mod3, w_in, w_in, w_out, ln_g[None, :], ln_b[None, :])


def _trunk(x, mod3, cond_row, nseq, seq_len, row_len, state, emit_state, w, lb_logits):
    nb = seq_len // BLK
    proj = _inproj(x, mod3, w["w_in"], lb_logits, cond_row)
    hg = _hgrn(proj, state, nseq, nb, emit_state)
    o_f, o_b = hg[0], hg[1]
    khat = _filter_spectra(seq_len, w["filt_w1"], w["filt_b1"], w["filt_w2"], w["filt_b2"],
                           w["filt_w3"], w["filt_b3"], w["filt_freq"], w["filt_w4"])
    if nb == 1:
        o_hy = _hyena_single(proj, khat, w["hy_conv_w"], w["hy_conv_b"], w["hy_skip"], row_len)
    else:
        z, x0, zh = _hyena_front(proj, w["hy_conv_w"], w["hy_conv_b"], row_len)
        o_hy = _hyena_back(zh, khat, z, x0, w["hy_skip"], nseq, nb)
    x1 = _merge(o_f, o_b, proj, o_hy, x, mod3, cond_row, w["hgrn_norm_w"], w["proj_a"], w["proj_b"],
                w["w_out"], w["ln1_g"], w["ln1_b"])
    x2 = _ffn(x1, mod3, cond_row, w["ffn_w_in"], w["ffn_w_out"], w["ln2_g"], w["ln2_b"])
    return x2, (hg[2] if emit_state else None)


def kernel(x_prompt, x_sample, state_hgrn, c, c_ctx, ada_w, ada_b, w_in, hgrn_lb_logits, hgrn_norm_w,
           hy_conv_w, hy_conv_b, filt_w1, filt_b1, filt_w2, filt_b2, filt_w3, filt_b3, filt_freq, filt_w4,
           hy_skip, proj_a, proj_b, w_out, ln1_g, ln1_b, ffn_w_in, ffn_w_out, ln2_g, ln2_b):
    assert ada_w.shape[0] == DEPTH == 1
    batch, seq, _ = x_prompt.shape
    dec_batch, dec_seq, _ = x_sample.shape
    assert seq % BLK == 0 and dec_seq % BLK == 0 and BLK % GRID_W == 0 and dec_batch + 1 <= 8

    w = dict(w_in=w_in[0].astype(BF16), hy_conv_w=hy_conv_w[0], hy_conv_b=hy_conv_b[0],
             filt_w1=filt_w1[0], filt_b1=filt_b1[0], filt_w2=filt_w2[0], filt_b2=filt_b2[0],
             filt_w3=filt_w3[0], filt_b3=filt_b3[0], filt_freq=filt_freq[0], filt_w4=filt_w4[0],
             hy_skip=hy_skip[0], hgrn_norm_w=hgrn_norm_w[0], proj_a=proj_a[0].astype(BF16),
             proj_b=proj_b[0].astype(BF16), w_out=w_out[0].astype(BF16), ln1_g=ln1_g[0], ln1_b=ln1_b[0],
             ffn_w_in=ffn_w_in[0].astype(BF16), ffn_w_out=ffn_w_out[0].astype(BF16),
             ln2_g=ln2_g[0], ln2_b=ln2_b[0])

    cond8 = jnp.zeros((8, D_MODEL), F32).at[0].set(c_ctx).at[1:1 + dec_batch].set(c)
    mod3 = _modulation(cond8, ada_w[0], ada_b[0][None, :]).reshape(8, N_MOD, D_MODEL)

    xp = x_prompt.reshape(batch * seq, D_MODEL)
    xs = x_sample.reshape(dec_batch * dec_seq, D_MODEL)
    yp, new_state = _trunk(xp, mod3, lambda tok: 0, batch, seq, seq, None, True, w, hgrn_lb_logits)
    ys, _ = _trunk(xs, mod3, lambda tok: 1 + tok // dec_seq, dec_batch, dec_seq, GRID_W, state_hgrn, False,
                   w, hgrn_lb_logits)
    return (yp.reshape(batch, seq, D_MODEL), ys.reshape(dec_batch, dec_seq, D_MODEL), new_state)
```

```python
import functools
import math

import numpy as np
import jax
import jax.numpy as jnp
from jax import lax
from jax.experimental import pallas as pl
from jax.experimental.pallas import tpu as pltpu

F32 = jnp.float32
BF16 = jnp.bfloat16

D_MODEL = 1024
DEPTH = 1
GRID_W = 64
H_A = 8
DK = 128
DV = 128
D_B = 1024
FILT_EMB = 33
FILT_BANDS = 16
FILT_ORDER = 64
DECAY_FAST = 0.3
DECAY_SLOW = 1.5
DECAY_TARGET = 1e-2
DECAY_SHIFT = 0.05
D_FF = 2816
N_MOD = 6
W_IN_COLS = 10 * D_MODEL
ALPHA = (2.0 * DEPTH) ** 0.25
LN_EPS = 1e-5
RMS_EPS = 1e-6

LANE = 128
BLK = 256
NFREQ = 2 * BLK
CHUNK = 32
NCHUNK = BLK // CHUNK
HPS = 2
VMEM_LIMIT = 56 * 1024 * 1024

CB_Q, CB_FF, CB_FB, CB_I, CB_G, CB_X0, CB_X1, CB_V, CB_GA, CB_GB = range(10)
STEP_COLS = (CB_Q, CB_FF, CB_FB, CB_G, CB_GA, CB_GB, CB_I, CB_X0, CB_X1, CB_V)
OF_Q, OF_FF, OF_FB, OF_G, OF_GA, OF_GB = range(6)
N_F32_COLS = 6
OB_I, OB_X0, OB_X1, OB_V = range(4)


def _sigmoid(x):
    return 1.0 / (1.0 + jnp.exp(-x))


def _dot(a, b):
    return jnp.dot(a, b, preferred_element_type=F32)


def _dot_nt(a, b):
    return lax.dot_general(a, b, (((1,), (1,)), ((), ())), preferred_element_type=F32)


def _dot_tn(a, b):
    return lax.dot_general(a, b, (((0,), (0,)), ((), ())), preferred_element_type=F32)


def _dot_hi(a, b):
    return jnp.dot(a, b, preferred_element_type=F32, precision=lax.Precision.HIGHEST)


def _params(*sem):
    return pltpu.CompilerParams(dimension_semantics=sem, vmem_limit_bytes=VMEM_LIMIT)


@functools.lru_cache(maxsize=None)
def _dft_consts():
    n = np.arange(BLK, dtype=np.float64)
    f = np.arange(BLK, dtype=np.float64)
    ang = 2.0 * np.pi * np.outer(f, n) / NFREQ
    fwd = np.zeros((NFREQ, BLK), np.float64)
    fwd[:BLK] = np.cos(ang)
    fwd[BLK + 1:] = -np.sin(ang[1:])
    fwd[BLK] = np.cos(np.pi * n)
    inv = np.zeros((BLK, NFREQ), np.float64)
    scale = np.full((BLK,), 2.0)
    scale[0] = 1.0
    inv[:, :BLK] = np.cos(ang.T) * scale[None, :]
    inv[:, BLK + 1:] = -2.0 * np.sin(ang.T[:, 1:])
    inv[:, BLK] = np.cos(np.pi * n)
    inv /= NFREQ
    fr = np.arange(NFREQ)
    freq_of_row = np.where(fr < BLK, fr, np.where(fr == BLK, BLK, fr - BLK))
    sgn = np.where(freq_of_row % 2 == 0, 1.0, -1.0)[:, None]
    return fwd.astype(np.float32), inv.astype(np.float32), sgn.astype(np.float32)


@functools.lru_cache(maxsize=None)
def _scan_consts():
    t = np.arange(BLK)
    ct = t // CHUNK
    same = ct[:, None] == ct[None, :]
    tri_f = (same & (t[None, :] <= t[:, None])).astype(np.float32)
    tri_b = (same & (t[None, :] >= t[:, None])).astype(np.float32)

    def levels(p, diag):
        x = p[:, None] ^ p[None, :]
        lvl = np.zeros_like(x)
        for bit in range(1, NCHUNK.bit_length()):
            lvl = np.where(x >= (1 << (bit - 1)), bit, lvl)
        lv = np.where(p[:, None] > p[None, :], lvl, -1)
        return np.where(same, np.where(diag, 0, -1), lv).astype(np.int32)

    lv_f = levels(ct, t[None, :] <= t[:, None])
    lv_b = levels(NCHUNK - 1 - ct, t[None, :] >= t[:, None])
    return tri_f, tri_b, lv_f, lv_b


@functools.lru_cache(maxsize=None)
def _filter_positions(seq_len):
    f32 = np.float32
    j = np.arange(-seq_len, seq_len)
    p = np.abs(j)
    valid = (j > -seq_len)
    pc = np.minimum(p, seq_len - 1)
    t = np.linspace(0.0, 1.0, seq_len, dtype=f32)[pc]
    wpos = (f32(2.0 * math.pi / seq_len) * np.arange(seq_len, dtype=f32))[pc]
    bands = np.linspace(1e-4, FILT_BANDS - 1, FILT_BANDS, dtype=f32)
    arg = (bands[None, :] * wpos[:, None]).astype(f32)
    z = np.zeros((2 * seq_len, LANE), f32)
    z[:, 0] = t
    z[:, 1:1 + FILT_BANDS] = np.cos(arg)
    z[:, 1 + FILT_BANDS:FILT_EMB] = -np.sin(arg)
    z[:, FILT_EMB] = valid.astype(f32)
    return z


@functools.lru_cache(maxsize=None)
def _decay_rates():
    max_decay = math.log(DECAY_TARGET) / DECAY_FAST
    min_decay = math.log(DECAY_TARGET) / DECAY_SLOW
    return np.abs(np.linspace(min_decay, max_decay, D_B, dtype=np.float32))[None, :]


def _mod_kernel(c_ref, w_ref, b_ref, o_ref):
    c = c_ref[...]
    s = (c * _sigmoid(c)).astype(BF16)
    o_ref[...] = _dot(s, w_ref[...].astype(BF16)) + b_ref[...]


def _modulation(cond8, ada_w, ada_b):
    tn = 1536
    n = N_MOD * D_MODEL
    return pl.pallas_call(
        _mod_kernel,
        grid=(n // tn,),
        in_specs=[pl.BlockSpec((8, D_MODEL), lambda j: (0, 0)),
                  pl.BlockSpec((D_MODEL, tn), lambda j: (0, j)),
                  pl.BlockSpec((1, tn), lambda j: (0, j))],
        out_specs=pl.BlockSpec((8, tn), lambda j: (0, j)),
        out_shape=jax.ShapeDtypeStruct((8, n), F32),
        compiler_params=_params("parallel"),
        name="modulation",
    )(cond8, ada_w, ada_b)


def _lower_bounds(lbl_ref):
    l0 = lbl_ref[0]
    l1 = lbl_ref[1]
    m = jnp.maximum(l0, l1)
    e0 = jnp.exp(l0 - m)
    e1 = jnp.exp(l1 - m)
    return e0 / (e0 + e1)


def _inproj_kernel(x_ref, mod_ref, w_ref, lbl_ref, of_ref, ob_ref, h_ref):
    j = pl.program_id(1)

    @pl.when(j == 0)
    def _():
        h = x_ref[...] * (1.0 + mod_ref[0, 1:2, :]) + mod_ref[0, 0:1, :]
        h_ref[...] = h.astype(BF16)

    def project(o_ref, act):
        for r in range(x_ref.shape[0] // BLK):
            rows = pl.ds(r * BLK, BLK)
            o_ref[rows, :] = act(_dot(h_ref[rows, :], w_ref[...])).astype(o_ref.dtype)

    @pl.when(jnp.logical_or(j == OF_Q, j == OF_G))
    def _():
        project(of_ref, lambda a: a * _sigmoid(a))

    @pl.when(jnp.logical_or(j == OF_GA, j == OF_GB))
    def _():
        project(of_ref, _sigmoid)

    @pl.when(jnp.logical_or(j == OF_FF, j == OF_FB))
    def _():
        lb2 = _lower_bounds(lbl_ref)
        lb = jnp.where(j == OF_FF, lb2[0:1, :], lb2[1:2, :])
        project(of_ref, lambda a: jnp.log(lb + (1.0 - lb) * _sigmoid(a)))

    @pl.when(j >= N_F32_COLS)
    def _():
        project(ob_ref, lambda a: a)


def _weight_col(j):
    col = jnp.int32(STEP_COLS[-1])
    for step in reversed(range(len(STEP_COLS) - 1)):
        col = jnp.where(j == step, STEP_COLS[step], col)
    return col


def _inproj(x, mod3, w_bf, lb_logits, cond_row):
    tm = 2048
    t = x.shape[0]
    n_b16 = len(STEP_COLS) - N_F32_COLS
    return pl.pallas_call(
        _inproj_kernel,
        grid=(t // tm, len(STEP_COLS)),
        in_specs=[pl.BlockSpec((tm, D_MODEL), lambda i, j: (i, 0), pipeline_mode=pl.Buffered(1)),
                  pl.BlockSpec((1, N_MOD, D_MODEL), lambda i, j: (cond_row(i * tm), 0, 0)),
                  pl.BlockSpec((D_MODEL, D_MODEL), lambda i, j: (0, _weight_col(j))),
                  pl.BlockSpec((2, 2, D_MODEL), lambda i, j: (0, 0, 0))],
        out_specs=[pl.BlockSpec((tm, D_MODEL), lambda i, j: (i, jnp.minimum(j, N_F32_COLS - 1))),
                   pl.BlockSpec((tm, D_MODEL), lambda i, j: (i, jnp.maximum(j - N_F32_COLS, 0)))],
        out_shape=[jax.ShapeDtypeStruct((t, N_F32_COLS * D_MODEL), F32),
                   jax.ShapeDtypeStruct((t, n_b16 * D_MODEL), BF16)],
        scratch_shapes=[pltpu.VMEM((tm, D_MODEL), BF16)],
        compiler_params=_params("parallel", "arbitrary"),
        name="inproj",
    )(x, mod3, w_bf, lb_logits)


def _chunk_cumsum(lf, tri):
    lf_hi = lf.astype(BF16)
    lf_lo = (lf - lf_hi.astype(F32)).astype(BF16)
    return _dot(tri, lf_hi) + _dot(tri, lf_lo)


def _hgrn_direction(q, lf, b, v, st, lv, reverse, use_state):
    k = 1.0 - jnp.exp(lf)
    qe = q * jnp.exp(b)
    k0 = k * jnp.exp(-b)

    order = [NCHUNK - 1 - i for i in range(NCHUNK)] if reverse else list(range(NCHUNK))
    chunk_of = {p: i for i, p in enumerate(order)}
    sl = lambda i: slice(i * CHUNK, (i + 1) * CHUNK)
    last_row = lambda i: (i * CHUNK) if reverse else (i * CHUNK + CHUNK - 1)
    c = [None] * NCHUNK
    for p in range(NCHUNK):
        r = last_row(chunk_of[p])
        c[p] = b[r:r + 1, :]
    cum = [jnp.zeros_like(c[0])]
    for p in range(NCHUNK):
        cum.append(cum[p] + c[p])
    total = cum[NCHUNK]

    qe_c, ke_c = {}, {}
    for p in range(NCHUNK):
        i = chunk_of[p]
        qe_c[p] = qe[sl(i), :]
        ke_c[p] = k0[sl(i), :] * jnp.exp(c[p])

    def assemble(parts):
        return jnp.concatenate([parts[order[i]] for i in range(NCHUNK)], axis=0).astype(BF16)

    s_all = _dot_nt(qe.astype(BF16), k0.astype(BF16))
    s_mat = jnp.where(lv == 0, s_all, 0.0)
    zero = jnp.zeros((CHUNK, LANE), F32)
    nlev = NCHUNK.bit_length() - 1
    for lev in range(1, nlev + 1):
        half = 1 << (lev - 1)
        qp, kp = {}, {}
        for p in range(NCHUNK):
            pm = ((p >> lev) << lev) + half
            if p >= pm:
                qp[p] = qe_c[p] * jnp.exp(cum[p] - cum[pm])
                kp[p] = zero
            else:
                qp[p] = zero
                kp[p] = ke_c[p] * jnp.exp(cum[pm] - cum[p + 1])
        s_lev = _dot_nt(assemble(qp), assemble(kp))
        s_mat = jnp.where(lv == lev, s_lev, s_mat)

    v_bf = v.astype(BF16)
    out = _dot(s_mat.astype(BF16), v_bf)
    if use_state:
        q_start = assemble({p: qe_c[p] * jnp.exp(cum[p]) for p in range(NCHUNK)})
        out = out + _dot_nt(q_start, st.astype(BF16))
    k_end = assemble({p: ke_c[p] * jnp.exp(total - cum[p + 1]) for p in range(NCHUNK)})
    upd = _dot_tn(v_bf, k_end)
    new_st = st * jnp.exp(total) + upd if use_state else upd
    return out, new_st


def _hgrn_kernel(*refs, nb, zero_init, emit_state):
    it = iter(refs)
    qf_ref, lff_ref, vf_ref, qb_ref, lfb_ref, vb_ref = [next(it) for _ in range(6)]
    s0_ref = None if zero_init else next(it)
    lvf_ref, lvb_ref, trif_ref, trib_ref = [next(it) for _ in range(4)]
    of_ref, ob_ref = next(it), next(it)
    so_ref = next(it) if emit_state else None
    st_ref = next(it)
    i = pl.program_id(2)

    use_state = not (zero_init and nb == 1)
    if use_state:
        @pl.when(i == 0)
        def _():
            for d in range(2):
                for h in range(HPS):
                    st_ref[d, h] = jnp.zeros((DV, DK), F32) if zero_init else s0_ref[0, 0, d, h].T

    lf_f = lff_ref[...]
    lf_b = lfb_ref[...]
    b_f = _chunk_cumsum(lf_f, trif_ref[...])
    b_b = _chunk_cumsum(lf_b, trib_ref[...])
    lv_f = lvf_ref[...]
    lv_b = lvb_ref[...]
    for h in range(HPS):
        hs = slice(h * LANE, (h + 1) * LANE)
        of, stf = _hgrn_direction(qf_ref[:, hs], lf_f[:, hs], b_f[:, hs], vf_ref[:, hs], st_ref[0, h],
                                  lv_f, False, use_state)
        ob, stb = _hgrn_direction(qb_ref[:, hs], lf_b[:, hs], b_b[:, hs], vb_ref[:, hs], st_ref[1, h],
                                  lv_b, True, use_state)
        of_ref[:, hs] = of
        ob_ref[:, hs] = ob
        if nb > 1:
            st_ref[0, h] = stf
            st_ref[1, h] = stb
        if emit_state:
            @pl.when(i == nb - 1)
            def _():
                so_ref[0, 0, 0, h] = stf.T
                so_ref[0, 0, 1, h] = stb.T


def _hgrn(pf, pb, state, nseq, nb, emit_state):
    zero_init = state is None
    t = pf.shape[0]
    tri_f, tri_b, lv_f, lv_b = _scan_consts()
    wid = HPS * LANE
    per = D_MODEL // wid
    fwd = lambda cb: pl.BlockSpec((BLK, wid), lambda b, h, i, cb=cb: (b * nb + i, cb * per + h))
    bwd = lambda cb: pl.BlockSpec((BLK, wid), lambda b, h, i, cb=cb: (b * nb + nb - 1 - i, cb * per + h))
    const = lambda: pl.BlockSpec((BLK, BLK), lambda b, h, i: (0, 0))
    st_spec = pl.BlockSpec((1, 1, 2, HPS, DK, DV), lambda b, h, i: (b, 0, 0, h, 0, 0))
    in_specs = [fwd(OF_Q), fwd(OF_FF), fwd(OB_I), bwd(OF_Q), bwd(OF_FB), bwd(OB_I)]
    args = [pf, pf, pb, pf, pf, pb]
    if not zero_init:
        in_specs.append(st_spec)
        args.append(state)
    in_specs += [const(), const(), const(), const()]
    args += [jnp.asarray(lv_f), jnp.asarray(lv_b), jnp.asarray(tri_f, BF16), jnp.asarray(tri_b, BF16)]
    out_specs = [pl.BlockSpec((BLK, wid), lambda b, h, i: (b * nb + i, h)),
                 pl.BlockSpec((BLK, wid), lambda b, h, i: (b * nb + nb - 1 - i, h))]
    out_shape = [jax.ShapeDtypeStruct((t, D_MODEL), F32), jax.ShapeDtypeStruct((t, D_MODEL), F32)]
    if emit_state:
        out_specs.append(st_spec)
        out_shape.append(jax.ShapeDtypeStruct((nseq, DEPTH, 2, H_A, DK, DV), F32))
    return pl.pallas_call(
        functools.partial(_hgrn_kernel, nb=nb, zero_init=zero_init, emit_state=emit_state),
        grid=(nseq, H_A // HPS, nb),
        in_specs=in_specs,
        out_specs=out_specs,
        out_shape=out_shape,
        scratch_shapes=[pltpu.VMEM((2, HPS, DV, DK), F32)],
        compiler_params=_params("parallel", "parallel", "arbitrary"),
        name="hgrn_scan",
    )(*args)


def _filter_kernel(z_ref, w1_ref, b1_ref, w2_ref, b2_ref, w3_ref, b3_ref, fq_ref, w4_ref, dec_ref,
                   fhi_ref, flo_ref, sgn_ref, o_ref, prev_ref):
    zp = z_ref[...]
    h = jnp.sin(fq_ref[0:1, :] * (_dot_hi(zp, w1_ref[...]) + b1_ref[...]))
    h = jnp.sin(fq_ref[1:2, :] * (_dot_hi(h, w2_ref[...]) + b2_ref[...]))
    h = jnp.sin(fq_ref[2:3, :] * (_dot_hi(h, w3_ref[...]) + b3_ref[...]))
    a = _dot_hi(h, w4_ref[...])
    window = jnp.exp(-zp[:, 0:1] * dec_ref[...]) + DECAY_SHIFT
    a = a * window * zp[:, FILT_EMB:FILT_EMB + 1]
    a_hi = a.astype(BF16)
    a_lo = (a - a_hi.astype(F32)).astype(BF16)
    f_hi = fhi_ref[...]
    ah = _dot(f_hi, a_hi) + (_dot(f_hi, a_lo) + _dot(flo_ref[...], a_hi))
    o_ref[0] = ah + sgn_ref[...] * prev_ref[...]
    prev_ref[...] = ah


def _filter_spectra(seq_len, w1, b1, w2, b2, w3, b3, freq, w4):
    nb = seq_len // BLK
    fwd_dft, _, sgn = _dft_consts()
    zpos = jnp.asarray(_filter_positions(seq_len))
    f_full = jnp.asarray(fwd_dft)
    f_hi = f_full.astype(BF16)
    f_lo = (f_full - f_hi.astype(F32)).astype(BF16)
    pad2 = lambda w: jnp.pad(w, ((0, LANE - w.shape[0]), (0, LANE - w.shape[1])))
    padv = lambda b: jnp.pad(b, (0, LANE - b.shape[0]))[None, :]
    w1p, w2p, w3p = pad2(w1), pad2(w2), pad2(w3)
    b1p, b2p, b3p = padv(b1), padv(b2), padv(b3)
    fqp = jnp.pad(freq, ((0, 0), (0, LANE - freq.shape[1])))
    w4p = jnp.pad(w4, ((0, LANE - w4.shape[0]), (0, 0)))
    small = lambda shape: pl.BlockSpec(shape, lambda m: (0, 0))
    return pl.pallas_call(
        _filter_kernel,
        grid=(2 * nb,),
        in_specs=[pl.BlockSpec((BLK, LANE), lambda m: (m, 0)),
                  small((LANE, LANE)), small((1, LANE)), small((LANE, LANE)), small((1, LANE)),
                  small((LANE, LANE)), small((1, LANE)), small((3, LANE)),
                  pl.BlockSpec((LANE, D_B), lambda m: (0, jnp.where(m < nb, 1, 0))),
                  small((1, D_B)), small((NFREQ, BLK)), small((NFREQ, BLK)), small((NFREQ, 1))],
        out_specs=pl.BlockSpec((1, NFREQ, D_B), lambda m: (jnp.maximum(m - 1, 0), 0, 0)),
        out_shape=jax.ShapeDtypeStruct((2 * nb - 1, NFREQ, D_B), F32),
        scratch_shapes=[pltpu.VMEM((NFREQ, D_B), F32)],
        compiler_params=_params("arbitrary"),
        name="hyena_filter",
    )(zpos, w1p, b1p, w2p, b2p, w3p, b3p, fqp, w4p, jnp.asarray(_decay_rates()),
      f_hi, f_lo, jnp.asarray(sgn))


def _short_conv_gate(u0, u1, uv, w_refs, b_refs, row_len):
    t = lax.broadcasted_iota(jnp.int32, (BLK, 1), 0)
    first = (t % row_len) == 0
    last = (t % row_len) == (row_len - 1)

    def conv(u, w_ref, b_ref):
        u = u.astype(F32)
        up = jnp.where(first, 0.0, pltpu.roll(u, 1, 0))
        dn = jnp.where(last, 0.0, pltpu.roll(u, BLK - 1, 0))
        return up * w_ref[0:1, :] + u * w_ref[1:2, :] + dn * w_ref[2:3, :] + b_ref[...]

    return (conv(u0, w_refs[0], b_refs[0]),
            conv(uv, w_refs[2], b_refs[2]) * conv(u1, w_refs[1], b_refs[1]))


def _hy_single_kernel(x0_ref, x1_ref, v_ref, w0_ref, w1_ref, wv_ref, b0_ref, b1_ref, bv_ref, f_ref, kh_ref,
                      skip_ref, g_ref, o_ref, *, row_len):
    x0, z = _short_conv_gate(x0_ref[...], x1_ref[...], v_ref[...], (w0_ref, w1_ref, wv_ref),
                             (b0_ref, b1_ref, bv_ref), row_len)
    zh = _dot(f_ref[...], z.astype(BF16))
    zr, zi = zh[:BLK, :], zh[BLK:, :]
    kr, ki = kh_ref[0, :BLK, :], kh_ref[0, BLK:, :]
    p = zr * kr
    q = zi * ki
    r = zr * ki + zi * kr
    row0 = lax.broadcasted_iota(jnp.int32, (BLK, 1), 0) == 0
    yh = jnp.concatenate([jnp.where(row0, p, p - q), jnp.where(row0, q, r)], axis=0).astype(BF16)
    y = _dot(g_ref[...], yh) + z * skip_ref[...]
    o_ref[...] = x0 * y


def _hyena_single(pb, khat, conv_w, conv_b, skip, row_len):
    t = pb.shape[0]
    fwd_dft, inv_dft, _ = _dft_consts()
    col = lambda cb: pl.BlockSpec((BLK, D_B), lambda i, cb=cb: (i, cb))
    wcol = lambda r, k: pl.BlockSpec((r, D_B), lambda i, k=k: (0, k))
    conv_b = conv_b[None, :]
    return pl.pallas_call(
        functools.partial(_hy_single_kernel, row_len=row_len),
        grid=(t // BLK,),
        in_specs=[col(OB_X0), col(OB_X1), col(OB_V),
                  wcol(3, 0), wcol(3, 1), wcol(3, 2), wcol(1, 0), wcol(1, 1), wcol(1, 2),
                  pl.BlockSpec((NFREQ, BLK), lambda i: (0, 0)),
                  pl.BlockSpec((1, NFREQ, D_B), lambda i: (0, 0, 0)),
                  pl.BlockSpec((1, D_B), lambda i: (0, 0)),
                  pl.BlockSpec((BLK, NFREQ), lambda i: (0, 0))],
        out_specs=pl.BlockSpec((BLK, D_B), lambda i: (i, 0)),
        out_shape=jax.ShapeDtypeStruct((t, D_B), F32),
        compiler_params=_params("parallel"),
        name="hyena_single",
    )(pb, pb, pb, conv_w, conv_w, conv_w, conv_b, conv_b, conv_b, jnp.asarray(fwd_dft).astype(BF16), khat,
      skip[None, :], jnp.asarray(inv_dft).astype(BF16))


ROWG = 32


def _hy_multi_kernel(x0_ref, x1_ref, v_ref, w0_ref, w1_ref, wv_ref, b0_ref, b1_ref, bv_ref, f_ref, kh_ref,
                     skip_ref, g_ref, o_ref, zh_ref, z_ref, x0s_ref, yh_ref, *, nb, row_len):
    i = pl.program_id(2)
    dt = o_ref.shape[1]

    @pl.when(i == 0)
    def _():
        def front(blk, carry):
            rows = pl.ds(pl.multiple_of(blk * BLK, BLK), BLK)
            x0, z = _short_conv_gate(x0_ref[rows, :], x1_ref[rows, :], v_ref[rows, :],
                                     (w0_ref, w1_ref, wv_ref), (b0_ref, b1_ref, bv_ref), row_len)
            x0s_ref[rows, :] = x0
            z_ref[rows, :] = z
            zh_ref[blk] = _dot(f_ref[...], z.astype(BF16))
            return carry

        lax.fori_loop(0, nb, front, 0)

    row0 = lax.broadcasted_iota(jnp.int32, (ROWG, 1), 0) == 0
    for rg in range(BLK // ROWG):
        re = pl.ds(rg * ROWG, ROWG)
        im = pl.ds(BLK + rg * ROWG, ROWG)

        def body(j, acc):
            p, q, r = acc
            kidx = i - j + (nb - 1)
            zr = zh_ref[j, re, :]
            zi = zh_ref[j, im, :]
            kr = kh_ref[kidx, re, :]
            ki = kh_ref[kidx, im, :]
            return (p + zr * kr, q + zi * ki, r + (zr * ki + zi * kr))

        zeros = jnp.zeros((ROWG, dt), F32)
        p, q, r = lax.fori_loop(0, nb, body, (zeros, zeros, zeros), unroll=True)
        if rg == 0:
            yr = jnp.where(row0, p, p - q)
            yi = jnp.where(row0, q, r)
        else:
            yr = p - q
            yi = r
        yh_ref[re, :] = yr.astype(BF16)
        yh_ref[im, :] = yi.astype(BF16)
    rows = pl.ds(pl.multiple_of(i * BLK, BLK), BLK)
    y = _dot(g_ref[...], yh_ref[...]) + z_ref[rows, :] * skip_ref[...]
    o_ref[...] = x0s_ref[rows, :] * y


def _hyena_multi(pb, khat, conv_w, conv_b, skip, nseq, nb, row_len):
    t = pb.shape[0]
    seq_len = nb * BLK
    dt = LANE
    per = D_B // dt
    fwd_dft, inv_dft, _ = _dft_consts()
    seq = lambda cb: pl.BlockSpec((seq_len, dt), lambda d, b, i, cb=cb: (b, cb * per + d))
    wcol = lambda r, k: pl.BlockSpec((r, dt), lambda d, b, i, k=k: (0, k * per + d))
    conv_b = conv_b[None, :]
    return pl.pallas_call(
        functools.partial(_hy_multi_kernel, nb=nb, row_len=row_len),
        grid=(per, nseq, nb),
        in_specs=[seq(OB_X0), seq(OB_X1), seq(OB_V),
                  wcol(3, 0), wcol(3, 1), wcol(3, 2), wcol(1, 0), wcol(1, 1), wcol(1, 2),
                  pl.BlockSpec((NFREQ, BLK), lambda d, b, i: (0, 0)),
                  pl.BlockSpec((2 * nb - 1, NFREQ, dt), lambda d, b, i: (0, 0, d)),
                  pl.BlockSpec((1, dt), lambda d, b, i: (0, d)),
                  pl.BlockSpec((BLK, NFREQ), lambda d, b, i: (0, 0))],
        out_specs=pl.BlockSpec((BLK, dt), lambda d, b, i: (b * nb + i, d)),
        out_shape=jax.ShapeDtypeStruct((t, D_B), F32),
        scratch_shapes=[pltpu.VMEM((nb, NFREQ, dt), F32), pltpu.VMEM((seq_len, dt), F32),
                        pltpu.VMEM((seq_len, dt), F32), pltpu.VMEM((NFREQ, dt), BF16)],
        compiler_params=_params("parallel", "parallel", "arbitrary"),
        name="hyena_multi",
    )(pb, pb, pb, conv_w, conv_w, conv_w, conv_b, conv_b, conv_b, jnp.asarray(fwd_dft).astype(BF16), khat,
      skip[None, :], jnp.asarray(inv_dft).astype(BF16))


def _layer_norm(y, g, b):
    mu = jnp.mean(y, axis=-1, keepdims=True)
    yc = y - mu
    var = jnp.mean(yc * yc, axis=-1, keepdims=True)
    return yc * lax.rsqrt(var + LN_EPS) * g + b


def _merge_kernel(of_ref, ob_ref, g_ref, ga_ref, gb_ref, hy_ref, x_ref, mod_ref, nw_ref,
                  pa_ref, pb_ref, wo_ref, lg_ref, lb_ref, o_ref):
    o = of_ref[...] + ob_ref[...]
    nw = nw_ref[...]
    parts = []
    for h in range(H_A):
        oh = o[:, h * DV:(h + 1) * DV]
        ms = jnp.mean(oh * oh, axis=-1, keepdims=True)
        parts.append(oh * lax.rsqrt(ms + RMS_EPS) * nw)
    oa = jnp.concatenate(parts, axis=1) * g_ref[...]
    a = _dot(oa.astype(BF16), pa_ref[...])
    b = _dot(hy_ref[...].astype(BF16), pb_ref[...])
    merged = ga_ref[...] * a + gb_ref[...] * b
    mix = _dot(merged.astype(BF16), wo_ref[...])
    y = ALPHA * x_ref[...] + mod_ref[0, 2:3, :] * mix
    o_ref[...] = _layer_norm(y, lg_ref[...], lb_ref[...])


def _merge(o_f, o_b, proj, o_hy, x, mod3, cond_row, norm_w, pa, pb, wo, ln_g, ln_b):
    tm = 256
    t = x.shape[0]
    tok = pl.BlockSpec((tm, D_MODEL), lambda i: (i, 0))
    col = lambda cb: pl.BlockSpec((tm, D_MODEL), lambda i, cb=cb: (i, cb))
    mat = pl.BlockSpec((D_MODEL, D_MODEL), lambda i: (0, 0))
    vec = pl.BlockSpec((1, D_MODEL), lambda i: (0, 0))
    return pl.pallas_call(
        _merge_kernel,
        grid=(t // tm,),
        in_specs=[tok, tok, col(OF_G), col(OF_GA), col(OF_GB), tok, tok,
                  pl.BlockSpec((1, N_MOD, D_MODEL), lambda i: (cond_row(i * tm), 0, 0)),
                  pl.BlockSpec((1, DV), lambda i: (0, 0)),
                  mat, mat, mat, vec, vec],
        out_specs=tok,
        out_shape=jax.ShapeDtypeStruct((t, D_MODEL), F32),
        compiler_params=_params("parallel"),
        name="merge",
    )(o_f, o_b, proj, proj, proj, o_hy, x, mod3, norm_w[None, :], pa, pb, wo, ln_g[None, :], ln_b[None, :])


def _ffn_kernel(x_ref, mod_ref, wg_ref, wu_ref, wo_ref, lg_ref, lb_ref, o_ref):
    x = x_ref[...]
    h = (x * (1.0 + mod_ref[0, 4:5, :]) + mod_ref[0, 3:4, :]).astype(BF16)
    gt = _dot(h, wg_ref[...])
    up = _dot(h, wu_ref[...])
    act = (gt * _sigmoid(gt) * up).astype(BF16)
    ff = _dot(act, wo_ref[...])
    y = ALPHA * x + mod_ref[0, 5:6, :] * ff
    o_ref[...] = _layer_norm(y, lg_ref[...], lb_ref[...])


def _ffn(x, mod3, cond_row, w_in, w_out, ln_g, ln_b):
    tm = 256
    t = x.shape[0]
    tok = pl.BlockSpec((tm, D_MODEL), lambda i: (i, 0))
    vec = pl.BlockSpec((1, D_MODEL), lambda i: (0, 0))
    return pl.pallas_call(
        _ffn_kernel,
        grid=(t // tm,),
        in_specs=[tok,
                  pl.BlockSpec((1, N_MOD, D_MODEL), lambda i: (cond_row(i * tm), 0, 0)),
                  pl.BlockSpec((D_MODEL, D_FF), lambda i: (0, 0)),
                  pl.BlockSpec((D_MODEL, D_FF), lambda i: (0, 1)),
                  pl.BlockSpec((D_FF, D_MODEL), lambda i: (0, 0)),
                  vec, vec],
        out_specs=tok,
        out_shape=jax.ShapeDtypeStruct((t, D_MODEL), F32),
        compiler_params=_params("parallel"),
        name="ffn",
    )(x, mod3, w_in, w_in, w_out, ln_g[None, :], ln_b[None, :])


def _trunk(x, mod3, cond_row, nseq, seq_len, row_len, state, emit_state, w, lb_logits):
    nb = seq_len // BLK
    pf, pb = _inproj(x, mod3, w["w_in"], lb_logits, cond_row)
    hg = _hgrn(pf, pb, state, nseq, nb, emit_state)
    o_f, o_b = hg[0], hg[1]
    khat = _filter_spectra(seq_len, w["filt_w1"], w["filt_b1"], w["filt_w2"], w["filt_b2"],
                           w["filt_w3"], w["filt_b3"], w["filt_freq"], w["filt_w4"])
    if nb == 1:
        o_hy = _hyena_single(pb, khat, w["hy_conv_w"], w["hy_conv_b"], w["hy_skip"], row_len)
    else:
        o_hy = _hyena_multi(pb, khat, w["hy_conv_w"], w["hy_conv_b"], w["hy_skip"], nseq, nb, row_len)
    x1 = _merge(o_f, o_b, pf, o_hy, x, mod3, cond_row, w["hgrn_norm_w"], w["proj_a"], w["proj_b"],
                w["w_out"], w["ln1_g"], w["ln1_b"])
    x2 = _ffn(x1, mod3, cond_row, w["ffn_w_in"], w["ffn_w_out"], w["ln2_g"], w["ln2_b"])
    return x2, (hg[2] if emit_state else None)


def kernel(x_prompt, x_sample, state_hgrn, c, c_ctx, ada_w, ada_b, w_in, hgrn_lb_logits, hgrn_norm_w,
           hy_conv_w, hy_conv_b, filt_w1, filt_b1, filt_w2, filt_b2, filt_w3, filt_b3, filt_freq, filt_w4,
           hy_skip, proj_a, proj_b, w_out, ln1_g, ln1_b, ffn_w_in, ffn_w_out, ln2_g, ln2_b):
    assert ada_w.shape[0] == DEPTH == 1
    batch, seq, _ = x_prompt.shape
    dec_batch, dec_seq, _ = x_sample.shape
    assert seq % BLK == 0 and dec_seq % BLK == 0 and BLK % GRID_W == 0 and dec_batch + 1 <= 8

    w = dict(w_in=w_in[0].astype(BF16), hy_conv_w=hy_conv_w[0], hy_conv_b=hy_conv_b[0],
             filt_w1=filt_w1[0], filt_b1=filt_b1[0], filt_w2=filt_w2[0], filt_b2=filt_b2[0],
             filt_w3=filt_w3[0], filt_b3=filt_b3[0], filt_freq=filt_freq[0], filt_w4=filt_w4[0],
             hy_skip=hy_skip[0], hgrn_norm_w=hgrn_norm_w[0], proj_a=proj_a[0].astype(BF16),
             proj_b=proj_b[0].astype(BF16), w_out=w_out[0].astype(BF16), ln1_g=ln1_g[0], ln1_b=ln1_b[0],
             ffn_w_in=ffn_w_in[0].astype(BF16), ffn_w_out=ffn_w_out[0].astype(BF16),
             ln2_g=ln2_g[0], ln2_b=ln2_b[0])

    cond8 = jnp.zeros((8, D_MODEL), F32).at[0].set(c_ctx).at[1:1 + dec_batch].set(c)
    mod3 = _modulation(cond8, ada_w[0], ada_b[0][None, :]).reshape(8, N_MOD, D_MODEL)

    xp = x_prompt.reshape(batch * seq, D_MODEL)
    xs = x_sample.reshape(dec_batch * dec_seq, D_MODEL)
    yp, new_state = _trunk(xp, mod3, lambda tok: 0, batch, seq, seq, None, True, w, hgrn_lb_logits)
    ys, _ = _trunk(xs, mod3, lambda tok: 1 + tok // dec_seq, dec_batch, dec_seq, GRID_W, state_hgrn, False,
                   w, hgrn_lb_logits)
    return (yp.reshape(batch, seq, D_MODEL), ys.reshape(dec_batch, dec_seq, D_MODEL), new_state)
```

```python
import functools
import math

import numpy as np
import jax
import jax.numpy as jnp
from jax import lax
from jax.experimental import pallas as pl
from jax.experimental.pallas import tpu as pltpu

F32 = jnp.float32
BF16 = jnp.bfloat16

D_MODEL = 1024
DEPTH = 1
GRID_W = 64
H_A = 8
DK = 128
DV = 128
D_B = 1024
FILT_EMB = 33
FILT_BANDS = 16
FILT_ORDER = 64
DECAY_FAST = 0.3
DECAY_SLOW = 1.5
DECAY_TARGET = 1e-2
DECAY_SHIFT = 0.05
D_FF = 2816
N_MOD = 6
W_IN_COLS = 10 * D_MODEL
ALPHA = (2.0 * DEPTH) ** 0.25
LN_EPS = 1e-5
RMS_EPS = 1e-6

LANE = 128
BLK = 256
NFREQ = 2 * BLK
CHUNK = 32
NCHUNK = BLK // CHUNK
HPS = 2
VMEM_LIMIT = 56 * 1024 * 1024

CB_Q, CB_FF, CB_FB, CB_I, CB_G, CB_X0, CB_X1, CB_V, CB_GA, CB_GB = range(10)
STEP_COLS = (CB_Q, CB_FF, CB_FB, CB_G, CB_GA, CB_GB, CB_I, CB_X0, CB_X1, CB_V)
OF_Q, OF_FF, OF_FB, OF_G, OF_GA, OF_GB = range(6)
N_F32_COLS = 6
OB_I, OB_X0, OB_X1, OB_V = range(4)


def _sigmoid(x):
    return 1.0 / (1.0 + jnp.exp(-x))


def _dot(a, b):
    return jnp.dot(a, b, preferred_element_type=F32)


def _dot_nt(a, b):
    return lax.dot_general(a, b, (((1,), (1,)), ((), ())), preferred_element_type=F32)


def _dot_tn(a, b):
    return lax.dot_general(a, b, (((0,), (0,)), ((), ())), preferred_element_type=F32)


def _dot_hi(a, b):
    return jnp.dot(a, b, preferred_element_type=F32, precision=lax.Precision.HIGHEST)


def _params(*sem):
    return pltpu.CompilerParams(dimension_semantics=sem, vmem_limit_bytes=VMEM_LIMIT)


@functools.lru_cache(maxsize=None)
def _dft_consts():
    n = np.arange(BLK, dtype=np.float64)
    f = np.arange(BLK, dtype=np.float64)
    ang = 2.0 * np.pi * np.outer(f, n) / NFREQ
    fwd = np.zeros((NFREQ, BLK), np.float64)
    fwd[:BLK] = np.cos(ang)
    fwd[BLK + 1:] = -np.sin(ang[1:])
    fwd[BLK] = np.cos(np.pi * n)
    inv = np.zeros((BLK, NFREQ), np.float64)
    scale = np.full((BLK,), 2.0)
    scale[0] = 1.0
    inv[:, :BLK] = np.cos(ang.T) * scale[None, :]
    inv[:, BLK + 1:] = -2.0 * np.sin(ang.T[:, 1:])
    inv[:, BLK] = np.cos(np.pi * n)
    inv /= NFREQ
    fr = np.arange(NFREQ)
    freq_of_row = np.where(fr < BLK, fr, np.where(fr == BLK, BLK, fr - BLK))
    sgn = np.where(freq_of_row % 2 == 0, 1.0, -1.0)[:, None]
    return fwd.astype(np.float32), inv.astype(np.float32), sgn.astype(np.float32)


@functools.lru_cache(maxsize=None)
def _scan_consts():
    t = np.arange(BLK)
    ct = t // CHUNK
    same = ct[:, None] == ct[None, :]
    tri_f = (same & (t[None, :] <= t[:, None])).astype(np.float32)
    tri_b = (same & (t[None, :] >= t[:, None])).astype(np.float32)

    def levels(p, diag):
        x = p[:, None] ^ p[None, :]
        lvl = np.zeros_like(x)
        for bit in range(1, NCHUNK.bit_length()):
            lvl = np.where(x >= (1 << (bit - 1)), bit, lvl)
        lv = np.where(p[:, None] > p[None, :], lvl, -1)
        return np.where(same, np.where(diag, 0, -1), lv).astype(np.int32)

    lv_f = levels(ct, t[None, :] <= t[:, None])
    lv_b = levels(NCHUNK - 1 - ct, t[None, :] >= t[:, None])
    return tri_f, tri_b, lv_f, lv_b


@functools.lru_cache(maxsize=None)
def _filter_positions(seq_len):
    f32 = np.float32
    j = np.arange(-seq_len, seq_len)
    p = np.abs(j)
    valid = (j > -seq_len)
    pc = np.minimum(p, seq_len - 1)
    t = np.linspace(0.0, 1.0, seq_len, dtype=f32)[pc]
    wpos = (f32(2.0 * math.pi / seq_len) * np.arange(seq_len, dtype=f32))[pc]
    bands = np.linspace(1e-4, FILT_BANDS - 1, FILT_BANDS, dtype=f32)
    arg = (bands[None, :] * wpos[:, None]).astype(f32)
    z = np.zeros((2 * seq_len, LANE), f32)
    z[:, 0] = t
    z[:, 1:1 + FILT_BANDS] = np.cos(arg)
    z[:, 1 + FILT_BANDS:FILT_EMB] = -np.sin(arg)
    z[:, FILT_EMB] = valid.astype(f32)
    return z


@functools.lru_cache(maxsize=None)
def _decay_rates():
    max_decay = math.log(DECAY_TARGET) / DECAY_FAST
    min_decay = math.log(DECAY_TARGET) / DECAY_SLOW
    return np.abs(np.linspace(min_decay, max_decay, D_B, dtype=np.float32))[None, :]


def _mod_kernel(c_ref, w_ref, b_ref, o_ref):
    c = c_ref[...]
    s = (c * _sigmoid(c)).astype(BF16)
    o_ref[...] = _dot(s, w_ref[...].astype(BF16)) + b_ref[...]


def _modulation(cond8, ada_w, ada_b):
    tn = 1536
    n = N_MOD * D_MODEL
    return pl.pallas_call(
        _mod_kernel,
        grid=(n // tn,),
        in_specs=[pl.BlockSpec((8, D_MODEL), lambda j: (0, 0)),
                  pl.BlockSpec((D_MODEL, tn), lambda j: (0, j)),
                  pl.BlockSpec((1, tn), lambda j: (0, j))],
        out_specs=pl.BlockSpec((8, tn), lambda j: (0, j)),
        out_shape=jax.ShapeDtypeStruct((8, n), F32),
        compiler_params=_params("parallel"),
        name="modulation",
    )(cond8, ada_w, ada_b)


def _lower_bounds(lbl_ref):
    l0 = lbl_ref[0]
    l1 = lbl_ref[1]
    m = jnp.maximum(l0, l1)
    e0 = jnp.exp(l0 - m)
    e1 = jnp.exp(l1 - m)
    return e0 / (e0 + e1)


def _inproj_kernel(x_ref, mod_ref, w_ref, lbl_ref, of_ref, ob_ref, h_ref):
    j = pl.program_id(1)

    @pl.when(j == 0)
    def _():
        h = x_ref[...] * (1.0 + mod_ref[0, 1:2, :]) + mod_ref[0, 0:1, :]
        h_ref[...] = h.astype(BF16)

    def project(o_ref, act):
        for r in range(x_ref.shape[0] // BLK):
            rows = pl.ds(r * BLK, BLK)
            o_ref[rows, :] = act(_dot(h_ref[rows, :], w_ref[...])).astype(o_ref.dtype)

    @pl.when(jnp.logical_or(j == OF_Q, j == OF_G))
    def _():
        project(of_ref, lambda a: a * _sigmoid(a))

    @pl.when(jnp.logical_or(j == OF_GA, j == OF_GB))
    def _():
        project(of_ref, _sigmoid)

    @pl.when(jnp.logical_or(j == OF_FF, j == OF_FB))
    def _():
        lb2 = _lower_bounds(lbl_ref)
        lb = jnp.where(j == OF_FF, lb2[0:1, :], lb2[1:2, :])
        project(of_ref, lambda a: jnp.log(lb + (1.0 - lb) * _sigmoid(a)))

    @pl.when(j >= N_F32_COLS)
    def _():
        project(ob_ref, lambda a: a)


def _weight_col(j):
    col = jnp.int32(STEP_COLS[-1])
    for step in reversed(range(len(STEP_COLS) - 1)):
        col = jnp.where(j == step, STEP_COLS[step], col)
    return col


def _inproj(x, mod3, w_bf, lb_logits, cond_row):
    tm = 2048
    t = x.shape[0]
    n_b16 = len(STEP_COLS) - N_F32_COLS
    return pl.pallas_call(
        _inproj_kernel,
        grid=(t // tm, len(STEP_COLS)),
        in_specs=[pl.BlockSpec((tm, D_MODEL), lambda i, j: (i, 0)),
                  pl.BlockSpec((1, N_MOD, D_MODEL), lambda i, j: (cond_row(i * tm), 0, 0)),
                  pl.BlockSpec((D_MODEL, D_MODEL), lambda i, j: (0, _weight_col(j))),
                  pl.BlockSpec((2, 2, D_MODEL), lambda i, j: (0, 0, 0))],
        out_specs=[pl.BlockSpec((tm, D_MODEL), lambda i, j: (i, jnp.minimum(j, N_F32_COLS - 1))),
                   pl.BlockSpec((tm, D_MODEL), lambda i, j: (i, jnp.maximum(j - N_F32_COLS, 0)))],
        out_shape=[jax.ShapeDtypeStruct((t, N_F32_COLS * D_MODEL), F32),
                   jax.ShapeDtypeStruct((t, n_b16 * D_MODEL), BF16)],
        scratch_shapes=[pltpu.VMEM((tm, D_MODEL), BF16)],
        compiler_params=_params("parallel", "arbitrary"),
        name="inproj",
    )(x, mod3, w_bf, lb_logits)


def _chunk_cumsum(lf, tri):
    lf_hi = lf.astype(BF16)
    lf_lo = (lf - lf_hi.astype(F32)).astype(BF16)
    return _dot(tri, lf_hi) + _dot(tri, lf_lo)


def _hgrn_direction(q, lf, b, v, st, lv, reverse, use_state):
    k = 1.0 - jnp.exp(lf)
    qe = q * jnp.exp(b)
    k0 = k * jnp.exp(-b)

    order = [NCHUNK - 1 - i for i in range(NCHUNK)] if reverse else list(range(NCHUNK))
    chunk_of = {p: i for i, p in enumerate(order)}
    sl = lambda i: slice(i * CHUNK, (i + 1) * CHUNK)
    last_row = lambda i: (i * CHUNK) if reverse else (i * CHUNK + CHUNK - 1)
    c = [None] * NCHUNK
    for p in range(NCHUNK):
        r = last_row(chunk_of[p])
        c[p] = b[r:r + 1, :]
    cum = [jnp.zeros_like(c[0])]
    for p in range(NCHUNK):
        cum.append(cum[p] + c[p])
    total = cum[NCHUNK]

    qe_c, ke_c = {}, {}
    for p in range(NCHUNK):
        i = chunk_of[p]
        qe_c[p] = qe[sl(i), :]
        ke_c[p] = k0[sl(i), :] * jnp.exp(c[p])

    def assemble(parts):
        return jnp.concatenate([parts[order[i]] for i in range(NCHUNK)], axis=0).astype(BF16)

    s_all = _dot_nt(qe.astype(BF16), k0.astype(BF16))
    s_mat = jnp.where(lv == 0, s_all, 0.0)
    zero = jnp.zeros((CHUNK, LANE), F32)
    nlev = NCHUNK.bit_length() - 1
    for lev in range(1, nlev + 1):
        half = 1 << (lev - 1)
        qp, kp = {}, {}
        for p in range(NCHUNK):
            pm = ((p >> lev) << lev) + half
            if p >= pm:
                qp[p] = qe_c[p] * jnp.exp(cum[p] - cum[pm])
                kp[p] = zero
            else:
                qp[p] = zero
                kp[p] = ke_c[p] * jnp.exp(cum[pm] - cum[p + 1])
        s_lev = _dot_nt(assemble(qp), assemble(kp))
        s_mat = jnp.where(lv == lev, s_lev, s_mat)

    v_bf = v.astype(BF16)
    out = _dot(s_mat.astype(BF16), v_bf)
    if use_state:
        q_start = assemble({p: qe_c[p] * jnp.exp(cum[p]) for p in range(NCHUNK)})
        out = out + _dot_nt(q_start, st.astype(BF16))
    k_end = assemble({p: ke_c[p] * jnp.exp(total - cum[p + 1]) for p in range(NCHUNK)})
    upd = _dot_tn(v_bf, k_end)
    new_st = st * jnp.exp(total) + upd if use_state else upd
    return out, new_st


def _hgrn_kernel(*refs, nb, zero_init, emit_state):
    it = iter(refs)
    qf_ref, lff_ref, vf_ref, qb_ref, lfb_ref, vb_ref = [next(it) for _ in range(6)]
    s0_ref = None if zero_init else next(it)
    lvf_ref, lvb_ref, trif_ref, trib_ref = [next(it) for _ in range(4)]
    of_ref = next(it)
    ob_ref = next(it) if nb > 1 else None
    so_ref = next(it) if emit_state else None
    st_ref = next(it)
    i = pl.program_id(2)

    use_state = not (zero_init and nb == 1)
    if use_state:
        @pl.when(i == 0)
        def _():
            for d in range(2):
                for h in range(HPS):
                    st_ref[d, h] = jnp.zeros((DV, DK), F32) if zero_init else s0_ref[0, 0, d, h].T

    lf_f = lff_ref[...]
    lf_b = lfb_ref[...]
    b_f = _chunk_cumsum(lf_f, trif_ref[...])
    b_b = _chunk_cumsum(lf_b, trib_ref[...])
    lv_f = lvf_ref[...]
    lv_b = lvb_ref[...]
    for h in range(HPS):
        hs = slice(h * LANE, (h + 1) * LANE)
        of, stf = _hgrn_direction(qf_ref[:, hs], lf_f[:, hs], b_f[:, hs], vf_ref[:, hs], st_ref[0, h],
                                  lv_f, False, use_state)
        ob, stb = _hgrn_direction(qb_ref[:, hs], lf_b[:, hs], b_b[:, hs], vb_ref[:, hs], st_ref[1, h],
                                  lv_b, True, use_state)
        if nb > 1:
            of_ref[:, hs] = of
            ob_ref[:, hs] = ob
        else:
            of_ref[:, hs] = of + ob
        if nb > 1:
            st_ref[0, h] = stf
            st_ref[1, h] = stb
        if emit_state:
            @pl.when(i == nb - 1)
            def _():
                so_ref[0, 0, 0, h] = stf.T
                so_ref[0, 0, 1, h] = stb.T


def _hgrn(pf, pb, state, nseq, nb, emit_state):
    zero_init = state is None
    t = pf.shape[0]
    tri_f, tri_b, lv_f, lv_b = _scan_consts()
    wid = HPS * LANE
    per = D_MODEL // wid
    fwd = lambda cb: pl.BlockSpec((BLK, wid), lambda b, h, i, cb=cb: (b * nb + i, cb * per + h))
    bwd = lambda cb: pl.BlockSpec((BLK, wid), lambda b, h, i, cb=cb: (b * nb + nb - 1 - i, cb * per + h))
    const = lambda: pl.BlockSpec((BLK, BLK), lambda b, h, i: (0, 0))
    st_spec = pl.BlockSpec((1, 1, 2, HPS, DK, DV), lambda b, h, i: (b, 0, 0, h, 0, 0))
    in_specs = [fwd(OF_Q), fwd(OF_FF), fwd(OB_I), bwd(OF_Q), bwd(OF_FB), bwd(OB_I)]
    args = [pf, pf, pb, pf, pf, pb]
    if not zero_init:
        in_specs.append(st_spec)
        args.append(state)
    in_specs += [const(), const(), const(), const()]
    args += [jnp.asarray(lv_f), jnp.asarray(lv_b), jnp.asarray(tri_f, BF16), jnp.asarray(tri_b, BF16)]
    out_specs = [pl.BlockSpec((BLK, wid), lambda b, h, i: (b * nb + i, h))]
    out_shape = [jax.ShapeDtypeStruct((t, D_MODEL), F32)]
    if nb > 1:
        out_specs.append(pl.BlockSpec((BLK, wid), lambda b, h, i: (b * nb + nb - 1 - i, h)))
        out_shape.append(jax.ShapeDtypeStruct((t, D_MODEL), F32))
    if emit_state:
        out_specs.append(st_spec)
        out_shape.append(jax.ShapeDtypeStruct((nseq, DEPTH, 2, H_A, DK, DV), F32))
    return pl.pallas_call(
        functools.partial(_hgrn_kernel, nb=nb, zero_init=zero_init, emit_state=emit_state),
        grid=(nseq, H_A // HPS, nb),
        in_specs=in_specs,
        out_specs=out_specs,
        out_shape=out_shape,
        scratch_shapes=[pltpu.VMEM((2, HPS, DV, DK), F32)],
        compiler_params=_params("parallel", "parallel", "arbitrary"),
        name="hgrn_scan",
    )(*args)


def _filter_kernel(z_ref, w1_ref, b1_ref, w2_ref, b2_ref, w3_ref, b3_ref, fq_ref, w4_ref, dec_ref,
                   fhi_ref, flo_ref, sgn_ref, o_ref, prev_ref):
    zp = z_ref[...]
    h = jnp.sin(fq_ref[0:1, :] * (_dot_hi(zp, w1_ref[...]) + b1_ref[...]))
    h = jnp.sin(fq_ref[1:2, :] * (_dot_hi(h, w2_ref[...]) + b2_ref[...]))
    h = jnp.sin(fq_ref[2:3, :] * (_dot_hi(h, w3_ref[...]) + b3_ref[...]))
    a = _dot_hi(h, w4_ref[...])
    window = jnp.exp(-zp[:, 0:1] * dec_ref[...]) + DECAY_SHIFT
    a = a * window * zp[:, FILT_EMB:FILT_EMB + 1]
    a_hi = a.astype(BF16)
    a_lo = (a - a_hi.astype(F32)).astype(BF16)
    f_hi = fhi_ref[...]
    ah = _dot(f_hi, a_hi) + (_dot(f_hi, a_lo) + _dot(flo_ref[...], a_hi))
    o_ref[0] = ah + sgn_ref[...] * prev_ref[...]
    prev_ref[...] = ah


def _filter_spectra(seq_len, w1, b1, w2, b2, w3, b3, freq, w4):
    nb = seq_len // BLK
    fwd_dft, _, sgn = _dft_consts()
    zpos = jnp.asarray(_filter_positions(seq_len))
    f_full = jnp.asarray(fwd_dft)
    f_hi = f_full.astype(BF16)
    f_lo = (f_full - f_hi.astype(F32)).astype(BF16)
    pad2 = lambda w: jnp.pad(w, ((0, LANE - w.shape[0]), (0, LANE - w.shape[1])))
    padv = lambda b: jnp.pad(b, (0, LANE - b.shape[0]))[None, :]
    w1p, w2p, w3p = pad2(w1), pad2(w2), pad2(w3)
    b1p, b2p, b3p = padv(b1), padv(b2), padv(b3)
    fqp = jnp.pad(freq, ((0, 0), (0, LANE - freq.shape[1])))
    w4p = jnp.pad(w4, ((0, LANE - w4.shape[0]), (0, 0)))
    small = lambda shape: pl.BlockSpec(shape, lambda m: (0, 0))
    return pl.pallas_call(
        _filter_kernel,
        grid=(2 * nb,),
        in_specs=[pl.BlockSpec((BLK, LANE), lambda m: (m, 0)),
                  small((LANE, LANE)), small((1, LANE)), small((LANE, LANE)), small((1, LANE)),
                  small((LANE, LANE)), small((1, LANE)), small((3, LANE)),
                  pl.BlockSpec((LANE, D_B), lambda m: (0, jnp.where(m < nb, 1, 0))),
                  small((1, D_B)), small((NFREQ, BLK)), small((NFREQ, BLK)), small((NFREQ, 1))],
        out_specs=pl.BlockSpec((1, NFREQ, D_B), lambda m: (jnp.maximum(m - 1, 0), 0, 0)),
        out_shape=jax.ShapeDtypeStruct((2 * nb - 1, NFREQ, D_B), F32),
        scratch_shapes=[pltpu.VMEM((NFREQ, D_B), F32)],
        compiler_params=_params("arbitrary"),
        name="hyena_filter",
    )(zpos, w1p, b1p, w2p, b2p, w3p, b3p, fqp, w4p, jnp.asarray(_decay_rates()),
      f_hi, f_lo, jnp.asarray(sgn))


def _short_conv_gate(u0, u1, uv, w_refs, b_refs, row_len):
    t = lax.broadcasted_iota(jnp.int32, (BLK, 1), 0)
    first = (t % row_len) == 0
    last = (t % row_len) == (row_len - 1)

    def conv(u, w_ref, b_ref):
        u = u.astype(F32)
        up = jnp.where(first, 0.0, pltpu.roll(u, 1, 0))
        dn = jnp.where(last, 0.0, pltpu.roll(u, BLK - 1, 0))
        return up * w_ref[0:1, :] + u * w_ref[1:2, :] + dn * w_ref[2:3, :] + b_ref[...]

    return (conv(u0, w_refs[0], b_refs[0]),
            conv(uv, w_refs[2], b_refs[2]) * conv(u1, w_refs[1], b_refs[1]))


def _hy_single_kernel(x0_ref, x1_ref, v_ref, w0_ref, w1_ref, wv_ref, b0_ref, b1_ref, bv_ref, f_ref, kh_ref,
                      skip_ref, g_ref, o_ref, *, row_len):
    x0, z = _short_conv_gate(x0_ref[...], x1_ref[...], v_ref[...], (w0_ref, w1_ref, wv_ref),
                             (b0_ref, b1_ref, bv_ref), row_len)
    zh = _dot(f_ref[...], z.astype(BF16))
    zr, zi = zh[:BLK, :], zh[BLK:, :]
    kr, ki = kh_ref[0, :BLK, :], kh_ref[0, BLK:, :]
    p = zr * kr
    q = zi * ki
    r = zr * ki + zi * kr
    row0 = lax.broadcasted_iota(jnp.int32, (BLK, 1), 0) == 0
    yh = jnp.concatenate([jnp.where(row0, p, p - q), jnp.where(row0, q, r)], axis=0).astype(BF16)
    y = _dot(g_ref[...], yh) + z * skip_ref[...]
    o_ref[...] = x0 * y


def _hyena_single(pb, khat, conv_w, conv_b, skip, row_len):
    t = pb.shape[0]
    fwd_dft, inv_dft, _ = _dft_consts()
    col = lambda cb: pl.BlockSpec((BLK, D_B), lambda i, cb=cb: (i, cb))
    wcol = lambda r, k: pl.BlockSpec((r, D_B), lambda i, k=k: (0, k))
    conv_b = conv_b[None, :]
    return pl.pallas_call(
        functools.partial(_hy_single_kernel, row_len=row_len),
        grid=(t // BLK,),
        in_specs=[col(OB_X0), col(OB_X1), col(OB_V),
                  wcol(3, 0), wcol(3, 1), wcol(3, 2), wcol(1, 0), wcol(1, 1), wcol(1, 2),
                  pl.BlockSpec((NFREQ, BLK), lambda i: (0, 0)),
                  pl.BlockSpec((1, NFREQ, D_B), lambda i: (0, 0, 0)),
                  pl.BlockSpec((1, D_B), lambda i: (0, 0)),
                  pl.BlockSpec((BLK, NFREQ), lambda i: (0, 0))],
        out_specs=pl.BlockSpec((BLK, D_B), lambda i: (i, 0)),
        out_shape=jax.ShapeDtypeStruct((t, D_B), F32),
        compiler_params=_params("parallel"),
        name="hyena_single",
    )(pb, pb, pb, conv_w, conv_w, conv_w, conv_b, conv_b, conv_b, jnp.asarray(fwd_dft).astype(BF16), khat,
      skip[None, :], jnp.asarray(inv_dft).astype(BF16))


ROWG = 32


def _hy_multi_kernel(x0_ref, x1_ref, v_ref, w0_ref, w1_ref, wv_ref, b0_ref, b1_ref, bv_ref, f_ref, kh_ref,
                     skip_ref, g_ref, o_ref, zh_ref, z_ref, yh_ref, *, nb, row_len):
    dt = o_ref.shape[1]

    def front(blk, carry):
        rows = pl.ds(pl.multiple_of(blk * BLK, BLK), BLK)
        x0, z = _short_conv_gate(x0_ref[rows, :], x1_ref[rows, :], v_ref[rows, :],
                                 (w0_ref, w1_ref, wv_ref), (b0_ref, b1_ref, bv_ref), row_len)
        o_ref[rows, :] = x0
        z_ref[rows, :] = z
        zh_ref[blk] = _dot(f_ref[...], z.astype(BF16))
        return carry

    lax.fori_loop(0, nb, front, 0)

    row0 = lax.broadcasted_iota(jnp.int32, (ROWG, 1), 0) == 0

    def back(i, carry):
        for rg in range(BLK // ROWG):
            re = pl.ds(rg * ROWG, ROWG)
            im = pl.ds(BLK + rg * ROWG, ROWG)

            def body(j, acc):
                p, q, r = acc
                kidx = i - j + (nb - 1)
                zr = zh_ref[j, re, :]
                zi = zh_ref[j, im, :]
                kr = kh_ref[kidx, re, :]
                ki = kh_ref[kidx, im, :]
                return (p + zr * kr, q + zi * ki, r + (zr * ki + zi * kr))

            zeros = jnp.zeros((ROWG, dt), F32)
            p, q, r = lax.fori_loop(0, nb, body, (zeros, zeros, zeros), unroll=True)
            if rg == 0:
                yr = jnp.where(row0, p, p - q)
                yi = jnp.where(row0, q, r)
            else:
                yr = p - q
                yi = r
            yh_ref[re, :] = yr.astype(BF16)
            yh_ref[im, :] = yi.astype(BF16)
        rows = pl.ds(pl.multiple_of(i * BLK, BLK), BLK)
        y = _dot(g_ref[...], yh_ref[...]) + z_ref[rows, :] * skip_ref[...]
        o_ref[rows, :] = o_ref[rows, :] * y
        return carry

    lax.fori_loop(0, nb, back, 0)


def _hyena_multi(pb, khat, conv_w, conv_b, skip, nseq, nb, row_len):
    t = pb.shape[0]
    seq_len = nb * BLK
    dt = LANE
    per = D_B // dt
    fwd_dft, inv_dft, _ = _dft_consts()
    seq = lambda cb: pl.BlockSpec((seq_len, dt), lambda d, b, cb=cb: (b, cb * per + d))
    wcol = lambda r, k: pl.BlockSpec((r, dt), lambda d, b, k=k: (0, k * per + d))
    conv_b = conv_b[None, :]
    return pl.pallas_call(
        functools.partial(_hy_multi_kernel, nb=nb, row_len=row_len),
        grid=(per, nseq),
        in_specs=[seq(OB_X0), seq(OB_X1), seq(OB_V),
                  wcol(3, 0), wcol(3, 1), wcol(3, 2), wcol(1, 0), wcol(1, 1), wcol(1, 2),
                  pl.BlockSpec((NFREQ, BLK), lambda d, b: (0, 0)),
                  pl.BlockSpec((2 * nb - 1, NFREQ, dt), lambda d, b: (0, 0, d)),
                  pl.BlockSpec((1, dt), lambda d, b: (0, d)),
                  pl.BlockSpec((BLK, NFREQ), lambda d, b: (0, 0))],
        out_specs=pl.BlockSpec((seq_len, dt), lambda d, b: (b, d)),
        out_shape=jax.ShapeDtypeStruct((t, D_B), F32),
        scratch_shapes=[pltpu.VMEM((nb, NFREQ, dt), F32), pltpu.VMEM((seq_len, dt), F32),
                        pltpu.VMEM((NFREQ, dt), BF16)],
        compiler_params=_params("parallel", "parallel"),
        name="hyena_multi",
    )(pb, pb, pb, conv_w, conv_w, conv_w, conv_b, conv_b, conv_b, jnp.asarray(fwd_dft).astype(BF16), khat,
      skip[None, :], jnp.asarray(inv_dft).astype(BF16))


def _layer_norm(y, g, b):
    mu = jnp.mean(y, axis=-1, keepdims=True)
    yc = y - mu
    var = jnp.mean(yc * yc, axis=-1, keepdims=True)
    return yc * lax.rsqrt(var + LN_EPS) * g + b


def _merge_kernel(*refs, n_dir):
    (g_ref, ga_ref, gb_ref, hy_ref, x_ref, mod_ref, nw_ref,
     pa_ref, pb_ref, wo_ref, lg_ref, lb_ref, o_ref) = refs[n_dir:]
    o = refs[0][...]
    for d_ref in refs[1:n_dir]:
        o = o + d_ref[...]
    nw = nw_ref[...]
    parts = []
    for h in range(H_A):
        oh = o[:, h * DV:(h + 1) * DV]
        ms = jnp.mean(oh * oh, axis=-1, keepdims=True)
        parts.append(oh * lax.rsqrt(ms + RMS_EPS) * nw)
    oa = jnp.concatenate(parts, axis=1) * g_ref[...]
    a = _dot(oa.astype(BF16), pa_ref[...])
    b = _dot(hy_ref[...].astype(BF16), pb_ref[...])
    merged = ga_ref[...] * a + gb_ref[...] * b
    mix = _dot(merged.astype(BF16), wo_ref[...])
    y = ALPHA * x_ref[...] + mod_ref[0, 2:3, :] * mix
    o_ref[...] = _layer_norm(y, lg_ref[...], lb_ref[...])


def _merge(o_dirs, proj, o_hy, x, mod3, cond_row, norm_w, pa, pb, wo, ln_g, ln_b):
    tm = 256
    t = x.shape[0]
    n_dir = len(o_dirs)
    tok = pl.BlockSpec((tm, D_MODEL), lambda i: (i, 0))
    col = lambda cb: pl.BlockSpec((tm, D_MODEL), lambda i, cb=cb: (i, cb))
    mat = pl.BlockSpec((D_MODEL, D_MODEL), lambda i: (0, 0))
    vec = pl.BlockSpec((1, D_MODEL), lambda i: (0, 0))
    return pl.pallas_call(
        functools.partial(_merge_kernel, n_dir=n_dir),
        grid=(t // tm,),
        in_specs=[tok] * n_dir + [col(OF_G), col(OF_GA), col(OF_GB), tok, tok,
                  pl.BlockSpec((1, N_MOD, D_MODEL), lambda i: (cond_row(i * tm), 0, 0)),
                  pl.BlockSpec((1, DV), lambda i: (0, 0)),
                  mat, mat, mat, vec, vec],
        out_specs=tok,
        out_shape=jax.ShapeDtypeStruct((t, D_MODEL), F32),
        compiler_params=_params("parallel"),
        name="merge",
    )(*o_dirs, proj, proj, proj, o_hy, x, mod3, norm_w[None, :], pa, pb, wo, ln_g[None, :], ln_b[None, :])


def _ffn_kernel(x_ref, mod_ref, wg_ref, wu_ref, wo_ref, lg_ref, lb_ref, o_ref):
    x = x_ref[...]
    h = (x * (1.0 + mod_ref[0, 4:5, :]) + mod_ref[0, 3:4, :]).astype(BF16)
    gt = _dot(h, wg_ref[...])
    up = _dot(h, wu_ref[...])
    act = (gt * _sigmoid(gt) * up).astype(BF16)
    ff = _dot(act, wo_ref[...])
    y = ALPHA * x + mod_ref[0, 5:6, :] * ff
    o_ref[...] = _layer_norm(y, lg_ref[...], lb_ref[...])


def _ffn(x, mod3, cond_row, w_in, w_out, ln_g, ln_b):
    tm = 256
    t = x.shape[0]
    tok = pl.BlockSpec((tm, D_MODEL), lambda i: (i, 0))
    vec = pl.BlockSpec((1, D_MODEL), lambda i: (0, 0))
    return pl.pallas_call(
        _ffn_kernel,
        grid=(t // tm,),
        in_specs=[tok,
                  pl.BlockSpec((1, N_MOD, D_MODEL), lambda i: (cond_row(i * tm), 0, 0)),
                  pl.BlockSpec((D_MODEL, D_FF), lambda i: (0, 0)),
                  pl.BlockSpec((D_MODEL, D_FF), lambda i: (0, 1)),
                  pl.BlockSpec((D_FF, D_MODEL), lambda i: (0, 0)),
                  vec, vec],
        out_specs=tok,
        out_shape=jax.ShapeDtypeStruct((t, D_MODEL), F32),
        compiler_params=_params("parallel"),
        name="ffn",
    )(x, mod3, w_in, w_in, w_out, ln_g[None, :], ln_b[None, :])


def _trunk(x, mod3, cond_row, nseq, seq_len, row_len, state, emit_state, w, lb_logits):
    nb = seq_len // BLK
    pf, pb = _inproj(x, mod3, w["w_in"], lb_logits, cond_row)
    hg = _hgrn(pf, pb, state, nseq, nb, emit_state)
    n_dir = 2 if nb > 1 else 1
    khat = _filter_spectra(seq_len, w["filt_w1"], w["filt_b1"], w["filt_w2"], w["filt_b2"],
                           w["filt_w3"], w["filt_b3"], w["filt_freq"], w["filt_w4"])
    if nb == 1:
        o_hy = _hyena_single(pb, khat, w["hy_conv_w"], w["hy_conv_b"], w["hy_skip"], row_len)
    else:
        o_hy = _hyena_multi(pb, khat, w["hy_conv_w"], w["hy_conv_b"], w["hy_skip"], nseq, nb, row_len)
    x1 = _merge(hg[:n_dir], pf, o_hy, x, mod3, cond_row, w["hgrn_norm_w"], w["proj_a"], w["proj_b"],
                w["w_out"], w["ln1_g"], w["ln1_b"])
    x2 = _ffn(x1, mod3, cond_row, w["ffn_w_in"], w["ffn_w_out"], w["ln2_g"], w["ln2_b"])
    return x2, (hg[n_dir] if emit_state else None)


def kernel(x_prompt, x_sample, state_hgrn, c, c_ctx, ada_w, ada_b, w_in, hgrn_lb_logits, hgrn_norm_w,
           hy_conv_w, hy_conv_b, filt_w1, filt_b1, filt_w2, filt_b2, filt_w3, filt_b3, filt_freq, filt_w4,
           hy_skip, proj_a, proj_b, w_out, ln1_g, ln1_b, ffn_w_in, ffn_w_out, ln2_g, ln2_b):
    assert ada_w.shape[0] == DEPTH == 1
    batch, seq, _ = x_prompt.shape
    dec_batch, dec_seq, _ = x_sample.shape
    assert seq % BLK == 0 and dec_seq % BLK == 0 and BLK % GRID_W == 0 and dec_batch + 1 <= 8

    w = dict(w_in=w_in[0].astype(BF16), hy_conv_w=hy_conv_w[0], hy_conv_b=hy_conv_b[0],
             filt_w1=filt_w1[0], filt_b1=filt_b1[0], filt_w2=filt_w2[0], filt_b2=filt_b2[0],
             filt_w3=filt_w3[0], filt_b3=filt_b3[0], filt_freq=filt_freq[0], filt_w4=filt_w4[0],
             hy_skip=hy_skip[0], hgrn_norm_w=hgrn_norm_w[0], proj_a=proj_a[0].astype(BF16),
             proj_b=proj_b[0].astype(BF16), w_out=w_out[0].astype(BF16), ln1_g=ln1_g[0], ln1_b=ln1_b[0],
             ffn_w_in=ffn_w_in[0].astype(BF16), ffn_w_out=ffn_w_out[0].astype(BF16),
             ln2_g=ln2_g[0], ln2_b=ln2_b[0])

    cond8 = jnp.zeros((8, D_MODEL), F32).at[0].set(c_ctx).at[1:1 + dec_batch].set(c)
    mod3 = _modulation(cond8, ada_w[0], ada_b[0][None, :]).reshape(8, N_MOD, D_MODEL)

    xp = x_prompt.reshape(batch * seq, D_MODEL)
    xs = x_sample.reshape(dec_batch * dec_seq, D_MODEL)
    yp, new_state = _trunk(xp, mod3, lambda tok: 0, batch, seq, seq, None, True, w, hgrn_lb_logits)
    ys, _ = _trunk(xs, mod3, lambda tok: 1 + tok // dec_seq, dec_batch, dec_seq, GRID_W, state_hgrn, False,
                   w, hgrn_lb_logits)
    return (yp.reshape(batch, seq, D_MODEL), ys.reshape(dec_batch, dec_seq, D_MODEL), new_state)
```

```python
import functools
import math

import numpy as np
import jax
import jax.numpy as jnp
from jax import lax
from jax.experimental import pallas as pl
from jax.experimental.pallas import tpu as pltpu

F32 = jnp.float32
BF16 = jnp.bfloat16

D_MODEL = 1024
DEPTH = 1
GRID_W = 64
H_A = 8
DK = 128
DV = 128
D_B = 1024
FILT_EMB = 33
FILT_BANDS = 16
FILT_ORDER = 64
DECAY_FAST = 0.3
DECAY_SLOW = 1.5
DECAY_TARGET = 1e-2
DECAY_SHIFT = 0.05
D_FF = 2816
N_MOD = 6
W_IN_COLS = 10 * D_MODEL
ALPHA = (2.0 * DEPTH) ** 0.25
LN_EPS = 1e-5
RMS_EPS = 1e-6

LANE = 128
BLK = 256
NFREQ = 2 * BLK
CHUNK = 32
NCHUNK = BLK // CHUNK
HPS = 8
VMEM_LIMIT = 56 * 1024 * 1024

CB_Q, CB_FF, CB_FB, CB_I, CB_G, CB_X0, CB_X1, CB_V, CB_GA, CB_GB = range(10)
STEP_COLS = (CB_Q, CB_FF, CB_FB, CB_G, CB_GA, CB_GB, CB_I, CB_X0, CB_X1, CB_V)
OF_Q, OF_FF, OF_FB, OF_G, OF_GA, OF_GB = range(6)
N_F32_COLS = 6
OB_I, OB_X0, OB_X1, OB_V = range(4)


def _sigmoid(x):
    return 1.0 / (1.0 + jnp.exp(-x))


def _dot(a, b):
    return jnp.dot(a, b, preferred_element_type=F32)


def _dot_nt(a, b):
    return lax.dot_general(a, b, (((1,), (1,)), ((), ())), preferred_element_type=F32)


def _dot_tn(a, b):
    return lax.dot_general(a, b, (((0,), (0,)), ((), ())), preferred_element_type=F32)


def _dot_hi(a, b):
    return jnp.dot(a, b, preferred_element_type=F32, precision=lax.Precision.HIGHEST)


def _params(*sem):
    return pltpu.CompilerParams(dimension_semantics=sem, vmem_limit_bytes=VMEM_LIMIT)


@functools.lru_cache(maxsize=None)
def _dft_consts():
    n = np.arange(BLK, dtype=np.float64)
    f = np.arange(BLK, dtype=np.float64)
    ang = 2.0 * np.pi * np.outer(f, n) / NFREQ
    fwd = np.zeros((NFREQ, BLK), np.float64)
    fwd[:BLK] = np.cos(ang)
    fwd[BLK + 1:] = -np.sin(ang[1:])
    fwd[BLK] = np.cos(np.pi * n)
    inv = np.zeros((BLK, NFREQ), np.float64)
    scale = np.full((BLK,), 2.0)
    scale[0] = 1.0
    inv[:, :BLK] = np.cos(ang.T) * scale[None, :]
    inv[:, BLK + 1:] = -2.0 * np.sin(ang.T[:, 1:])
    inv[:, BLK] = np.cos(np.pi * n)
    inv /= NFREQ
    fr = np.arange(NFREQ)
    freq_of_row = np.where(fr < BLK, fr, np.where(fr == BLK, BLK, fr - BLK))
    sgn = np.where(freq_of_row % 2 == 0, 1.0, -1.0)[:, None]
    return fwd.astype(np.float32), inv.astype(np.float32), sgn.astype(np.float32)


@functools.lru_cache(maxsize=None)
def _scan_consts():
    t = np.arange(BLK)
    ct = t // CHUNK
    same = ct[:, None] == ct[None, :]
    tri_f = (same & (t[None, :] <= t[:, None])).astype(np.float32)
    tri_b = (same & (t[None, :] >= t[:, None])).astype(np.float32)

    def levels(p, diag):
        x = p[:, None] ^ p[None, :]
        lvl = np.zeros_like(x)
        for bit in range(1, NCHUNK.bit_length()):
            lvl = np.where(x >= (1 << (bit - 1)), bit, lvl)
        lv = np.where(p[:, None] > p[None, :], lvl, -1)
        return np.where(same, np.where(diag, 0, -1), lv).astype(np.int32)

    lv_f = levels(ct, t[None, :] <= t[:, None])
    lv_b = levels(NCHUNK - 1 - ct, t[None, :] >= t[:, None])
    return tri_f, tri_b, lv_f, lv_b


@functools.lru_cache(maxsize=None)
def _filter_positions(seq_len):
    f32 = np.float32
    j = np.arange(-seq_len, seq_len)
    p = np.abs(j)
    valid = (j > -seq_len)
    pc = np.minimum(p, seq_len - 1)
    t = np.linspace(0.0, 1.0, seq_len, dtype=f32)[pc]
    wpos = (f32(2.0 * math.pi / seq_len) * np.arange(seq_len, dtype=f32))[pc]
    bands = np.linspace(1e-4, FILT_BANDS - 1, FILT_BANDS, dtype=f32)
    arg = (bands[None, :] * wpos[:, None]).astype(f32)
    z = np.zeros((2 * seq_len, LANE), f32)
    z[:, 0] = t
    z[:, 1:1 + FILT_BANDS] = np.cos(arg)
    z[:, 1 + FILT_BANDS:FILT_EMB] = -np.sin(arg)
    z[:, FILT_EMB] = valid.astype(f32)
    return z


@functools.lru_cache(maxsize=None)
def _decay_rates():
    max_decay = math.log(DECAY_TARGET) / DECAY_FAST
    min_decay = math.log(DECAY_TARGET) / DECAY_SLOW
    return np.abs(np.linspace(min_decay, max_decay, D_B, dtype=np.float32))[None, :]


def _mod_kernel(c_ref, w_ref, b_ref, o_ref):
    c = c_ref[...]
    s = (c * _sigmoid(c)).astype(BF16)
    o_ref[...] = _dot(s, w_ref[...].astype(BF16)) + b_ref[...]


def _modulation(cond8, ada_w, ada_b):
    tn = 1536
    n = N_MOD * D_MODEL
    return pl.pallas_call(
        _mod_kernel,
        grid=(n // tn,),
        in_specs=[pl.BlockSpec((8, D_MODEL), lambda j: (0, 0)),
                  pl.BlockSpec((D_MODEL, tn), lambda j: (0, j)),
                  pl.BlockSpec((1, tn), lambda j: (0, j))],
        out_specs=pl.BlockSpec((8, tn), lambda j: (0, j)),
        out_shape=jax.ShapeDtypeStruct((8, n), F32),
        compiler_params=_params("parallel"),
        name="modulation",
    )(cond8, ada_w, ada_b)


def _lower_bounds(lbl_ref):
    l0 = lbl_ref[0]
    l1 = lbl_ref[1]
    m = jnp.maximum(l0, l1)
    e0 = jnp.exp(l0 - m)
    e1 = jnp.exp(l1 - m)
    return e0 / (e0 + e1)


def _inproj_kernel(x_ref, mod_ref, w_ref, lbl_ref, of_ref, ob_ref, h_ref):
    j = pl.program_id(1)

    @pl.when(j == 0)
    def _():
        h = x_ref[...] * (1.0 + mod_ref[0, 1:2, :]) + mod_ref[0, 0:1, :]
        h_ref[...] = h.astype(BF16)

    def project(o_ref, act):
        for r in range(x_ref.shape[0] // BLK):
            rows = pl.ds(r * BLK, BLK)
            o_ref[rows, :] = act(_dot(h_ref[rows, :], w_ref[...])).astype(o_ref.dtype)

    @pl.when(jnp.logical_or(j == OF_Q, j == OF_G))
    def _():
        project(of_ref, lambda a: a * _sigmoid(a))

    @pl.when(jnp.logical_or(j == OF_GA, j == OF_GB))
    def _():
        project(of_ref, _sigmoid)

    @pl.when(jnp.logical_or(j == OF_FF, j == OF_FB))
    def _():
        lb2 = _lower_bounds(lbl_ref)
        lb = jnp.where(j == OF_FF, lb2[0:1, :], lb2[1:2, :])
        project(of_ref, lambda a: jnp.log(lb + (1.0 - lb) * _sigmoid(a)))

    @pl.when(j >= N_F32_COLS)
    def _():
        project(ob_ref, lambda a: a)


def _weight_col(j):
    col = jnp.int32(STEP_COLS[-1])
    for step in reversed(range(len(STEP_COLS) - 1)):
        col = jnp.where(j == step, STEP_COLS[step], col)
    return col


def _inproj(x, mod3, w_bf, lb_logits, cond_row):
    tm = 2048
    t = x.shape[0]
    n_b16 = len(STEP_COLS) - N_F32_COLS
    return pl.pallas_call(
        _inproj_kernel,
        grid=(t // tm, len(STEP_COLS)),
        in_specs=[pl.BlockSpec((tm, D_MODEL), lambda i, j: (i, 0)),
                  pl.BlockSpec((1, N_MOD, D_MODEL), lambda i, j: (cond_row(i * tm), 0, 0)),
                  pl.BlockSpec((D_MODEL, D_MODEL), lambda i, j: (0, _weight_col(j))),
                  pl.BlockSpec((2, 2, D_MODEL), lambda i, j: (0, 0, 0))],
        out_specs=[pl.BlockSpec((tm, D_MODEL), lambda i, j: (i, jnp.minimum(j, N_F32_COLS - 1))),
                   pl.BlockSpec((tm, D_MODEL), lambda i, j: (i, jnp.maximum(j - N_F32_COLS, 0)))],
        out_shape=[jax.ShapeDtypeStruct((t, N_F32_COLS * D_MODEL), F32),
                   jax.ShapeDtypeStruct((t, n_b16 * D_MODEL), BF16)],
        scratch_shapes=[pltpu.VMEM((tm, D_MODEL), BF16)],
        compiler_params=_params("parallel", "arbitrary"),
        name="inproj",
    )(x, mod3, w_bf, lb_logits)


def _chunk_cumsum(lf, tri):
    lf_hi = lf.astype(BF16)
    lf_lo = (lf - lf_hi.astype(F32)).astype(BF16)
    return _dot(tri, lf_hi) + _dot(tri, lf_lo)


def _hgrn_direction(q, lf, b, v, st, lv, reverse, use_state):
    k = 1.0 - jnp.exp(lf)
    qe = q * jnp.exp(b)
    k0 = k * jnp.exp(-b)

    order = [NCHUNK - 1 - i for i in range(NCHUNK)] if reverse else list(range(NCHUNK))
    chunk_of = {p: i for i, p in enumerate(order)}
    sl = lambda i: slice(i * CHUNK, (i + 1) * CHUNK)
    last_row = lambda i: (i * CHUNK) if reverse else (i * CHUNK + CHUNK - 1)
    c = [None] * NCHUNK
    for p in range(NCHUNK):
        r = last_row(chunk_of[p])
        c[p] = b[r:r + 1, :]
    cum = [jnp.zeros_like(c[0])]
    for p in range(NCHUNK):
        cum.append(cum[p] + c[p])
    total = cum[NCHUNK]

    qe_c, ke_c = {}, {}
    for p in range(NCHUNK):
        i = chunk_of[p]
        qe_c[p] = qe[sl(i), :]
        ke_c[p] = k0[sl(i), :] * jnp.exp(c[p])

    def assemble(parts):
        return jnp.concatenate([parts[order[i]] for i in range(NCHUNK)], axis=0).astype(BF16)

    cph = NCHUNK // 2
    nlev = NCHUNK.bit_length() - 1

    def half_rows(hh):
        first = (1 - hh) if reverse else hh
        return slice(first * cph * CHUNK, (first + 1) * cph * CHUNK)

    def assemble_half(parts, hh):
        ps = range(hh * cph, (hh + 1) * cph)
        return jnp.concatenate([parts[p] for p in (reversed(ps) if reverse else ps)], axis=0).astype(BF16)

    zero = jnp.zeros((CHUNK, LANE), F32)
    qe_bf, k0_bf = qe.astype(BF16), k0.astype(BF16)
    s_half = [jnp.where(lv == 0, _dot_nt(qe_bf[half_rows(hh), :], k0_bf[half_rows(hh), :]), 0.0)
              for hh in range(2)]
    for lev in range(1, nlev):
        mid = 1 << (lev - 1)
        qp, kp = {}, {}
        for p in range(NCHUNK):
            pm = ((p >> lev) << lev) + mid
            if p >= pm:
                qp[p] = qe_c[p] * jnp.exp(cum[p] - cum[pm])
                kp[p] = zero
            else:
                qp[p] = zero
                kp[p] = ke_c[p] * jnp.exp(cum[pm] - cum[p + 1])
        for hh in range(2):
            s_lev = _dot_nt(assemble_half(qp, hh), assemble_half(kp, hh))
            s_half[hh] = jnp.where(lv == lev, s_lev, s_half[hh])
    q_top = assemble_half({p: qe_c[p] * jnp.exp(cum[p] - cum[cph]) for p in range(cph, NCHUNK)}, 1)
    k_top = assemble_half({p: ke_c[p] * jnp.exp(cum[cph] - cum[p + 1]) for p in range(cph)}, 0)
    s_top = _dot_nt(q_top, k_top)

    v_bf = v.astype(BF16)
    v_half = [v_bf[half_rows(hh), :] for hh in range(2)]
    out_half = [_dot(s_half[0].astype(BF16), v_half[0]),
                _dot(jnp.concatenate([s_top, s_half[1]], axis=1).astype(BF16),
                     jnp.concatenate(v_half, axis=0))]
    out = jnp.concatenate(out_half[::-1] if reverse else out_half, axis=0)
    if use_state:
        q_start = assemble({p: qe_c[p] * jnp.exp(cum[p]) for p in range(NCHUNK)})
        out = out + _dot_nt(q_start, st.astype(BF16))
    k_end = assemble({p: ke_c[p] * jnp.exp(total - cum[p + 1]) for p in range(NCHUNK)})
    upd = _dot_tn(v_bf, k_end)
    new_st = st * jnp.exp(total) + upd if use_state else upd
    return out, new_st


def _hgrn_kernel(*refs, nb, zero_init, emit_state):
    it = iter(refs)
    qf_ref, lff_ref, vf_ref, qb_ref, lfb_ref, vb_ref = [next(it) for _ in range(6)]
    s0_ref = None if zero_init else next(it)
    lvf_ref, lvb_ref, trif_ref, trib_ref = [next(it) for _ in range(4)]
    of_ref = next(it)
    ob_ref = next(it) if nb > 1 else None
    so_ref = next(it) if emit_state else None
    st_ref = next(it)
    i = pl.program_id(2)

    use_state = not (zero_init and nb == 1)
    if use_state:
        @pl.when(i == 0)
        def _():
            for d in range(2):
                for h in range(HPS):
                    st_ref[d, h] = jnp.zeros((DV, DK), F32) if zero_init else s0_ref[0, 0, d, h].T

    lf_f = lff_ref[...]
    lf_b = lfb_ref[...]
    b_f = _chunk_cumsum(lf_f, trif_ref[...])
    b_b = _chunk_cumsum(lf_b, trib_ref[...])
    lv_f = lvf_ref[...]
    lv_b = lvb_ref[...]
    for h in range(HPS):
        hs = slice(h * LANE, (h + 1) * LANE)
        of, stf = _hgrn_direction(qf_ref[:, hs], lf_f[:, hs], b_f[:, hs], vf_ref[:, hs], st_ref[0, h],
                                  lv_f, False, use_state)
        ob, stb = _hgrn_direction(qb_ref[:, hs], lf_b[:, hs], b_b[:, hs], vb_ref[:, hs], st_ref[1, h],
                                  lv_b, True, use_state)
        if nb > 1:
            of_ref[:, hs] = of
            ob_ref[:, hs] = ob
        else:
            of_ref[:, hs] = of + ob
        if nb > 1:
            st_ref[0, h] = stf
            st_ref[1, h] = stb
        if emit_state:
            @pl.when(i == nb - 1)
            def _():
                so_ref[0, 0, 0, h] = stf.T
                so_ref[0, 0, 1, h] = stb.T


def _hgrn(pf, pb, state, nseq, nb, emit_state):
    zero_init = state is None
    t = pf.shape[0]
    tri_f, tri_b, lv_f, lv_b = _scan_consts()
    wid = HPS * LANE
    per = D_MODEL // wid
    fwd = lambda cb: pl.BlockSpec((BLK, wid), lambda b, h, i, cb=cb: (b * nb + i, cb * per + h))
    bwd = lambda cb: pl.BlockSpec((BLK, wid), lambda b, h, i, cb=cb: (b * nb + nb - 1 - i, cb * per + h))
    const = lambda n=BLK: pl.BlockSpec((n, n), lambda b, h, i: (0, 0))
    hl = BLK // 2
    st_spec = pl.BlockSpec((1, 1, 2, HPS, DK, DV), lambda b, h, i: (b, 0, 0, h, 0, 0))
    in_specs = [fwd(OF_Q), fwd(OF_FF), fwd(OB_I), bwd(OF_Q), bwd(OF_FB), bwd(OB_I)]
    args = [pf, pf, pb, pf, pf, pb]
    if not zero_init:
        in_specs.append(st_spec)
        args.append(state)
    in_specs += [const(hl), const(hl), const(), const()]
    args += [jnp.asarray(lv_f[:hl, :hl]), jnp.asarray(lv_b[:hl, :hl]),
             jnp.asarray(tri_f, BF16), jnp.asarray(tri_b, BF16)]
    out_specs = [pl.BlockSpec((BLK, wid), lambda b, h, i: (b * nb + i, h))]
    out_shape = [jax.ShapeDtypeStruct((t, D_MODEL), F32)]
    if nb > 1:
        out_specs.append(pl.BlockSpec((BLK, wid), lambda b, h, i: (b * nb + nb - 1 - i, h)))
        out_shape.append(jax.ShapeDtypeStruct((t, D_MODEL), F32))
    if emit_state:
        out_specs.append(st_spec)
        out_shape.append(jax.ShapeDtypeStruct((nseq, DEPTH, 2, H_A, DK, DV), F32))
    return pl.pallas_call(
        functools.partial(_hgrn_kernel, nb=nb, zero_init=zero_init, emit_state=emit_state),
        grid=(nseq, H_A // HPS, nb),
        in_specs=in_specs,
        out_specs=out_specs,
        out_shape=out_shape,
        scratch_shapes=[pltpu.VMEM((2, HPS, DV, DK), F32)],
        compiler_params=_params("parallel", "parallel", "arbitrary"),
        name="hgrn_scan",
    )(*args)


def _filter_kernel(z_ref, w1_ref, b1_ref, w2_ref, b2_ref, w3_ref, b3_ref, fq_ref, w4_ref, dec_ref,
                   fhi_ref, flo_ref, sgn_ref, o_ref, prev_ref):
    zp = z_ref[...]
    h = jnp.sin(fq_ref[0:1, :] * (_dot_hi(zp, w1_ref[...]) + b1_ref[...]))
    h = jnp.sin(fq_ref[1:2, :] * (_dot_hi(h, w2_ref[...]) + b2_ref[...]))
    h = jnp.sin(fq_ref[2:3, :] * (_dot_hi(h, w3_ref[...]) + b3_ref[...]))
    a = _dot_hi(h, w4_ref[...])
    window = jnp.exp(-zp[:, 0:1] * dec_ref[...]) + DECAY_SHIFT
    a = a * window * zp[:, FILT_EMB:FILT_EMB + 1]
    a_hi = a.astype(BF16)
    a_lo = (a - a_hi.astype(F32)).astype(BF16)
    f_hi = fhi_ref[...]
    ah = _dot(f_hi, a_hi) + (_dot(f_hi, a_lo) + _dot(flo_ref[...], a_hi))
    o_ref[0] = ah + sgn_ref[...] * prev_ref[...]
    prev_ref[...] = ah


def _filter_spectra(seq_len, w1, b1, w2, b2, w3, b3, freq, w4):
    nb = seq_len // BLK
    fwd_dft, _, sgn = _dft_consts()
    zpos = jnp.asarray(_filter_positions(seq_len))
    f_full = jnp.asarray(fwd_dft)
    f_hi = f_full.astype(BF16)
    f_lo = (f_full - f_hi.astype(F32)).astype(BF16)
    pad2 = lambda w: jnp.pad(w, ((0, LANE - w.shape[0]), (0, LANE - w.shape[1])))
    padv = lambda b: jnp.pad(b, (0, LANE - b.shape[0]))[None, :]
    w1p, w2p, w3p = pad2(w1), pad2(w2), pad2(w3)
    b1p, b2p, b3p = padv(b1), padv(b2), padv(b3)
    fqp = jnp.pad(freq, ((0, 0), (0, LANE - freq.shape[1])))
    w4p = jnp.pad(w4, ((0, LANE - w4.shape[0]), (0, 0)))
    small = lambda shape: pl.BlockSpec(shape, lambda m: (0, 0))
    return pl.pallas_call(
        _filter_kernel,
        grid=(2 * nb,),
        in_specs=[pl.BlockSpec((BLK, LANE), lambda m: (m, 0)),
                  small((LANE, LANE)), small((1, LANE)), small((LANE, LANE)), small((1, LANE)),
                  small((LANE, LANE)), small((1, LANE)), small((3, LANE)),
                  pl.BlockSpec((LANE, D_B), lambda m: (0, jnp.where(m < nb, 1, 0))),
                  small((1, D_B)), small((NFREQ, BLK)), small((NFREQ, BLK)), small((NFREQ, 1))],
        out_specs=pl.BlockSpec((1, NFREQ, D_B), lambda m: (jnp.maximum(m - 1, 0), 0, 0)),
        out_shape=jax.ShapeDtypeStruct((2 * nb - 1, NFREQ, D_B), F32),
        scratch_shapes=[pltpu.VMEM((NFREQ, D_B), F32)],
        compiler_params=_params("arbitrary"),
        name="hyena_filter",
    )(zpos, w1p, b1p, w2p, b2p, w3p, b3p, fqp, w4p, jnp.asarray(_decay_rates()),
      f_hi, f_lo, jnp.asarray(sgn))


def _short_conv_gate(u0, u1, uv, w_refs, b_refs, row_len):
    t = lax.broadcasted_iota(jnp.int32, (BLK, 1), 0)
    first = (t % row_len) == 0
    last = (t % row_len) == (row_len - 1)

    def conv(u, w_ref, b_ref):
        u = u.astype(F32)
        up = jnp.where(first, 0.0, pltpu.roll(u, 1, 0))
        dn = jnp.where(last, 0.0, pltpu.roll(u, BLK - 1, 0))
        return up * w_ref[0:1, :] + u * w_ref[1:2, :] + dn * w_ref[2:3, :] + b_ref[...]

    return (conv(u0, w_refs[0], b_refs[0]),
            conv(uv, w_refs[2], b_refs[2]) * conv(u1, w_refs[1], b_refs[1]))


def _hy_single_kernel(x0_ref, x1_ref, v_ref, w0_ref, w1_ref, wv_ref, b0_ref, b1_ref, bv_ref, f_ref, kh_ref,
                      skip_ref, g_ref, o_ref, *, row_len):
    x0, z = _short_conv_gate(x0_ref[...], x1_ref[...], v_ref[...], (w0_ref, w1_ref, wv_ref),
                             (b0_ref, b1_ref, bv_ref), row_len)
    zh = _dot(f_ref[...], z.astype(BF16))
    zr, zi = zh[:BLK, :], zh[BLK:, :]
    kr, ki = kh_ref[0, :BLK, :], kh_ref[0, BLK:, :]
    p = zr * kr
    q = zi * ki
    r = zr * ki + zi * kr
    row0 = lax.broadcasted_iota(jnp.int32, (BLK, 1), 0) == 0
    yh = jnp.concatenate([jnp.where(row0, p, p - q), jnp.where(row0, q, r)], axis=0).astype(BF16)
    y = _dot(g_ref[...], yh) + z * skip_ref[...]
    o_ref[...] = (x0 * y).astype(o_ref.dtype)


def _hyena_single(pb, khat, conv_w, conv_b, skip, row_len):
    t = pb.shape[0]
    fwd_dft, inv_dft, _ = _dft_consts()
    col = lambda cb: pl.BlockSpec((BLK, D_B), lambda i, cb=cb: (i, cb))
    wcol = lambda r, k: pl.BlockSpec((r, D_B), lambda i, k=k: (0, k))
    conv_b = conv_b[None, :]
    return pl.pallas_call(
        functools.partial(_hy_single_kernel, row_len=row_len),
        grid=(t // BLK,),
        in_specs=[col(OB_X0), col(OB_X1), col(OB_V),
                  wcol(3, 0), wcol(3, 1), wcol(3, 2), wcol(1, 0), wcol(1, 1), wcol(1, 2),
                  pl.BlockSpec((NFREQ, BLK), lambda i: (0, 0)),
                  pl.BlockSpec((1, NFREQ, D_B), lambda i: (0, 0, 0)),
                  pl.BlockSpec((1, D_B), lambda i: (0, 0)),
                  pl.BlockSpec((BLK, NFREQ), lambda i: (0, 0))],
        out_specs=pl.BlockSpec((BLK, D_B), lambda i: (i, 0)),
        out_shape=jax.ShapeDtypeStruct((t, D_B), BF16),
        compiler_params=_params("parallel"),
        name="hyena_single",
    )(pb, pb, pb, conv_w, conv_w, conv_w, conv_b, conv_b, conv_b, jnp.asarray(fwd_dft).astype(BF16), khat,
      skip[None, :], jnp.asarray(inv_dft).astype(BF16))


ROWG = 32


def _hy_multi_kernel(x0_ref, x1_ref, v_ref, w0_ref, w1_ref, wv_ref, b0_ref, b1_ref, bv_ref, f_ref, kh_ref,
                     skip_ref, g_ref, o_ref, zh_ref, z_ref, x0s_ref, yh_ref, *, nb, row_len):
    dt = o_ref.shape[1]

    def front(blk, carry):
        rows = pl.ds(pl.multiple_of(blk * BLK, BLK), BLK)
        x0, z = _short_conv_gate(x0_ref[rows, :], x1_ref[rows, :], v_ref[rows, :],
                                 (w0_ref, w1_ref, wv_ref), (b0_ref, b1_ref, bv_ref), row_len)
        x0s_ref[rows, :] = x0
        z_ref[rows, :] = z
        zh_ref[blk] = _dot(f_ref[...], z.astype(BF16))
        return carry

    lax.fori_loop(0, nb, front, 0)

    row_in_group = lax.broadcasted_iota(jnp.int32, (ROWG, 1), 0)

    def back(i, carry):
        def row_group(rg, carry2):
            re = pl.ds(pl.multiple_of(rg * ROWG, ROWG), ROWG)
            im = pl.ds(pl.multiple_of(BLK + rg * ROWG, ROWG), ROWG)

            def body(j, acc):
                p, q, r = acc
                kidx = i - j + (nb - 1)
                zr = zh_ref[j, re, :]
                zi = zh_ref[j, im, :]
                kr = kh_ref[kidx, re, :]
                ki = kh_ref[kidx, im, :]
                return (p + zr * kr, q + zi * ki, r + (zr * ki + zi * kr))

            zeros = jnp.zeros((ROWG, dt), F32)
            p, q, r = lax.fori_loop(0, nb, body, (zeros, zeros, zeros), unroll=True)
            row0 = (row_in_group + rg * ROWG) == 0
            yh_ref[re, :] = jnp.where(row0, p, p - q).astype(BF16)
            yh_ref[im, :] = jnp.where(row0, q, r).astype(BF16)
            return carry2

        lax.fori_loop(0, BLK // ROWG, row_group, 0)
        rows = pl.ds(pl.multiple_of(i * BLK, BLK), BLK)
        y = _dot(g_ref[...], yh_ref[...]) + z_ref[rows, :] * skip_ref[...]
        o_ref[rows, :] = (x0s_ref[rows, :] * y).astype(o_ref.dtype)
        return carry

    lax.fori_loop(0, nb, back, 0)


def _hyena_multi(pb, khat, conv_w, conv_b, skip, nseq, nb, row_len):
    t = pb.shape[0]
    seq_len = nb * BLK
    dt = LANE
    per = D_B // dt
    fwd_dft, inv_dft, _ = _dft_consts()
    seq = lambda cb: pl.BlockSpec((seq_len, dt), lambda d, b, cb=cb: (b, cb * per + d))
    wcol = lambda r, k: pl.BlockSpec((r, dt), lambda d, b, k=k: (0, k * per + d))
    conv_b = conv_b[None, :]
    return pl.pallas_call(
        functools.partial(_hy_multi_kernel, nb=nb, row_len=row_len),
        grid=(per, nseq),
        in_specs=[seq(OB_X0), seq(OB_X1), seq(OB_V),
                  wcol(3, 0), wcol(3, 1), wcol(3, 2), wcol(1, 0), wcol(1, 1), wcol(1, 2),
                  pl.BlockSpec((NFREQ, BLK), lambda d, b: (0, 0)),
                  pl.BlockSpec((2 * nb - 1, NFREQ, dt), lambda d, b: (0, 0, d)),
                  pl.BlockSpec((1, dt), lambda d, b: (0, d)),
                  pl.BlockSpec((BLK, NFREQ), lambda d, b: (0, 0))],
        out_specs=pl.BlockSpec((seq_len, dt), lambda d, b: (b, d)),
        out_shape=jax.ShapeDtypeStruct((t, D_B), BF16),
        scratch_shapes=[pltpu.VMEM((nb, NFREQ, dt), F32), pltpu.VMEM((seq_len, dt), F32),
                        pltpu.VMEM((seq_len, dt), F32), pltpu.VMEM((NFREQ, dt), BF16)],
        compiler_params=_params("parallel", "parallel"),
        name="hyena_multi",
    )(pb, pb, pb, conv_w, conv_w, conv_w, conv_b, conv_b, conv_b, jnp.asarray(fwd_dft).astype(BF16), khat,
      skip[None, :], jnp.asarray(inv_dft).astype(BF16))


def _layer_norm(y, g, b):
    mu = jnp.mean(y, axis=-1, keepdims=True)
    yc = y - mu
    var = jnp.mean(yc * yc, axis=-1, keepdims=True)
    return yc * lax.rsqrt(var + LN_EPS) * g + b


def _merge_kernel(*refs, n_dir):
    (g_ref, ga_ref, gb_ref, hy_ref, x_ref, mod_ref, nw_ref,
     pa_ref, pb_ref, wo_ref, lg_ref, lb_ref, o_ref) = refs[n_dir:]
    o = refs[0][...]
    for d_ref in refs[1:n_dir]:
        o = o + d_ref[...]
    nw = nw_ref[...]
    parts = []
    for h in range(H_A):
        oh = o[:, h * DV:(h + 1) * DV]
        ms = jnp.mean(oh * oh, axis=-1, keepdims=True)
        parts.append(oh * lax.rsqrt(ms + RMS_EPS) * nw)
    oa = jnp.concatenate(parts, axis=1) * g_ref[...]
    a = _dot(oa.astype(BF16), pa_ref[...])
    b = _dot(hy_ref[...].astype(BF16), pb_ref[...])
    merged = ga_ref[...] * a + gb_ref[...] * b
    mix = _dot(merged.astype(BF16), wo_ref[...])
    y = ALPHA * x_ref[...] + mod_ref[0, 2:3, :] * mix
    o_ref[...] = _layer_norm(y, lg_ref[...], lb_ref[...])


def _merge(o_dirs, proj, o_hy, x, mod3, cond_row, norm_w, pa, pb, wo, ln_g, ln_b):
    tm = 256
    t = x.shape[0]
    n_dir = len(o_dirs)
    tok = pl.BlockSpec((tm, D_MODEL), lambda i: (i, 0))
    col = lambda cb: pl.BlockSpec((tm, D_MODEL), lambda i, cb=cb: (i, cb))
    mat = pl.BlockSpec((D_MODEL, D_MODEL), lambda i: (0, 0))
    vec = pl.BlockSpec((1, D_MODEL), lambda i: (0, 0))
    return pl.pallas_call(
        functools.partial(_merge_kernel, n_dir=n_dir),
        grid=(t // tm,),
        in_specs=[tok] * n_dir + [col(OF_G), col(OF_GA), col(OF_GB), tok, tok,
                  pl.BlockSpec((1, N_MOD, D_MODEL), lambda i: (cond_row(i * tm), 0, 0)),
                  pl.BlockSpec((1, DV), lambda i: (0, 0)),
                  mat, mat, mat, vec, vec],
        out_specs=tok,
        out_shape=jax.ShapeDtypeStruct((t, D_MODEL), F32),
        compiler_params=_params("parallel"),
        name="merge",
    )(*o_dirs, proj, proj, proj, o_hy, x, mod3, norm_w[None, :], pa, pb, wo, ln_g[None, :], ln_b[None, :])


def _ffn_kernel(x_ref, mod_ref, wg_ref, wu_ref, wo_ref, lg_ref, lb_ref, o_ref):
    x = x_ref[...]
    h = (x * (1.0 + mod_ref[0, 4:5, :]) + mod_ref[0, 3:4, :]).astype(BF16)
    gt = _dot(h, wg_ref[...])
    up = _dot(h, wu_ref[...])
    act = (gt * _sigmoid(gt) * up).astype(BF16)
    ff = _dot(act, wo_ref[...])
    y = ALPHA * x + mod_ref[0, 5:6, :] * ff
    o_ref[...] = _layer_norm(y, lg_ref[...], lb_ref[...])


def _ffn(x, mod3, cond_row, w_in, w_out, ln_g, ln_b):
    tm = 256
    t = x.shape[0]
    tok = pl.BlockSpec((tm, D_MODEL), lambda i: (i, 0))
    vec = pl.BlockSpec((1, D_MODEL), lambda i: (0, 0))
    return pl.pallas_call(
        _ffn_kernel,
        grid=(t // tm,),
        in_specs=[tok,
                  pl.BlockSpec((1, N_MOD, D_MODEL), lambda i: (cond_row(i * tm), 0, 0)),
                  pl.BlockSpec((D_MODEL, D_FF), lambda i: (0, 0)),
                  pl.BlockSpec((D_MODEL, D_FF), lambda i: (0, 1)),
                  pl.BlockSpec((D_FF, D_MODEL), lambda i: (0, 0)),
                  vec, vec],
        out_specs=tok,
        out_shape=jax.ShapeDtypeStruct((t, D_MODEL), F32),
        compiler_params=_params("parallel"),
        name="ffn",
    )(x, mod3, w_in, w_in, w_out, ln_g[None, :], ln_b[None, :])


def _trunk(x, mod3, cond_row, nseq, seq_len, row_len, state, emit_state, w, lb_logits):
    nb = seq_len // BLK
    pf, pb = _inproj(x, mod3, w["w_in"], lb_logits, cond_row)
    hg = _hgrn(pf, pb, state, nseq, nb, emit_state)
    n_dir = 2 if nb > 1 else 1
    khat = _filter_spectra(seq_len, w["filt_w1"], w["filt_b1"], w["filt_w2"], w["filt_b2"],
                           w["filt_w3"], w["filt_b3"], w["filt_freq"], w["filt_w4"])
    if nb == 1:
        o_hy = _hyena_single(pb, khat, w["hy_conv_w"], w["hy_conv_b"], w["hy_skip"], row_len)
    else:
        o_hy = _hyena_multi(pb, khat, w["hy_conv_w"], w["hy_conv_b"], w["hy_skip"], nseq, nb, row_len)
    x1 = _merge(hg[:n_dir], pf, o_hy, x, mod3, cond_row, w["hgrn_norm_w"], w["proj_a"], w["proj_b"],
                w["w_out"], w["ln1_g"], w["ln1_b"])
    x2 = _ffn(x1, mod3, cond_row, w["ffn_w_in"], w["ffn_w_out"], w["ln2_g"], w["ln2_b"])
    return x2, (hg[n_dir] if emit_state else None)


def kernel(x_prompt, x_sample, state_hgrn, c, c_ctx, ada_w, ada_b, w_in, hgrn_lb_logits, hgrn_norm_w,
           hy_conv_w, hy_conv_b, filt_w1, filt_b1, filt_w2, filt_b2, filt_w3, filt_b3, filt_freq, filt_w4,
           hy_skip, proj_a, proj_b, w_out, ln1_g, ln1_b, ffn_w_in, ffn_w_out, ln2_g, ln2_b):
    assert ada_w.shape[0] == DEPTH == 1
    batch, seq, _ = x_prompt.shape
    dec_batch, dec_seq, _ = x_sample.shape
    assert seq % BLK == 0 and dec_seq % BLK == 0 and BLK % GRID_W == 0 and dec_batch + 1 <= 8

    w = dict(w_in=w_in[0].astype(BF16), hy_conv_w=hy_conv_w[0], hy_conv_b=hy_conv_b[0],
             filt_w1=filt_w1[0], filt_b1=filt_b1[0], filt_w2=filt_w2[0], filt_b2=filt_b2[0],
             filt_w3=filt_w3[0], filt_b3=filt_b3[0], filt_freq=filt_freq[0], filt_w4=filt_w4[0],
             hy_skip=hy_skip[0], hgrn_norm_w=hgrn_norm_w[0], proj_a=proj_a[0].astype(BF16),
             proj_b=proj_b[0].astype(BF16), w_out=w_out[0].astype(BF16), ln1_g=ln1_g[0], ln1_b=ln1_b[0],
             ffn_w_in=ffn_w_in[0].astype(BF16), ffn_w_out=ffn_w_out[0].astype(BF16),
             ln2_g=ln2_g[0], ln2_b=ln2_b[0])

    cond8 = jnp.zeros((8, D_MODEL), F32).at[0].set(c_ctx).at[1:1 + dec_batch].set(c)
    mod3 = _modulation(cond8, ada_w[0], ada_b[0][None, :]).reshape(8, N_MOD, D_MODEL)

    xp = x_prompt.reshape(batch * seq, D_MODEL)
    xs = x_sample.reshape(dec_batch * dec_seq, D_MODEL)
    yp, new_state = _trunk(xp, mod3, lambda tok: 0, batch, seq, seq, None, True, w, hgrn_lb_logits)
    ys, _ = _trunk(xs, mod3, lambda tok: 1 + tok // dec_seq, dec_batch, dec_seq, GRID_W, state_hgrn, False,
                   w, hgrn_lb_logits)
    return (yp.reshape(batch, seq, D_MODEL), ys.reshape(dec_batch, dec_seq, D_MODEL), new_state)
```

```python
import functools
import math

import numpy as np
import jax
import jax.numpy as jnp
from jax import lax
from jax.experimental import pallas as pl
from jax.experimental.pallas import tpu as pltpu

F32 = jnp.float32
BF16 = jnp.bfloat16

D_MODEL = 1024
DEPTH = 1
GRID_W = 64
H_A = 8
DK = 128
DV = 128
D_B = 1024
FILT_EMB = 33
FILT_BANDS = 16
FILT_ORDER = 64
DECAY_FAST = 0.3
DECAY_SLOW = 1.5
DECAY_TARGET = 1e-2
DECAY_SHIFT = 0.05
D_FF = 2816
N_MOD = 6
W_IN_COLS = 10 * D_MODEL
ALPHA = (2.0 * DEPTH) ** 0.25
LN_EPS = 1e-5
RMS_EPS = 1e-6

LANE = 128
BLK = 256
NFREQ = 2 * BLK
CHUNK = 32
NCHUNK = BLK // CHUNK
HPS = 8
VMEM_LIMIT = 56 * 1024 * 1024

CB_Q, CB_FF, CB_FB, CB_I, CB_G, CB_X0, CB_X1, CB_V, CB_GA, CB_GB = range(10)
STEP_COLS = (CB_Q, CB_FF, CB_FB, CB_G, CB_GA, CB_GB, CB_I, CB_X0, CB_X1, CB_V)
N_F32_COLS = 3
OF_Q, OF_FF, OF_FB = range(N_F32_COLS)
OB_G, OB_GA, OB_GB, OB_I, OB_X0, OB_X1, OB_V = range(len(STEP_COLS) - N_F32_COLS)


def _sigmoid(x):
    return 1.0 / (1.0 + jnp.exp(-x))


def _dot(a, b):
    return jnp.dot(a, b, preferred_element_type=F32)


def _dot_nt(a, b):
    return lax.dot_general(a, b, (((1,), (1,)), ((), ())), preferred_element_type=F32)


def _dot_tn(a, b):
    return lax.dot_general(a, b, (((0,), (0,)), ((), ())), preferred_element_type=F32)


def _dot_hi(a, b):
    return jnp.dot(a, b, preferred_element_type=F32, precision=lax.Precision.HIGHEST)


def _params(*sem):
    return pltpu.CompilerParams(dimension_semantics=sem, vmem_limit_bytes=VMEM_LIMIT)


@functools.lru_cache(maxsize=None)
def _dft_consts():
    n = np.arange(BLK, dtype=np.float64)
    f = np.arange(BLK, dtype=np.float64)
    ang = 2.0 * np.pi * np.outer(f, n) / NFREQ
    fwd = np.zeros((NFREQ, BLK), np.float64)
    fwd[:BLK] = np.cos(ang)
    fwd[BLK + 1:] = -np.sin(ang[1:])
    fwd[BLK] = np.cos(np.pi * n)
    inv = np.zeros((BLK, NFREQ), np.float64)
    scale = np.full((BLK,), 2.0)
    scale[0] = 1.0
    inv[:, :BLK] = np.cos(ang.T) * scale[None, :]
    inv[:, BLK + 1:] = -2.0 * np.sin(ang.T[:, 1:])
    inv[:, BLK] = np.cos(np.pi * n)
    inv /= NFREQ
    fr = np.arange(NFREQ)
    freq_of_row = np.where(fr < BLK, fr, np.where(fr == BLK, BLK, fr - BLK))
    sgn = np.where(freq_of_row % 2 == 0, 1.0, -1.0)[:, None]
    return fwd.astype(np.float32), inv.astype(np.float32), sgn.astype(np.float32)


@functools.lru_cache(maxsize=None)
def _scan_consts():
    t = np.arange(BLK)
    ct = t // CHUNK
    same = ct[:, None] == ct[None, :]
    tri_f = (same & (t[None, :] <= t[:, None])).astype(np.float32)
    tri_b = (same & (t[None, :] >= t[:, None])).astype(np.float32)

    def levels(p, diag):
        x = p[:, None] ^ p[None, :]
        lvl = np.zeros_like(x)
        for bit in range(1, NCHUNK.bit_length()):
            lvl = np.where(x >= (1 << (bit - 1)), bit, lvl)
        lv = np.where(p[:, None] > p[None, :], lvl, -1)
        return np.where(same, np.where(diag, 0, -1), lv).astype(np.int32)

    lv_f = levels(ct, t[None, :] <= t[:, None])
    lv_b = levels(NCHUNK - 1 - ct, t[None, :] >= t[:, None])
    return tri_f, tri_b, lv_f, lv_b


@functools.lru_cache(maxsize=None)
def _filter_positions(seq_len):
    f32 = np.float32
    j = np.arange(-seq_len, seq_len)
    p = np.abs(j)
    valid = (j > -seq_len)
    pc = np.minimum(p, seq_len - 1)
    t = np.linspace(0.0, 1.0, seq_len, dtype=f32)[pc]
    wpos = (f32(2.0 * math.pi / seq_len) * np.arange(seq_len, dtype=f32))[pc]
    bands = np.linspace(1e-4, FILT_BANDS - 1, FILT_BANDS, dtype=f32)
    arg = (bands[None, :] * wpos[:, None]).astype(f32)
    z = np.zeros((2 * seq_len, LANE), f32)
    z[:, 0] = t
    z[:, 1:1 + FILT_BANDS] = np.cos(arg)
    z[:, 1 + FILT_BANDS:FILT_EMB] = -np.sin(arg)
    z[:, FILT_EMB] = valid.astype(f32)
    return z


@functools.lru_cache(maxsize=None)
def _decay_rates():
    max_decay = math.log(DECAY_TARGET) / DECAY_FAST
    min_decay = math.log(DECAY_TARGET) / DECAY_SLOW
    return np.abs(np.linspace(min_decay, max_decay, D_B, dtype=np.float32))[None, :]


def _mod_kernel(c_ref, w_ref, b_ref, o_ref):
    c = c_ref[...]
    s = (c * _sigmoid(c)).astype(BF16)
    o_ref[...] = _dot(s, w_ref[...].astype(BF16)) + b_ref[...]


def _modulation(cond8, ada_w, ada_b):
    tn = 1536
    n = N_MOD * D_MODEL
    return pl.pallas_call(
        _mod_kernel,
        grid=(n // tn,),
        in_specs=[pl.BlockSpec((8, D_MODEL), lambda j: (0, 0)),
                  pl.BlockSpec((D_MODEL, tn), lambda j: (0, j)),
                  pl.BlockSpec((1, tn), lambda j: (0, j))],
        out_specs=pl.BlockSpec((8, tn), lambda j: (0, j)),
        out_shape=jax.ShapeDtypeStruct((8, n), F32),
        compiler_params=_params("parallel"),
        name="modulation",
    )(cond8, ada_w, ada_b)


def _lower_bounds(lbl_ref):
    l0 = lbl_ref[0]
    l1 = lbl_ref[1]
    m = jnp.maximum(l0, l1)
    e0 = jnp.exp(l0 - m)
    e1 = jnp.exp(l1 - m)
    return e0 / (e0 + e1)


def _inproj_kernel(x_ref, mod_ref, w_ref, lbl_ref, of_ref, ob_ref, h_ref):
    j = pl.program_id(1)

    @pl.when(j == 0)
    def _():
        h = x_ref[...] * (1.0 + mod_ref[0, 1:2, :]) + mod_ref[0, 0:1, :]
        h_ref[...] = h.astype(BF16)

    def project(o_ref, act):
        for r in range(x_ref.shape[0] // BLK):
            rows = pl.ds(r * BLK, BLK)
            o_ref[rows, :] = act(_dot(h_ref[rows, :], w_ref[...])).astype(o_ref.dtype)

    silu = lambda a: a * _sigmoid(a)
    step = STEP_COLS.index

    @pl.when(j == step(CB_Q))
    def _():
        project(of_ref, silu)

    @pl.when(jnp.logical_or(j == step(CB_FF), j == step(CB_FB)))
    def _():
        lb2 = _lower_bounds(lbl_ref)
        lb = jnp.where(j == step(CB_FF), lb2[0:1, :], lb2[1:2, :])
        project(of_ref, lambda a: jnp.log(lb + (1.0 - lb) * _sigmoid(a)))

    @pl.when(j == step(CB_G))
    def _():
        project(ob_ref, silu)

    @pl.when(jnp.logical_or(j == step(CB_GA), j == step(CB_GB)))
    def _():
        project(ob_ref, _sigmoid)

    @pl.when(j >= step(CB_I))
    def _():
        project(ob_ref, lambda a: a)


def _weight_col(j):
    col = jnp.int32(STEP_COLS[-1])
    for step in reversed(range(len(STEP_COLS) - 1)):
        col = jnp.where(j == step, STEP_COLS[step], col)
    return col


def _inproj(x, mod3, w_bf, lb_logits, cond_row):
    tm = 2048
    t = x.shape[0]
    n_b16 = len(STEP_COLS) - N_F32_COLS
    return pl.pallas_call(
        _inproj_kernel,
        grid=(t // tm, len(STEP_COLS)),
        in_specs=[pl.BlockSpec((tm, D_MODEL), lambda i, j: (i, 0)),
                  pl.BlockSpec((1, N_MOD, D_MODEL), lambda i, j: (cond_row(i * tm), 0, 0)),
                  pl.BlockSpec((D_MODEL, D_MODEL), lambda i, j: (0, _weight_col(j))),
                  pl.BlockSpec((2, 2, D_MODEL), lambda i, j: (0, 0, 0))],
        out_specs=[pl.BlockSpec((tm, D_MODEL), lambda i, j: (i, jnp.minimum(j, N_F32_COLS - 1))),
                   pl.BlockSpec((tm, D_MODEL), lambda i, j: (i, jnp.maximum(j - N_F32_COLS, 0)))],
        out_shape=[jax.ShapeDtypeStruct((t, N_F32_COLS * D_MODEL), F32),
                   jax.ShapeDtypeStruct((t, n_b16 * D_MODEL), BF16)],
        scratch_shapes=[pltpu.VMEM((tm, D_MODEL), BF16)],
        compiler_params=_params("parallel", "arbitrary"),
        name="inproj",
    )(x, mod3, w_bf, lb_logits)


def _chunk_cumsum(lf, tri):
    lf_hi = lf.astype(BF16)
    lf_lo = (lf - lf_hi.astype(F32)).astype(BF16)
    return _dot(tri, lf_hi) + _dot(tri, lf_lo)


def _hgrn_direction(q, lf, b, v, st, lv, reverse, use_state):
    k = 1.0 - jnp.exp(lf)
    qe = q * jnp.exp(b)
    k0 = k * jnp.exp(-b)

    order = [NCHUNK - 1 - i for i in range(NCHUNK)] if reverse else list(range(NCHUNK))
    chunk_of = {p: i for i, p in enumerate(order)}
    sl = lambda i: slice(i * CHUNK, (i + 1) * CHUNK)
    last_row = lambda i: (i * CHUNK) if reverse else (i * CHUNK + CHUNK - 1)
    c = [None] * NCHUNK
    for p in range(NCHUNK):
        r = last_row(chunk_of[p])
        c[p] = b[r:r + 1, :]
    cum = [jnp.zeros_like(c[0])]
    for p in range(NCHUNK):
        cum.append(cum[p] + c[p])
    total = cum[NCHUNK]

    qe_c, ke_c = {}, {}
    for p in range(NCHUNK):
        i = chunk_of[p]
        qe_c[p] = qe[sl(i), :]
        ke_c[p] = k0[sl(i), :] * jnp.exp(c[p])

    def assemble(parts):
        return jnp.concatenate([parts[order[i]] for i in range(NCHUNK)], axis=0).astype(BF16)

    cph = NCHUNK // 2
    nlev = NCHUNK.bit_length() - 1

    def half_rows(hh):
        first = (1 - hh) if reverse else hh
        return slice(first * cph * CHUNK, (first + 1) * cph * CHUNK)

    def assemble_half(parts, hh):
        ps = range(hh * cph, (hh + 1) * cph)
        return jnp.concatenate([parts[p] for p in (reversed(ps) if reverse else ps)], axis=0).astype(BF16)

    zero = jnp.zeros((CHUNK, LANE), F32)
    qe_bf, k0_bf = qe.astype(BF16), k0.astype(BF16)
    s_half = [jnp.where(lv == 0, _dot_nt(qe_bf[half_rows(hh), :], k0_bf[half_rows(hh), :]), 0.0)
              for hh in range(2)]
    for lev in range(1, nlev):
        mid = 1 << (lev - 1)
        qp, kp = {}, {}
        for p in range(NCHUNK):
            pm = ((p >> lev) << lev) + mid
            if p >= pm:
                qp[p] = qe_c[p] * jnp.exp(cum[p] - cum[pm])
                kp[p] = zero
            else:
                qp[p] = zero
                kp[p] = ke_c[p] * jnp.exp(cum[pm] - cum[p + 1])
        for hh in range(2):
            s_lev = _dot_nt(assemble_half(qp, hh), assemble_half(kp, hh))
            s_half[hh] = jnp.where(lv == lev, s_lev, s_half[hh])
    q_top = assemble_half({p: qe_c[p] * jnp.exp(cum[p] - cum[cph]) for p in range(cph, NCHUNK)}, 1)
    k_top = assemble_half({p: ke_c[p] * jnp.exp(cum[cph] - cum[p + 1]) for p in range(cph)}, 0)
    s_top = _dot_nt(q_top, k_top)

    v_bf = v.astype(BF16)
    v_half = [v_bf[half_rows(hh), :] for hh in range(2)]
    out_half = [_dot(s_half[0].astype(BF16), v_half[0]),
                _dot(jnp.concatenate([s_top, s_half[1]], axis=1).astype(BF16),
                     jnp.concatenate(v_half, axis=0))]
    out = jnp.concatenate(out_half[::-1] if reverse else out_half, axis=0)
    if use_state:
        q_start = assemble({p: qe_c[p] * jnp.exp(cum[p]) for p in range(NCHUNK)})
        out = out + _dot_nt(q_start, st.astype(BF16))
    k_end = assemble({p: ke_c[p] * jnp.exp(total - cum[p + 1]) for p in range(NCHUNK)})
    upd = _dot_tn(v_bf, k_end)
    new_st = st * jnp.exp(total) + upd if use_state else upd
    return out, new_st


def _hgrn_kernel(*refs, nb, zero_init, emit_state):
    it = iter(refs)
    qf_ref, lff_ref, vf_ref, qb_ref, lfb_ref, vb_ref = [next(it) for _ in range(6)]
    s0_ref = None if zero_init else next(it)
    lvf_ref, lvb_ref, trif_ref, trib_ref = [next(it) for _ in range(4)]
    of_ref = next(it)
    ob_ref = next(it) if nb > 1 else None
    so_ref = next(it) if emit_state else None
    st_ref = next(it)
    i = pl.program_id(2)

    use_state = not (zero_init and nb == 1)
    if use_state:
        @pl.when(i == 0)
        def _():
            for d in range(2):
                for h in range(HPS):
                    st_ref[d, h] = jnp.zeros((DV, DK), F32) if zero_init else s0_ref[0, 0, d, h].T

    lf_f = lff_ref[...]
    lf_b = lfb_ref[...]
    b_f = _chunk_cumsum(lf_f, trif_ref[...])
    b_b = _chunk_cumsum(lf_b, trib_ref[...])
    lv_f = lvf_ref[...]
    lv_b = lvb_ref[...]
    for h in range(HPS):
        hs = slice(h * LANE, (h + 1) * LANE)
        of, stf = _hgrn_direction(qf_ref[:, hs], lf_f[:, hs], b_f[:, hs], vf_ref[:, hs], st_ref[0, h],
                                  lv_f, False, use_state)
        ob, stb = _hgrn_direction(qb_ref[:, hs], lf_b[:, hs], b_b[:, hs], vb_ref[:, hs], st_ref[1, h],
                                  lv_b, True, use_state)
        if nb > 1:
            of_ref[:, hs] = of
            ob_ref[:, hs] = ob
        else:
            of_ref[:, hs] = of + ob
        if nb > 1:
            st_ref[0, h] = stf
            st_ref[1, h] = stb
        if emit_state:
            @pl.when(i == nb - 1)
            def _():
                so_ref[0, 0, 0, h] = stf.T
                so_ref[0, 0, 1, h] = stb.T


def _hgrn(pf, pb, state, nseq, nb, emit_state):
    zero_init = state is None
    t = pf.shape[0]
    tri_f, tri_b, lv_f, lv_b = _scan_consts()
    wid = HPS * LANE
    per = D_MODEL // wid
    fwd = lambda cb: pl.BlockSpec((BLK, wid), lambda b, h, i, cb=cb: (b * nb + i, cb * per + h))
    bwd = lambda cb: pl.BlockSpec((BLK, wid), lambda b, h, i, cb=cb: (b * nb + nb - 1 - i, cb * per + h))
    const = lambda n=BLK: pl.BlockSpec((n, n), lambda b, h, i: (0, 0))
    hl = BLK // 2
    st_spec = pl.BlockSpec((1, 1, 2, HPS, DK, DV), lambda b, h, i: (b, 0, 0, h, 0, 0))
    in_specs = [fwd(OF_Q), fwd(OF_FF), fwd(OB_I), bwd(OF_Q), bwd(OF_FB), bwd(OB_I)]
    args = [pf, pf, pb, pf, pf, pb]
    if not zero_init:
        in_specs.append(st_spec)
        args.append(state)
    in_specs += [const(hl), const(hl), const(), const()]
    args += [jnp.asarray(lv_f[:hl, :hl]), jnp.asarray(lv_b[:hl, :hl]),
             jnp.asarray(tri_f, BF16), jnp.asarray(tri_b, BF16)]
    out_specs = [pl.BlockSpec((BLK, wid), lambda b, h, i: (b * nb + i, h))]
    out_shape = [jax.ShapeDtypeStruct((t, D_MODEL), F32)]
    if nb > 1:
        out_specs.append(pl.BlockSpec((BLK, wid), lambda b, h, i: (b * nb + nb - 1 - i, h)))
        out_shape.append(jax.ShapeDtypeStruct((t, D_MODEL), F32))
    if emit_state:
        out_specs.append(st_spec)
        out_shape.append(jax.ShapeDtypeStruct((nseq, DEPTH, 2, H_A, DK, DV), F32))
    return pl.pallas_call(
        functools.partial(_hgrn_kernel, nb=nb, zero_init=zero_init, emit_state=emit_state),
        grid=(nseq, H_A // HPS, nb),
        in_specs=in_specs,
        out_specs=out_specs,
        out_shape=out_shape,
        scratch_shapes=[pltpu.VMEM((2, HPS, DV, DK), F32)],
        compiler_params=_params("parallel", "parallel", "arbitrary"),
        name="hgrn_scan",
    )(*args)


def _filter_kernel(z_ref, w1_ref, b1_ref, w2_ref, b2_ref, w3_ref, b3_ref, fq_ref, w4_ref, dec_ref,
                   fhi_ref, flo_ref, sgn_ref, o_ref, prev_ref):
    zp = z_ref[...]
    h = jnp.sin(fq_ref[0:1, :] * (_dot_hi(zp, w1_ref[...]) + b1_ref[...]))
    h = jnp.sin(fq_ref[1:2, :] * (_dot_hi(h, w2_ref[...]) + b2_ref[...]))
    h = jnp.sin(fq_ref[2:3, :] * (_dot_hi(h, w3_ref[...]) + b3_ref[...]))
    a = _dot_hi(h, w4_ref[...])
    window = jnp.exp(-zp[:, 0:1] * dec_ref[...]) + DECAY_SHIFT
    a = a * window * zp[:, FILT_EMB:FILT_EMB + 1]
    a_hi = a.astype(BF16)
    a_lo = (a - a_hi.astype(F32)).astype(BF16)
    f_hi = fhi_ref[...]
    ah = _dot(f_hi, a_hi) + (_dot(f_hi, a_lo) + _dot(flo_ref[...], a_hi))
    o_ref[0] = ah + sgn_ref[...] * prev_ref[...]
    prev_ref[...] = ah


def _filter_spectra(seq_len, w1, b1, w2, b2, w3, b3, freq, w4):
    nb = seq_len // BLK
    fwd_dft, _, sgn = _dft_consts()
    zpos = jnp.asarray(_filter_positions(seq_len))
    f_full = jnp.asarray(fwd_dft)
    f_hi = f_full.astype(BF16)
    f_lo = (f_full - f_hi.astype(F32)).astype(BF16)
    pad2 = lambda w: jnp.pad(w, ((0, LANE - w.shape[0]), (0, LANE - w.shape[1])))
    padv = lambda b: jnp.pad(b, (0, LANE - b.shape[0]))[None, :]
    w1p, w2p, w3p = pad2(w1), pad2(w2), pad2(w3)
    b1p, b2p, b3p = padv(b1), padv(b2), padv(b3)
    fqp = jnp.pad(freq, ((0, 0), (0, LANE - freq.shape[1])))
    w4p = jnp.pad(w4, ((0, LANE - w4.shape[0]), (0, 0)))
    small = lambda shape: pl.BlockSpec(shape, lambda m: (0, 0))
    return pl.pallas_call(
        _filter_kernel,
        grid=(2 * nb,),
        in_specs=[pl.BlockSpec((BLK, LANE), lambda m: (m, 0)),
                  small((LANE, LANE)), small((1, LANE)), small((LANE, LANE)), small((1, LANE)),
                  small((LANE, LANE)), small((1, LANE)), small((3, LANE)),
                  pl.BlockSpec((LANE, D_B), lambda m: (0, jnp.where(m < nb, 1, 0))),
                  small((1, D_B)), small((NFREQ, BLK)), small((NFREQ, BLK)), small((NFREQ, 1))],
        out_specs=pl.BlockSpec((1, NFREQ, D_B), lambda m: (jnp.maximum(m - 1, 0), 0, 0)),
        out_shape=jax.ShapeDtypeStruct((2 * nb - 1, NFREQ, D_B), F32),
        scratch_shapes=[pltpu.VMEM((NFREQ, D_B), F32)],
        compiler_params=_params("arbitrary"),
        name="hyena_filter",
    )(zpos, w1p, b1p, w2p, b2p, w3p, b3p, fqp, w4p, jnp.asarray(_decay_rates()),
      f_hi, f_lo, jnp.asarray(sgn))


def _short_conv_gate(u0, u1, uv, w_refs, b_refs, row_len):
    t = lax.broadcasted_iota(jnp.int32, (BLK, 1), 0)
    first = (t % row_len) == 0
    last = (t % row_len) == (row_len - 1)

    def conv(u, w_ref, b_ref):
        u = u.astype(F32)
        up = jnp.where(first, 0.0, pltpu.roll(u, 1, 0))
        dn = jnp.where(last, 0.0, pltpu.roll(u, BLK - 1, 0))
        return up * w_ref[0:1, :] + u * w_ref[1:2, :] + dn * w_ref[2:3, :] + b_ref[...]

    return (conv(u0, w_refs[0], b_refs[0]),
            conv(uv, w_refs[2], b_refs[2]) * conv(u1, w_refs[1], b_refs[1]))


def _hy_single_kernel(x0_ref, x1_ref, v_ref, w0_ref, w1_ref, wv_ref, b0_ref, b1_ref, bv_ref, f_ref, kh_ref,
                      skip_ref, g_ref, o_ref, *, row_len):
    x0, z = _short_conv_gate(x0_ref[...], x1_ref[...], v_ref[...], (w0_ref, w1_ref, wv_ref),
                             (b0_ref, b1_ref, bv_ref), row_len)
    zh = _dot(f_ref[...], z.astype(BF16))
    zr, zi = zh[:BLK, :], zh[BLK:, :]
    kr, ki = kh_ref[0, :BLK, :], kh_ref[0, BLK:, :]
    p = zr * kr
    q = zi * ki
    r = zr * ki + zi * kr
    row0 = lax.broadcasted_iota(jnp.int32, (BLK, 1), 0) == 0
    yh = jnp.concatenate([jnp.where(row0, p, p - q), jnp.where(row0, q, r)], axis=0).astype(BF16)
    y = _dot(g_ref[...], yh) + z * skip_ref[...]
    o_ref[...] = (x0 * y).astype(o_ref.dtype)


def _hyena_single(pb, khat, conv_w, conv_b, skip, row_len):
    t = pb.shape[0]
    fwd_dft, inv_dft, _ = _dft_consts()
    col = lambda cb: pl.BlockSpec((BLK, D_B), lambda i, cb=cb: (i, cb))
    wcol = lambda r, k: pl.BlockSpec((r, D_B), lambda i, k=k: (0, k))
    conv_b = conv_b[None, :]
    return pl.pallas_call(
        functools.partial(_hy_single_kernel, row_len=row_len),
        grid=(t // BLK,),
        in_specs=[col(OB_X0), col(OB_X1), col(OB_V),
                  wcol(3, 0), wcol(3, 1), wcol(3, 2), wcol(1, 0), wcol(1, 1), wcol(1, 2),
                  pl.BlockSpec((NFREQ, BLK), lambda i: (0, 0)),
                  pl.BlockSpec((1, NFREQ, D_B), lambda i: (0, 0, 0)),
                  pl.BlockSpec((1, D_B), lambda i: (0, 0)),
                  pl.BlockSpec((BLK, NFREQ), lambda i: (0, 0))],
        out_specs=pl.BlockSpec((BLK, D_B), lambda i: (i, 0)),
        out_shape=jax.ShapeDtypeStruct((t, D_B), BF16),
        compiler_params=_params("parallel"),
        name="hyena_single",
    )(pb, pb, pb, conv_w, conv_w, conv_w, conv_b, conv_b, conv_b, jnp.asarray(fwd_dft).astype(BF16), khat,
      skip[None, :], jnp.asarray(inv_dft).astype(BF16))


ROWG = 32


def _hy_multi_kernel(x0_ref, x1_ref, v_ref, w0_ref, w1_ref, wv_ref, b0_ref, b1_ref, bv_ref, f_ref, kh_ref,
                     skip_ref, g_ref, o_ref, zh_ref, z_ref, x0s_ref, yh_ref, *, nb, row_len):
    dt = o_ref.shape[1]

    def front(blk, carry):
        rows = pl.ds(pl.multiple_of(blk * BLK, BLK), BLK)
        x0, z = _short_conv_gate(x0_ref[rows, :], x1_ref[rows, :], v_ref[rows, :],
                                 (w0_ref, w1_ref, wv_ref), (b0_ref, b1_ref, bv_ref), row_len)
        x0s_ref[rows, :] = x0
        z_ref[rows, :] = z
        zh_ref[blk] = _dot(f_ref[...], z.astype(BF16))
        return carry

    lax.fori_loop(0, nb, front, 0)

    row_in_group = lax.broadcasted_iota(jnp.int32, (ROWG, 1), 0)

    def back(i, carry):
        def row_group(rg, carry2):
            re = pl.ds(pl.multiple_of(rg * ROWG, ROWG), ROWG)
            im = pl.ds(pl.multiple_of(BLK + rg * ROWG, ROWG), ROWG)

            def body(j, acc):
                p, q, r = acc
                kidx = i - j + (nb - 1)
                zr = zh_ref[j, re, :]
                zi = zh_ref[j, im, :]
                kr = kh_ref[kidx, re, :]
                ki = kh_ref[kidx, im, :]
                return (p + zr * kr, q + zi * ki, r + (zr * ki + zi * kr))

            zeros = jnp.zeros((ROWG, dt), F32)
            p, q, r = lax.fori_loop(0, nb, body, (zeros, zeros, zeros), unroll=True)
            row0 = (row_in_group + rg * ROWG) == 0
            yh_ref[re, :] = jnp.where(row0, p, p - q).astype(BF16)
            yh_ref[im, :] = jnp.where(row0, q, r).astype(BF16)
            return carry2

        lax.fori_loop(0, BLK // ROWG, row_group, 0)
        rows = pl.ds(pl.multiple_of(i * BLK, BLK), BLK)
        y = _dot(g_ref[...], yh_ref[...]) + z_ref[rows, :] * skip_ref[...]
        o_ref[rows, :] = (x0s_ref[rows, :] * y).astype(o_ref.dtype)
        return carry

    lax.fori_loop(0, nb, back, 0)


def _hyena_multi(pb, khat, conv_w, conv_b, skip, nseq, nb, row_len):
    t = pb.shape[0]
    seq_len = nb * BLK
    dt = LANE
    per = D_B // dt
    fwd_dft, inv_dft, _ = _dft_consts()
    seq = lambda cb: pl.BlockSpec((seq_len, dt), lambda d, b, cb=cb: (b, cb * per + d))
    wcol = lambda r, k: pl.BlockSpec((r, dt), lambda d, b, k=k: (0, k * per + d))
    conv_b = conv_b[None, :]
    return pl.pallas_call(
        functools.partial(_hy_multi_kernel, nb=nb, row_len=row_len),
        grid=(per, nseq),
        in_specs=[seq(OB_X0), seq(OB_X1), seq(OB_V),
                  wcol(3, 0), wcol(3, 1), wcol(3, 2), wcol(1, 0), wcol(1, 1), wcol(1, 2),
                  pl.BlockSpec((NFREQ, BLK), lambda d, b: (0, 0)),
                  pl.BlockSpec((2 * nb - 1, NFREQ, dt), lambda d, b: (0, 0, d)),
                  pl.BlockSpec((1, dt), lambda d, b: (0, d)),
                  pl.BlockSpec((BLK, NFREQ), lambda d, b: (0, 0))],
        out_specs=pl.BlockSpec((seq_len, dt), lambda d, b: (b, d)),
        out_shape=jax.ShapeDtypeStruct((t, D_B), BF16),
        scratch_shapes=[pltpu.VMEM((nb, NFREQ, dt), F32), pltpu.VMEM((seq_len, dt), F32),
                        pltpu.VMEM((seq_len, dt), F32), pltpu.VMEM((NFREQ, dt), BF16)],
        compiler_params=_params("parallel", "parallel"),
        name="hyena_multi",
    )(pb, pb, pb, conv_w, conv_w, conv_w, conv_b, conv_b, conv_b, jnp.asarray(fwd_dft).astype(BF16), khat,
      skip[None, :], jnp.asarray(inv_dft).astype(BF16))


def _layer_norm(y, g, b):
    mu = jnp.mean(y, axis=-1, keepdims=True)
    yc = y - mu
    var = jnp.mean(yc * yc, axis=-1, keepdims=True)
    return yc * lax.rsqrt(var + LN_EPS) * g + b


def _post_kernel(*refs, n_dir):
    (g_ref, ga_ref, gb_ref, hy_ref, x_ref, mod_ref, nw_ref, pa_ref, pb_ref, wo_ref, lg_ref, lb_ref,
     wg_ref, wu_ref, wo2_ref, lg2_ref, lb2_ref, o_ref) = refs[n_dir:]
    o = refs[0][...]
    for d_ref in refs[1:n_dir]:
        o = o + d_ref[...]
    nw = nw_ref[...]
    parts = []
    for h in range(H_A):
        oh = o[:, h * DV:(h + 1) * DV]
        ms = jnp.mean(oh * oh, axis=-1, keepdims=True)
        parts.append(oh * lax.rsqrt(ms + RMS_EPS) * nw)
    oa = jnp.concatenate(parts, axis=1) * g_ref[...]
    a = _dot(oa.astype(BF16), pa_ref[...])
    b = _dot(hy_ref[...].astype(BF16), pb_ref[...])
    merged = ga_ref[...] * a + gb_ref[...] * b
    mix = _dot(merged.astype(BF16), wo_ref[...])
    y = ALPHA * x_ref[...] + mod_ref[0, 2:3, :] * mix
    x1 = _layer_norm(y, lg_ref[...], lb_ref[...])

    h = (x1 * (1.0 + mod_ref[0, 4:5, :]) + mod_ref[0, 3:4, :]).astype(BF16)
    gt = _dot(h, wg_ref[...])
    up = _dot(h, wu_ref[...])
    act = (gt * _sigmoid(gt) * up).astype(BF16)
    ff = _dot(act, wo2_ref[...])
    y2 = ALPHA * x1 + mod_ref[0, 5:6, :] * ff
    o_ref[...] = _layer_norm(y2, lg2_ref[...], lb2_ref[...])


def _post(o_dirs, pb16, o_hy, x, mod3, cond_row, norm_w, pa, pb, wo, ln_g, ln_b,
          ffn_w_in, ffn_w_out, ln2_g, ln2_b):
    tm = 256
    t = x.shape[0]
    n_dir = len(o_dirs)
    once = lambda shape, idx: pl.BlockSpec(shape, idx, pipeline_mode=pl.Buffered(1))
    tok = pl.BlockSpec((tm, D_MODEL), lambda i: (i, 0))
    col = lambda cb: pl.BlockSpec((tm, D_MODEL), lambda i, cb=cb: (i, cb))
    mat = once((D_MODEL, D_MODEL), lambda i: (0, 0))
    vec = pl.BlockSpec((1, D_MODEL), lambda i: (0, 0))
    return pl.pallas_call(
        functools.partial(_post_kernel, n_dir=n_dir),
        grid=(t // tm,),
        in_specs=[tok] * n_dir + [col(OB_G), col(OB_GA), col(OB_GB), tok, tok,
                  pl.BlockSpec((1, N_MOD, D_MODEL), lambda i: (cond_row(i * tm), 0, 0)),
                  pl.BlockSpec((1, DV), lambda i: (0, 0)),
                  mat, mat, mat, vec, vec,
                  once((D_MODEL, D_FF), lambda i: (0, 0)),
                  once((D_MODEL, D_FF), lambda i: (0, 1)),
                  once((D_FF, D_MODEL), lambda i: (0, 0)),
                  vec, vec],
        out_specs=tok,
        out_shape=jax.ShapeDtypeStruct((t, D_MODEL), F32),
        compiler_params=_params("parallel"),
        name="merge_ffn",
    )(*o_dirs, pb16, pb16, pb16, o_hy, x, mod3, norm_w[None, :], pa, pb, wo, ln_g[None, :], ln_b[None, :],
      ffn_w_in, ffn_w_in, ffn_w_out, ln2_g[None, :], ln2_b[None, :])


def _trunk(x, mod3, cond_row, nseq, seq_len, row_len, state, emit_state, w, lb_logits):
    nb = seq_len // BLK
    pf, pb = _inproj(x, mod3, w["w_in"], lb_logits, cond_row)
    hg = _hgrn(pf, pb, state, nseq, nb, emit_state)
    n_dir = 2 if nb > 1 else 1
    khat = _filter_spectra(seq_len, w["filt_w1"], w["filt_b1"], w["filt_w2"], w["filt_b2"],
                           w["filt_w3"], w["filt_b3"], w["filt_freq"], w["filt_w4"])
    if nb == 1:
        o_hy = _hyena_single(pb, khat, w["hy_conv_w"], w["hy_conv_b"], w["hy_skip"], row_len)
    else:
        o_hy = _hyena_multi(pb, khat, w["hy_conv_w"], w["hy_conv_b"], w["hy_skip"], nseq, nb, row_len)
    x2 = _post(hg[:n_dir], pb, o_hy, x, mod3, cond_row, w["hgrn_norm_w"], w["proj_a"], w["proj_b"],
               w["w_out"], w["ln1_g"], w["ln1_b"], w["ffn_w_in"], w["ffn_w_out"], w["ln2_g"], w["ln2_b"])
    return x2, (hg[n_dir] if emit_state else None)


def kernel(x_prompt, x_sample, state_hgrn, c, c_ctx, ada_w, ada_b, w_in, hgrn_lb_logits, hgrn_norm_w,
           hy_conv_w, hy_conv_b, filt_w1, filt_b1, filt_w2, filt_b2, filt_w3, filt_b3, filt_freq, filt_w4,
           hy_skip, proj_a, proj_b, w_out, ln1_g, ln1_b, ffn_w_in, ffn_w_out, ln2_g, ln2_b):
    assert ada_w.shape[0] == DEPTH == 1
    batch, seq, _ = x_prompt.shape
    dec_batch, dec_seq, _ = x_sample.shape
    assert seq % BLK == 0 and dec_seq % BLK == 0 and BLK % GRID_W == 0 and dec_batch + 1 <= 8

    w = dict(w_in=w_in[0].astype(BF16), hy_conv_w=hy_conv_w[0], hy_conv_b=hy_conv_b[0],
             filt_w1=filt_w1[0], filt_b1=filt_b1[0], filt_w2=filt_w2[0], filt_b2=filt_b2[0],
             filt_w3=filt_w3[0], filt_b3=filt_b3[0], filt_freq=filt_freq[0], filt_w4=filt_w4[0],
             hy_skip=hy_skip[0], hgrn_norm_w=hgrn_norm_w[0], proj_a=proj_a[0].astype(BF16),
             proj_b=proj_b[0].astype(BF16), w_out=w_out[0].astype(BF16), ln1_g=ln1_g[0], ln1_b=ln1_b[0],
             ffn_w_in=ffn_w_in[0].astype(BF16), ffn_w_out=ffn_w_out[0].astype(BF16),
             ln2_g=ln2_g[0], ln2_b=ln2_b[0])

    cond8 = jnp.zeros((8, D_MODEL), F32).at[0].set(c_ctx).at[1:1 + dec_batch].set(c)
    mod3 = _modulation(cond8, ada_w[0], ada_b[0][None, :]).reshape(8, N_MOD, D_MODEL)

    xp = x_prompt.reshape(batch * seq, D_MODEL)
    xs = x_sample.reshape(dec_batch * dec_seq, D_MODEL)
    yp, new_state = _trunk(xp, mod3, lambda tok: 0, batch, seq, seq, None, True, w, hgrn_lb_logits)
    ys, _ = _trunk(xs, mod3, lambda tok: 1 + tok // dec_seq, dec_batch, dec_seq, GRID_W, state_hgrn, False,
                   w, hgrn_lb_logits)
    return (yp.reshape(batch, seq, D_MODEL), ys.reshape(dec_batch, dec_seq, D_MODEL), new_state)
```

```python
import functools
import math

import numpy as np
import jax
import jax.numpy as jnp
from jax import lax
from jax.experimental import pallas as pl
from jax.experimental.pallas import tpu as pltpu

F32 = jnp.float32
BF16 = jnp.bfloat16

D_MODEL = 1024
DEPTH = 1
GRID_W = 64
H_A = 8
DK = 128
DV = 128
D_B = 1024
FILT_EMB = 33
FILT_BANDS = 16
FILT_ORDER = 64
DECAY_FAST = 0.3
DECAY_SLOW = 1.5
DECAY_TARGET = 1e-2
DECAY_SHIFT = 0.05
D_FF = 2816
N_MOD = 6
W_IN_COLS = 10 * D_MODEL
ALPHA = (2.0 * DEPTH) ** 0.25
LN_EPS = 1e-5
RMS_EPS = 1e-6

LANE = 128
BLK = 256
NFREQ = 2 * BLK
CHUNK = 32
NCHUNK = BLK // CHUNK
HPS = 8
VMEM_LIMIT = 56 * 1024 * 1024

CB_Q, CB_FF, CB_FB, CB_I, CB_G, CB_X0, CB_X1, CB_V, CB_GA, CB_GB = range(10)
STEP_COLS = (CB_FF, CB_FB, CB_Q, CB_G, CB_GA, CB_GB, CB_I, CB_X0, CB_X1, CB_V)
N_F32_COLS = 2
OF_FF, OF_FB = range(N_F32_COLS)
OB_Q, OB_G, OB_GA, OB_GB, OB_I, OB_X0, OB_X1, OB_V = range(len(STEP_COLS) - N_F32_COLS)


def _sigmoid(x):
    return 1.0 / (1.0 + jnp.exp(-x))


def _dot(a, b):
    return jnp.dot(a, b, preferred_element_type=F32)


def _dot_nt(a, b):
    return lax.dot_general(a, b, (((1,), (1,)), ((), ())), preferred_element_type=F32)


def _dot_tn(a, b):
    return lax.dot_general(a, b, (((0,), (0,)), ((), ())), preferred_element_type=F32)


def _dot_hi(a, b):
    return jnp.dot(a, b, preferred_element_type=F32, precision=lax.Precision.HIGHEST)


def _params(*sem):
    return pltpu.CompilerParams(dimension_semantics=sem, vmem_limit_bytes=VMEM_LIMIT)


@functools.lru_cache(maxsize=None)
def _dft_consts():
    n = np.arange(BLK, dtype=np.float64)
    f = np.arange(BLK, dtype=np.float64)
    ang = 2.0 * np.pi * np.outer(f, n) / NFREQ
    fwd = np.zeros((NFREQ, BLK), np.float64)
    fwd[:BLK] = np.cos(ang)
    fwd[BLK + 1:] = -np.sin(ang[1:])
    fwd[BLK] = np.cos(np.pi * n)
    inv = np.zeros((BLK, NFREQ), np.float64)
    scale = np.full((BLK,), 2.0)
    scale[0] = 1.0
    inv[:, :BLK] = np.cos(ang.T) * scale[None, :]
    inv[:, BLK + 1:] = -2.0 * np.sin(ang.T[:, 1:])
    inv[:, BLK] = np.cos(np.pi * n)
    inv /= NFREQ
    fr = np.arange(NFREQ)
    freq_of_row = np.where(fr < BLK, fr, np.where(fr == BLK, BLK, fr - BLK))
    sgn = np.where(freq_of_row % 2 == 0, 1.0, -1.0)[:, None]
    return fwd.astype(np.float32), inv.astype(np.float32), sgn.astype(np.float32)


@functools.lru_cache(maxsize=None)
def _scan_consts():
    t = np.arange(BLK)
    ct = t // CHUNK
    same = ct[:, None] == ct[None, :]
    tri_f = (same & (t[None, :] <= t[:, None])).astype(np.float32)
    tri_b = (same & (t[None, :] >= t[:, None])).astype(np.float32)

    def levels(p, diag):
        x = p[:, None] ^ p[None, :]
        lvl = np.zeros_like(x)
        for bit in range(1, NCHUNK.bit_length()):
            lvl = np.where(x >= (1 << (bit - 1)), bit, lvl)
        lv = np.where(p[:, None] > p[None, :], lvl, -1)
        return np.where(same, np.where(diag, 0, -1), lv).astype(np.int32)

    lv_f = levels(ct, t[None, :] <= t[:, None])
    lv_b = levels(NCHUNK - 1 - ct, t[None, :] >= t[:, None])
    return tri_f, tri_b, lv_f, lv_b


@functools.lru_cache(maxsize=None)
def _filter_positions(seq_len):
    f32 = np.float32
    j = np.arange(-seq_len, seq_len)
    p = np.abs(j)
    valid = (j > -seq_len)
    pc = np.minimum(p, seq_len - 1)
    t = np.linspace(0.0, 1.0, seq_len, dtype=f32)[pc]
    wpos = (f32(2.0 * math.pi / seq_len) * np.arange(seq_len, dtype=f32))[pc]
    bands = np.linspace(1e-4, FILT_BANDS - 1, FILT_BANDS, dtype=f32)
    arg = (bands[None, :] * wpos[:, None]).astype(f32)
    z = np.zeros((2 * seq_len, LANE), f32)
    z[:, 0] = t
    z[:, 1:1 + FILT_BANDS] = np.cos(arg)
    z[:, 1 + FILT_BANDS:FILT_EMB] = -np.sin(arg)
    z[:, FILT_EMB] = valid.astype(f32)
    return z


@functools.lru_cache(maxsize=None)
def _decay_rates():
    max_decay = math.log(DECAY_TARGET) / DECAY_FAST
    min_decay = math.log(DECAY_TARGET) / DECAY_SLOW
    return np.abs(np.linspace(min_decay, max_decay, D_B, dtype=np.float32))[None, :]


def _mod_kernel(c_ref, w_ref, b_ref, o_ref):
    c = c_ref[...]
    s = (c * _sigmoid(c)).astype(BF16)
    o_ref[...] = _dot(s, w_ref[...].astype(BF16)) + b_ref[...]


def _modulation(cond8, ada_w, ada_b):
    tn = 1536
    n = N_MOD * D_MODEL
    return pl.pallas_call(
        _mod_kernel,
        grid=(n // tn,),
        in_specs=[pl.BlockSpec((8, D_MODEL), lambda j: (0, 0)),
                  pl.BlockSpec((D_MODEL, tn), lambda j: (0, j)),
                  pl.BlockSpec((1, tn), lambda j: (0, j))],
        out_specs=pl.BlockSpec((8, tn), lambda j: (0, j)),
        out_shape=jax.ShapeDtypeStruct((8, n), F32),
        compiler_params=_params("parallel"),
        name="modulation",
    )(cond8, ada_w, ada_b)


def _lower_bounds(lbl_ref):
    l0 = lbl_ref[0]
    l1 = lbl_ref[1]
    m = jnp.maximum(l0, l1)
    e0 = jnp.exp(l0 - m)
    e1 = jnp.exp(l1 - m)
    return e0 / (e0 + e1)


def _inproj_kernel(x_ref, mod_ref, w_ref, lbl_ref, of_ref, ob_ref, h_ref):
    j = pl.program_id(1)

    @pl.when(j == 0)
    def _():
        h = x_ref[...] * (1.0 + mod_ref[0, 1:2, :]) + mod_ref[0, 0:1, :]
        h_ref[...] = h.astype(BF16)

    def project(o_ref, act):
        for r in range(x_ref.shape[0] // BLK):
            rows = pl.ds(r * BLK, BLK)
            o_ref[rows, :] = act(_dot(h_ref[rows, :], w_ref[...])).astype(o_ref.dtype)

    silu = lambda a: a * _sigmoid(a)
    step = STEP_COLS.index

    @pl.when(jnp.logical_or(j == step(CB_FF), j == step(CB_FB)))
    def _():
        lb2 = _lower_bounds(lbl_ref)
        lb = jnp.where(j == step(CB_FF), lb2[0:1, :], lb2[1:2, :])
        project(of_ref, lambda a: jnp.log(lb + (1.0 - lb) * _sigmoid(a)))

    @pl.when(jnp.logical_or(j == step(CB_Q), j == step(CB_G)))
    def _():
        project(ob_ref, silu)

    @pl.when(jnp.logical_or(j == step(CB_GA), j == step(CB_GB)))
    def _():
        project(ob_ref, _sigmoid)

    @pl.when(j >= step(CB_I))
    def _():
        project(ob_ref, lambda a: a)


def _weight_col(j):
    col = jnp.int32(STEP_COLS[-1])
    for step in reversed(range(len(STEP_COLS) - 1)):
        col = jnp.where(j == step, STEP_COLS[step], col)
    return col


def _inproj(x, mod3, w_bf, lb_logits, cond_row):
    tm = 2048
    t = x.shape[0]
    n_b16 = len(STEP_COLS) - N_F32_COLS
    return pl.pallas_call(
        _inproj_kernel,
        grid=(t // tm, len(STEP_COLS)),
        in_specs=[pl.BlockSpec((tm, D_MODEL), lambda i, j: (i, 0)),
                  pl.BlockSpec((1, N_MOD, D_MODEL), lambda i, j: (cond_row(i * tm), 0, 0)),
                  pl.BlockSpec((D_MODEL, D_MODEL), lambda i, j: (0, _weight_col(j))),
                  pl.BlockSpec((2, 2, D_MODEL), lambda i, j: (0, 0, 0))],
        out_specs=[pl.BlockSpec((tm, D_MODEL), lambda i, j: (i, jnp.minimum(j, N_F32_COLS - 1))),
                   pl.BlockSpec((tm, D_MODEL), lambda i, j: (i, jnp.maximum(j - N_F32_COLS, 0)))],
        out_shape=[jax.ShapeDtypeStruct((t, N_F32_COLS * D_MODEL), F32),
                   jax.ShapeDtypeStruct((t, n_b16 * D_MODEL), BF16)],
        scratch_shapes=[pltpu.VMEM((tm, D_MODEL), BF16)],
        compiler_params=_params("parallel", "arbitrary"),
        name="inproj",
    )(x, mod3, w_bf, lb_logits)


def _chunk_cumsum(lf, tri):
    lf_hi = lf.astype(BF16)
    lf_lo = (lf - lf_hi.astype(F32)).astype(BF16)
    return _dot(tri, lf_hi) + _dot(tri, lf_lo)


def _hgrn_direction(q, lf, b, v, st, lv, reverse, use_state):
    k = 1.0 - jnp.exp(lf)
    qe = q.astype(F32) * jnp.exp(b)
    k0 = k * jnp.exp(-b)

    order = [NCHUNK - 1 - i for i in range(NCHUNK)] if reverse else list(range(NCHUNK))
    chunk_of = {p: i for i, p in enumerate(order)}
    sl = lambda i: slice(i * CHUNK, (i + 1) * CHUNK)
    last_row = lambda i: (i * CHUNK) if reverse else (i * CHUNK + CHUNK - 1)
    c = [None] * NCHUNK
    for p in range(NCHUNK):
        r = last_row(chunk_of[p])
        c[p] = b[r:r + 1, :]
    cum = [jnp.zeros_like(c[0])]
    for p in range(NCHUNK):
        cum.append(cum[p] + c[p])
    total = cum[NCHUNK]

    qe_c, ke_c = {}, {}
    for p in range(NCHUNK):
        i = chunk_of[p]
        qe_c[p] = qe[sl(i), :]
        ke_c[p] = k0[sl(i), :] * jnp.exp(c[p])

    def assemble(parts):
        return jnp.concatenate([parts[order[i]] for i in range(NCHUNK)], axis=0).astype(BF16)

    cph = NCHUNK // 2
    nlev = NCHUNK.bit_length() - 1

    def half_rows(hh):
        first = (1 - hh) if reverse else hh
        return slice(first * cph * CHUNK, (first + 1) * cph * CHUNK)

    def assemble_half(parts, hh):
        ps = range(hh * cph, (hh + 1) * cph)
        return jnp.concatenate([parts[p] for p in (reversed(ps) if reverse else ps)], axis=0).astype(BF16)

    zero = jnp.zeros((CHUNK, LANE), F32)
    qe_bf, k0_bf = qe.astype(BF16), k0.astype(BF16)
    s_half = [jnp.where(lv == 0, _dot_nt(qe_bf[half_rows(hh), :], k0_bf[half_rows(hh), :]), 0.0)
              for hh in range(2)]
    for lev in range(1, nlev):
        mid = 1 << (lev - 1)
        qp, kp = {}, {}
        for p in range(NCHUNK):
            pm = ((p >> lev) << lev) + mid
            if p >= pm:
                qp[p] = qe_c[p] * jnp.exp(cum[p] - cum[pm])
                kp[p] = zero
            else:
                qp[p] = zero
                kp[p] = ke_c[p] * jnp.exp(cum[pm] - cum[p + 1])
        for hh in range(2):
            s_lev = _dot_nt(assemble_half(qp, hh), assemble_half(kp, hh))
            s_half[hh] = jnp.where(lv == lev, s_lev, s_half[hh])
    q_top = assemble_half({p: qe_c[p] * jnp.exp(cum[p] - cum[cph]) for p in range(cph, NCHUNK)}, 1)
    k_top = assemble_half({p: ke_c[p] * jnp.exp(cum[cph] - cum[p + 1]) for p in range(cph)}, 0)
    s_top = _dot_nt(q_top, k_top)

    v_bf = v.astype(BF16)
    v_half = [v_bf[half_rows(hh), :] for hh in range(2)]
    out_half = [_dot(s_half[0].astype(BF16), v_half[0]),
                _dot(jnp.concatenate([s_top, s_half[1]], axis=1).astype(BF16),
                     jnp.concatenate(v_half, axis=0))]
    out = jnp.concatenate(out_half[::-1] if reverse else out_half, axis=0)
    if use_state:
        q_start = assemble({p: qe_c[p] * jnp.exp(cum[p]) for p in range(NCHUNK)})
        out = out + _dot_nt(q_start, st.astype(BF16))
    k_end = assemble({p: ke_c[p] * jnp.exp(total - cum[p + 1]) for p in range(NCHUNK)})
    upd = _dot_tn(v_bf, k_end)
    new_st = st * jnp.exp(total) + upd if use_state else upd
    return out, new_st


def _hgrn_kernel(*refs, nb, zero_init, emit_state):
    it = iter(refs)
    qf_ref, lff_ref, vf_ref, qb_ref, lfb_ref, vb_ref = [next(it) for _ in range(6)]
    s0_ref = None if zero_init else next(it)
    lvf_ref, lvb_ref, trif_ref, trib_ref = [next(it) for _ in range(4)]
    of_ref = next(it)
    ob_ref = next(it) if nb > 1 else None
    so_ref = next(it) if emit_state else None
    st_ref = next(it)
    i = pl.program_id(2)

    use_state = not (zero_init and nb == 1)
    if use_state:
        @pl.when(i == 0)
        def _():
            for d in range(2):
                for h in range(HPS):
                    st_ref[d, h] = jnp.zeros((DV, DK), F32) if zero_init else s0_ref[0, 0, d, h].T

    lf_f = lff_ref[...]
    lf_b = lfb_ref[...]
    b_f = _chunk_cumsum(lf_f, trif_ref[...])
    b_b = _chunk_cumsum(lf_b, trib_ref[...])
    lv_f = lvf_ref[...]
    lv_b = lvb_ref[...]
    for h in range(HPS):
        hs = slice(h * LANE, (h + 1) * LANE)
        of, stf = _hgrn_direction(qf_ref[:, hs], lf_f[:, hs], b_f[:, hs], vf_ref[:, hs], st_ref[0, h],
                                  lv_f, False, use_state)
        ob, stb = _hgrn_direction(qb_ref[:, hs], lf_b[:, hs], b_b[:, hs], vb_ref[:, hs], st_ref[1, h],
                                  lv_b, True, use_state)
        if nb > 1:
            of_ref[:, hs] = of
            ob_ref[:, hs] = ob
        else:
            of_ref[:, hs] = of + ob
        if nb > 1:
            st_ref[0, h] = stf
            st_ref[1, h] = stb
        if emit_state:
            @pl.when(i == nb - 1)
            def _():
                so_ref[0, 0, 0, h] = stf.T
                so_ref[0, 0, 1, h] = stb.T


def _hgrn(pf, pb, state, nseq, nb, emit_state):
    zero_init = state is None
    t = pf.shape[0]
    tri_f, tri_b, lv_f, lv_b = _scan_consts()
    wid = HPS * LANE
    per = D_MODEL // wid
    fwd = lambda cb: pl.BlockSpec((BLK, wid), lambda b, h, i, cb=cb: (b * nb + i, cb * per + h))
    bwd = lambda cb: pl.BlockSpec((BLK, wid), lambda b, h, i, cb=cb: (b * nb + nb - 1 - i, cb * per + h))
    const = lambda n=BLK: pl.BlockSpec((n, n), lambda b, h, i: (0, 0))
    hl = BLK // 2
    st_spec = pl.BlockSpec((1, 1, 2, HPS, DK, DV), lambda b, h, i: (b, 0, 0, h, 0, 0))
    in_specs = [fwd(OB_Q), fwd(OF_FF), fwd(OB_I), bwd(OB_Q), bwd(OF_FB), bwd(OB_I)]
    args = [pb, pf, pb, pb, pf, pb]
    if not zero_init:
        in_specs.append(st_spec)
        args.append(state)
    in_specs += [const(hl), const(hl), const(), const()]
    args += [jnp.asarray(lv_f[:hl, :hl]), jnp.asarray(lv_b[:hl, :hl]),
             jnp.asarray(tri_f, BF16), jnp.asarray(tri_b, BF16)]
    out_specs = [pl.BlockSpec((BLK, wid), lambda b, h, i: (b * nb + i, h))]
    out_shape = [jax.ShapeDtypeStruct((t, D_MODEL), F32)]
    if nb > 1:
        out_specs.append(pl.BlockSpec((BLK, wid), lambda b, h, i: (b * nb + nb - 1 - i, h)))
        out_shape.append(jax.ShapeDtypeStruct((t, D_MODEL), F32))
    if emit_state:
        out_specs.append(st_spec)
        out_shape.append(jax.ShapeDtypeStruct((nseq, DEPTH, 2, H_A, DK, DV), F32))
    return pl.pallas_call(
        functools.partial(_hgrn_kernel, nb=nb, zero_init=zero_init, emit_state=emit_state),
        grid=(nseq, H_A // HPS, nb),
        in_specs=in_specs,
        out_specs=out_specs,
        out_shape=out_shape,
        scratch_shapes=[pltpu.VMEM((2, HPS, DV, DK), F32)],
        compiler_params=_params("parallel", "parallel", "arbitrary"),
        name="hgrn_scan",
    )(*args)


def _split_bf16(x):
    hi = x.astype(BF16)
    return hi, (x - hi.astype(F32)).astype(BF16)


def _dot3(a, b):
    a_hi, a_lo = _split_bf16(a)
    b_hi, b_lo = _split_bf16(b)
    return _dot(a_hi, b_hi) + (_dot(a_hi, b_lo) + _dot(a_lo, b_hi))


def _filter_kernel(z_ref, zt_ref, w1_ref, b1_ref, w2_ref, b2_ref, w3_ref, b3_ref, fq_ref, w4_ref, dec_ref,
                   fhi_ref, flo_ref, sgn_ref, o_ref, prev_ref):
    zp = z_ref[...]
    pad = jnp.zeros((LANE - FILT_ORDER, BLK), F32)

    def layer(x, w_ref, b_ref, k):
        h = jnp.sin(fq_ref[:, k:k + 1] * (_dot_hi(w_ref[...], x) + b_ref[...]))
        return jnp.concatenate([h, pad], axis=0)

    h = layer(zt_ref[...], w1_ref, b1_ref, 0)
    h = layer(h, w2_ref, b2_ref, 1)
    h = layer(h, w3_ref, b3_ref, 2)
    a = _dot3(h.T, w4_ref[...])
    window = jnp.exp(-zp[:, 0:1] * dec_ref[...]) + DECAY_SHIFT
    a = a * window * zp[:, FILT_EMB:FILT_EMB + 1]
    a_hi, a_lo = _split_bf16(a)
    f_hi = fhi_ref[...]
    ah = _dot(f_hi, a_hi) + (_dot(f_hi, a_lo) + _dot(flo_ref[...], a_hi))
    o_ref[0] = ah + sgn_ref[...] * prev_ref[...]
    prev_ref[...] = ah


def _filter_spectra(seq_len, w1, b1, w2, b2, w3, b3, freq, w4):
    nb = seq_len // BLK
    fwd_dft, _, sgn = _dft_consts()
    zpos_np = _filter_positions(seq_len)
    zpos = jnp.asarray(zpos_np)
    zpos_t = jnp.asarray(np.ascontiguousarray(zpos_np.T))
    f_hi, f_lo = _split_bf16(jnp.asarray(fwd_dft))
    wt = lambda w: jnp.pad(w, ((0, LANE - w.shape[0]), (0, 0))).T
    colv = lambda b: b[:, None]
    w1p, w2p, w3p = wt(w1), wt(w2), wt(w3)
    b1p, b2p, b3p = colv(b1), colv(b2), colv(b3)
    fqp = freq.T
    w4p = jnp.pad(w4, ((0, LANE - w4.shape[0]), (0, 0)))
    small = lambda shape: pl.BlockSpec(shape, lambda m: (0, 0))
    return pl.pallas_call(
        _filter_kernel,
        grid=(2 * nb,),
        in_specs=[pl.BlockSpec((BLK, LANE), lambda m: (m, 0)),
                  pl.BlockSpec((LANE, BLK), lambda m: (0, m)),
                  small((FILT_ORDER, LANE)), small((FILT_ORDER, 1)), small((FILT_ORDER, LANE)),
                  small((FILT_ORDER, 1)), small((FILT_ORDER, LANE)), small((FILT_ORDER, 1)),
                  small((FILT_ORDER, 3)),
                  pl.BlockSpec((LANE, D_B), lambda m: (0, jnp.where(m < nb, 1, 0))),
                  small((1, D_B)), small((NFREQ, BLK)), small((NFREQ, BLK)), small((NFREQ, 1))],
        out_specs=pl.BlockSpec((1, NFREQ, D_B), lambda m: (jnp.maximum(m - 1, 0), 0, 0)),
        out_shape=jax.ShapeDtypeStruct((2 * nb - 1, NFREQ, D_B), F32),
        scratch_shapes=[pltpu.VMEM((NFREQ, D_B), F32)],
        compiler_params=_params("arbitrary"),
        name="hyena_filter",
    )(zpos, zpos_t, w1p, b1p, w2p, b2p, w3p, b3p, fqp, w4p, jnp.asarray(_decay_rates()),
      f_hi, f_lo, jnp.asarray(sgn))


def _short_conv_gate(u0, u1, uv, w_refs, b_refs, row_len):
    t = lax.broadcasted_iota(jnp.int32, (BLK, 1), 0)
    first = (t % row_len) == 0
    last = (t % row_len) == (row_len - 1)

    def conv(u, w_ref, b_ref):
        u = u.astype(F32)
        up = jnp.where(first, 0.0, pltpu.roll(u, 1, 0))
        dn = jnp.where(last, 0.0, pltpu.roll(u, BLK - 1, 0))
        return up * w_ref[0:1, :] + u * w_ref[1:2, :] + dn * w_ref[2:3, :] + b_ref[...]

    return (conv(u0, w_refs[0], b_refs[0]),
            conv(uv, w_refs[2], b_refs[2]) * conv(u1, w_refs[1], b_refs[1]))


def _hy_single_kernel(x0_ref, x1_ref, v_ref, w0_ref, w1_ref, wv_ref, b0_ref, b1_ref, bv_ref, f_ref, kh_ref,
                      skip_ref, g_ref, o_ref, *, row_len):
    x0, z = _short_conv_gate(x0_ref[...], x1_ref[...], v_ref[...], (w0_ref, w1_ref, wv_ref),
                             (b0_ref, b1_ref, bv_ref), row_len)
    zh = _dot(f_ref[...], z.astype(BF16))
    zr, zi = zh[:BLK, :], zh[BLK:, :]
    kr, ki = kh_ref[0, :BLK, :], kh_ref[0, BLK:, :]
    p = zr * kr
    q = zi * ki
    r = zr * ki + zi * kr
    row0 = lax.broadcasted_iota(jnp.int32, (BLK, 1), 0) == 0
    yh = jnp.concatenate([jnp.where(row0, p, p - q), jnp.where(row0, q, r)], axis=0).astype(BF16)
    y = _dot(g_ref[...], yh) + z * skip_ref[...]
    o_ref[...] = (x0 * y).astype(o_ref.dtype)


def _hyena_single(pb, khat, conv_w, conv_b, skip, row_len):
    t = pb.shape[0]
    fwd_dft, inv_dft, _ = _dft_consts()
    col = lambda cb: pl.BlockSpec((BLK, D_B), lambda i, cb=cb: (i, cb))
    wcol = lambda r, k: pl.BlockSpec((r, D_B), lambda i, k=k: (0, k))
    conv_b = conv_b[None, :]
    return pl.pallas_call(
        functools.partial(_hy_single_kernel, row_len=row_len),
        grid=(t // BLK,),
        in_specs=[col(OB_X0), col(OB_X1), col(OB_V),
                  wcol(3, 0), wcol(3, 1), wcol(3, 2), wcol(1, 0), wcol(1, 1), wcol(1, 2),
                  pl.BlockSpec((NFREQ, BLK), lambda i: (0, 0)),
                  pl.BlockSpec((1, NFREQ, D_B), lambda i: (0, 0, 0)),
                  pl.BlockSpec((1, D_B), lambda i: (0, 0)),
                  pl.BlockSpec((BLK, NFREQ), lambda i: (0, 0))],
        out_specs=pl.BlockSpec((BLK, D_B), lambda i: (i, 0)),
        out_shape=jax.ShapeDtypeStruct((t, D_B), BF16),
        compiler_params=_params("parallel"),
        name="hyena_single",
    )(pb, pb, pb, conv_w, conv_w, conv_w, conv_b, conv_b, conv_b, jnp.asarray(fwd_dft).astype(BF16), khat,
      skip[None, :], jnp.asarray(inv_dft).astype(BF16))


HY_DT = 2 * LANE
ROWG = 16


def _hy_multi_kernel(x0_ref, x1_ref, v_ref, w0_ref, w1_ref, wv_ref, b0_ref, b1_ref, bv_ref, f_ref, kh_ref,
                     skip_ref, g_ref, o_ref, zh_ref, z_ref, x0s_ref, yh_ref, *, nb, row_len):
    dt = o_ref.shape[1]

    def front(blk, carry):
        rows = pl.ds(pl.multiple_of(blk * BLK, BLK), BLK)
        x0, z = _short_conv_gate(x0_ref[rows, :], x1_ref[rows, :], v_ref[rows, :],
                                 (w0_ref, w1_ref, wv_ref), (b0_ref, b1_ref, bv_ref), row_len)
        x0s_ref[rows, :] = x0
        z_ref[rows, :] = z
        zh_ref[blk] = _dot(f_ref[...], z.astype(BF16))
        return carry

    lax.fori_loop(0, nb, front, 0)

    row_in_group = lax.broadcasted_iota(jnp.int32, (ROWG, 1), 0)

    def back(i, carry):
        def row_group(rg, carry2):
            re = pl.ds(pl.multiple_of(rg * ROWG, ROWG), ROWG)
            im = pl.ds(pl.multiple_of(BLK + rg * ROWG, ROWG), ROWG)

            def body(j, acc):
                p, q, r = acc
                kidx = i - j + (nb - 1)
                zr = zh_ref[j, re, :]
                zi = zh_ref[j, im, :]
                kr = kh_ref[kidx, re, :]
                ki = kh_ref[kidx, im, :]
                return (p + zr * kr, q + zi * ki, r + (zr * ki + zi * kr))

            zeros = jnp.zeros((ROWG, dt), F32)
            p, q, r = lax.fori_loop(0, nb, body, (zeros, zeros, zeros), unroll=True)
            row0 = (row_in_group + rg * ROWG) == 0
            yh_ref[re, :] = jnp.where(row0, p, p - q).astype(BF16)
            yh_ref[im, :] = jnp.where(row0, q, r).astype(BF16)
            return carry2

        lax.fori_loop(0, BLK // ROWG, row_group, 0)
        rows = pl.ds(pl.multiple_of(i * BLK, BLK), BLK)
        y = _dot(g_ref[...], yh_ref[...]) + z_ref[rows, :] * skip_ref[...]
        o_ref[rows, :] = (x0s_ref[rows, :] * y).astype(o_ref.dtype)
        return carry

    lax.fori_loop(0, nb, back, 0)


def _hyena_multi(pb, khat, conv_w, conv_b, skip, nseq, nb, row_len):
    t = pb.shape[0]
    seq_len = nb * BLK
    dt = HY_DT
    per = D_B // dt
    fwd_dft, inv_dft, _ = _dft_consts()
    seq = lambda cb: pl.BlockSpec((seq_len, dt), lambda d, b, cb=cb: (b, cb * per + d))
    wcol = lambda r, k: pl.BlockSpec((r, dt), lambda d, b, k=k: (0, k * per + d))
    conv_b = conv_b[None, :]
    return pl.pallas_call(
        functools.partial(_hy_multi_kernel, nb=nb, row_len=row_len),
        grid=(per, nseq),
        in_specs=[seq(OB_X0), seq(OB_X1), seq(OB_V),
                  wcol(3, 0), wcol(3, 1), wcol(3, 2), wcol(1, 0), wcol(1, 1), wcol(1, 2),
                  pl.BlockSpec((NFREQ, BLK), lambda d, b: (0, 0)),
                  pl.BlockSpec((2 * nb - 1, NFREQ, dt), lambda d, b: (0, 0, d), pipeline_mode=pl.Buffered(1)),
                  pl.BlockSpec((1, dt), lambda d, b: (0, d)),
                  pl.BlockSpec((BLK, NFREQ), lambda d, b: (0, 0))],
        out_specs=pl.BlockSpec((seq_len, dt), lambda d, b: (b, d)),
        out_shape=jax.ShapeDtypeStruct((t, D_B), BF16),
        scratch_shapes=[pltpu.VMEM((nb, NFREQ, dt), F32), pltpu.VMEM((seq_len, dt), F32),
                        pltpu.VMEM((seq_len, dt), F32), pltpu.VMEM((NFREQ, dt), BF16)],
        compiler_params=_params("parallel", "parallel"),
        name="hyena_multi",
    )(pb, pb, pb, conv_w, conv_w, conv_w, conv_b, conv_b, conv_b, jnp.asarray(fwd_dft).astype(BF16), khat,
      skip[None, :], jnp.asarray(inv_dft).astype(BF16))


def _layer_norm(y, g, b):
    mu = jnp.mean(y, axis=-1, keepdims=True)
    yc = y - mu
    var = jnp.mean(yc * yc, axis=-1, keepdims=True)
    return yc * lax.rsqrt(var + LN_EPS) * g + b


def _post_kernel(*refs, n_dir):
    (g_ref, ga_ref, gb_ref, hy_ref, x_ref, mod_ref, nw_ref, pa_ref, pb_ref, wo_ref, lg_ref, lb_ref,
     wg_ref, wu_ref, wo2_ref, lg2_ref, lb2_ref, o_ref) = refs[n_dir:]
    o = refs[0][...]
    for d_ref in refs[1:n_dir]:
        o = o + d_ref[...]
    nw = nw_ref[...]
    parts = []
    for h in range(H_A):
        oh = o[:, h * DV:(h + 1) * DV]
        ms = jnp.mean(oh * oh, axis=-1, keepdims=True)
        parts.append(oh * lax.rsqrt(ms + RMS_EPS) * nw)
    oa = jnp.concatenate(parts, axis=1) * g_ref[...]
    a = _dot(oa.astype(BF16), pa_ref[...])
    b = _dot(hy_ref[...].astype(BF16), pb_ref[...])
    merged = ga_ref[...] * a + gb_ref[...] * b
    mix = _dot(merged.astype(BF16), wo_ref[...])
    y = ALPHA * x_ref[...] + mod_ref[0, 2:3, :] * mix
    x1 = _layer_norm(y, lg_ref[...], lb_ref[...])

    h = (x1 * (1.0 + mod_ref[0, 4:5, :]) + mod_ref[0, 3:4, :]).astype(BF16)
    gt = _dot(h, wg_ref[...])
    up = _dot(h, wu_ref[...])
    act = (gt * _sigmoid(gt) * up).astype(BF16)
    ff = _dot(act, wo2_ref[...])
    y2 = ALPHA * x1 + mod_ref[0, 5:6, :] * ff
    o_ref[...] = _layer_norm(y2, lg2_ref[...], lb2_ref[...])


def _post(o_dirs, pb16, o_hy, x, mod3, cond_row, norm_w, pa, pb, wo, ln_g, ln_b,
          ffn_w_in, ffn_w_out, ln2_g, ln2_b):
    tm = 256
    t = x.shape[0]
    n_dir = len(o_dirs)
    once = lambda shape, idx: pl.BlockSpec(shape, idx, pipeline_mode=pl.Buffered(1))
    tok = pl.BlockSpec((tm, D_MODEL), lambda i: (i, 0))
    col = lambda cb: pl.BlockSpec((tm, D_MODEL), lambda i, cb=cb: (i, cb))
    mat = once((D_MODEL, D_MODEL), lambda i: (0, 0))
    vec = pl.BlockSpec((1, D_MODEL), lambda i: (0, 0))
    return pl.pallas_call(
        functools.partial(_post_kernel, n_dir=n_dir),
        grid=(t // tm,),
        in_specs=[tok] * n_dir + [col(OB_G), col(OB_GA), col(OB_GB), tok, tok,
                  pl.BlockSpec((1, N_MOD, D_MODEL), lambda i: (cond_row(i * tm), 0, 0)),
                  pl.BlockSpec((1, DV), lambda i: (0, 0)),
                  mat, mat, mat, vec, vec,
                  once((D_MODEL, D_FF), lambda i: (0, 0)),
                  once((D_MODEL, D_FF), lambda i: (0, 1)),
                  once((D_FF, D_MODEL), lambda i: (0, 0)),
                  vec, vec],
        out_specs=tok,
        out_shape=jax.ShapeDtypeStruct((t, D_MODEL), F32),
        compiler_params=_params("parallel"),
        name="merge_ffn",
    )(*o_dirs, pb16, pb16, pb16, o_hy, x, mod3, norm_w[None, :], pa, pb, wo, ln_g[None, :], ln_b[None, :],
      ffn_w_in, ffn_w_in, ffn_w_out, ln2_g[None, :], ln2_b[None, :])


def _trunk(x, mod3, cond_row, nseq, seq_len, row_len, state, emit_state, w, lb_logits):
    nb = seq_len // BLK
    pf, pb = _inproj(x, mod3, w["w_in"], lb_logits, cond_row)
    hg = _hgrn(pf, pb, state, nseq, nb, emit_state)
    n_dir = 2 if nb > 1 else 1
    khat = _filter_spectra(seq_len, w["filt_w1"], w["filt_b1"], w["filt_w2"], w["filt_b2"],
                           w["filt_w3"], w["filt_b3"], w["filt_freq"], w["filt_w4"])
    if nb == 1:
        o_hy = _hyena_single(pb, khat, w["hy_conv_w"], w["hy_conv_b"], w["hy_skip"], row_len)
    else:
        o_hy = _hyena_multi(pb, khat, w["hy_conv_w"], w["hy_conv_b"], w["hy_skip"], nseq, nb, row_len)
    x2 = _post(hg[:n_dir], pb, o_hy, x, mod3, cond_row, w["hgrn_norm_w"], w["proj_a"], w["proj_b"],
               w["w_out"], w["ln1_g"], w["ln1_b"], w["ffn_w_in"], w["ffn_w_out"], w["ln2_g"], w["ln2_b"])
    return x2, (hg[n_dir] if emit_state else None)


def kernel(x_prompt, x_sample, state_hgrn, c, c_ctx, ada_w, ada_b, w_in, hgrn_lb_logits, hgrn_norm_w,
           hy_conv_w, hy_conv_b, filt_w1, filt_b1, filt_w2, filt_b2, filt_w3, filt_b3, filt_freq, filt_w4,
           hy_skip, proj_a, proj_b, w_out, ln1_g, ln1_b, ffn_w_in, ffn_w_out, ln2_g, ln2_b):
    assert ada_w.shape[0] == DEPTH == 1
    batch, seq, _ = x_prompt.shape
    dec_batch, dec_seq, _ = x_sample.shape
    assert seq % BLK == 0 and dec_seq % BLK == 0 and BLK % GRID_W == 0 and dec_batch + 1 <= 8

    w = dict(w_in=w_in[0].astype(BF16), hy_conv_w=hy_conv_w[0], hy_conv_b=hy_conv_b[0],
             filt_w1=filt_w1[0], filt_b1=filt_b1[0], filt_w2=filt_w2[0], filt_b2=filt_b2[0],
             filt_w3=filt_w3[0], filt_b3=filt_b3[0], filt_freq=filt_freq[0], filt_w4=filt_w4[0],
             hy_skip=hy_skip[0], hgrn_norm_w=hgrn_norm_w[0], proj_a=proj_a[0].astype(BF16),
             proj_b=proj_b[0].astype(BF16), w_out=w_out[0].astype(BF16), ln1_g=ln1_g[0], ln1_b=ln1_b[0],
             ffn_w_in=ffn_w_in[0].astype(BF16), ffn_w_out=ffn_w_out[0].astype(BF16),
             ln2_g=ln2_g[0], ln2_b=ln2_b[0])

    cond8 = jnp.zeros((8, D_MODEL), F32).at[0].set(c_ctx).at[1:1 + dec_batch].set(c)
    mod3 = _modulation(cond8, ada_w[0], ada_b[0][None, :]).reshape(8, N_MOD, D_MODEL)

    xp = x_prompt.reshape(batch * seq, D_MODEL)
    xs = x_sample.reshape(dec_batch * dec_seq, D_MODEL)
    yp, new_state = _trunk(xp, mod3, lambda tok: 0, batch, seq, seq, None, True, w, hgrn_lb_logits)
    ys, _ = _trunk(xs, mod3, lambda tok: 1 + tok // dec_seq, dec_batch, dec_seq, GRID_W, state_hgrn, False,
                   w, hgrn_lb_logits)
    return (yp.reshape(batch, seq, D_MODEL), ys.reshape(dec_batch, dec_seq, D_MODEL), new_state)
```

```python
import functools
import math

import numpy as np
import jax
import jax.numpy as jnp
from jax import lax
from jax.experimental import pallas as pl
from jax.experimental.pallas import tpu as pltpu

F32 = jnp.float32
BF16 = jnp.bfloat16

D_MODEL = 1024
DEPTH = 1
GRID_W = 64
H_A = 8
DK = 128
DV = 128
D_B = 1024
FILT_EMB = 33
FILT_BANDS = 16
FILT_ORDER = 64
DECAY_FAST = 0.3
DECAY_SLOW = 1.5
DECAY_TARGET = 1e-2
DECAY_SHIFT = 0.05
D_FF = 2816
N_MOD = 6
W_IN_COLS = 10 * D_MODEL
ALPHA = (2.0 * DEPTH) ** 0.25
LN_EPS = 1e-5
RMS_EPS = 1e-6

LANE = 128
SUBLANES = 8
BLK = 256
NFREQ = 2 * BLK
CHUNK = 32
NCHUNK = BLK // CHUNK
HPS = 8
PROJ_ROWS = 128
VMEM_LIMIT = 56 * 1024 * 1024

CB_Q, CB_FF, CB_FB, CB_I, CB_G, CB_X0, CB_X1, CB_V, CB_GA, CB_GB = range(10)
STEP_COLS = (CB_FF, CB_FB, CB_Q, CB_G, CB_GA, CB_GB, CB_I, CB_X0, CB_X1, CB_V)
N_F32_COLS = 2
OF_FF, OF_FB = range(N_F32_COLS)
OB_Q, OB_G, OB_GA, OB_GB, OB_I, OB_X0, OB_X1, OB_V = range(len(STEP_COLS) - N_F32_COLS)


def _sigmoid(x):
    return 1.0 / (1.0 + jnp.exp(-x))


def _dot(a, b):
    return jnp.dot(a, b, preferred_element_type=F32)


def _dot_nt(a, b):
    return lax.dot_general(a, b, (((1,), (1,)), ((), ())), preferred_element_type=F32)


def _dot_tn(a, b):
    return lax.dot_general(a, b, (((0,), (0,)), ((), ())), preferred_element_type=F32)


def _dot_hi(a, b):
    return jnp.dot(a, b, preferred_element_type=F32, precision=lax.Precision.HIGHEST)


def _params(*sem):
    return pltpu.CompilerParams(dimension_semantics=sem, vmem_limit_bytes=VMEM_LIMIT)


@functools.lru_cache(maxsize=None)
def _dft_consts():
    n = np.arange(BLK, dtype=np.float64)
    f = np.arange(BLK, dtype=np.float64)
    ang = 2.0 * np.pi * np.outer(f, n) / NFREQ
    fwd = np.zeros((NFREQ, BLK), np.float64)
    fwd[:BLK] = np.cos(ang)
    fwd[BLK + 1:] = -np.sin(ang[1:])
    fwd[BLK] = np.cos(np.pi * n)
    inv = np.zeros((BLK, NFREQ), np.float64)
    scale = np.full((BLK,), 2.0)
    scale[0] = 1.0
    inv[:, :BLK] = np.cos(ang.T) * scale[None, :]
    inv[:, BLK + 1:] = -2.0 * np.sin(ang.T[:, 1:])
    inv[:, BLK] = np.cos(np.pi * n)
    inv /= NFREQ
    fr = np.arange(NFREQ)
    freq_of_row = np.where(fr < BLK, fr, np.where(fr == BLK, BLK, fr - BLK))
    sgn = np.where(freq_of_row % 2 == 0, 1.0, -1.0)[:, None]
    return fwd.astype(np.float32), inv.astype(np.float32), sgn.astype(np.float32)


@functools.lru_cache(maxsize=None)
def _scan_consts():
    t = np.arange(BLK)
    ct = t // CHUNK
    same = ct[:, None] == ct[None, :]
    tri_f = (same & (t[None, :] <= t[:, None])).astype(np.float32)
    tri_b = (same & (t[None, :] >= t[:, None])).astype(np.float32)

    def levels(p, diag):
        x = p[:, None] ^ p[None, :]
        lvl = np.zeros_like(x)
        for bit in range(1, NCHUNK.bit_length()):
            lvl = np.where(x >= (1 << (bit - 1)), bit, lvl)
        lv = np.where(p[:, None] > p[None, :], lvl, -1)
        return np.where(same, np.where(diag, 0, -1), lv).astype(np.int32)

    lv_f = levels(ct, t[None, :] <= t[:, None])
    lv_b = levels(NCHUNK - 1 - ct, t[None, :] >= t[:, None])
    return tri_f, tri_b, lv_f, lv_b


@functools.lru_cache(maxsize=None)
def _filter_positions(seq_len):
    f32 = np.float32
    j = np.arange(-seq_len, seq_len)
    p = np.abs(j)
    valid = (j > -seq_len)
    pc = np.minimum(p, seq_len - 1)
    t = np.linspace(0.0, 1.0, seq_len, dtype=f32)[pc]
    wpos = (f32(2.0 * math.pi / seq_len) * np.arange(seq_len, dtype=f32))[pc]
    bands = np.linspace(1e-4, FILT_BANDS - 1, FILT_BANDS, dtype=f32)
    arg = (bands[None, :] * wpos[:, None]).astype(f32)
    z = np.zeros((2 * seq_len, LANE), f32)
    z[:, 0] = t
    z[:, 1:1 + FILT_BANDS] = np.cos(arg)
    z[:, 1 + FILT_BANDS:FILT_EMB] = -np.sin(arg)
    z[:, FILT_EMB] = valid.astype(f32)
    return z


@functools.lru_cache(maxsize=None)
def _decay_rates():
    max_decay = math.log(DECAY_TARGET) / DECAY_FAST
    min_decay = math.log(DECAY_TARGET) / DECAY_SLOW
    return np.abs(np.linspace(min_decay, max_decay, D_B, dtype=np.float32))[None, :]


def _mod_kernel(c_ref, w_ref, b_ref, o_ref):
    c = c_ref[...]
    s = (c * _sigmoid(c)).astype(BF16)
    o_ref[...] = _dot(s, w_ref[...].astype(BF16)) + b_ref[...]


def _modulation(cond8, ada_w, ada_b):
    tn = 1536
    n = N_MOD * D_MODEL
    return pl.pallas_call(
        _mod_kernel,
        grid=(n // tn,),
        in_specs=[pl.BlockSpec((8, D_MODEL), lambda j: (0, 0)),
                  pl.BlockSpec((D_MODEL, tn), lambda j: (0, j)),
                  pl.BlockSpec((1, tn), lambda j: (0, j))],
        out_specs=pl.BlockSpec((8, tn), lambda j: (0, j)),
        out_shape=jax.ShapeDtypeStruct((8, n), F32),
        compiler_params=_params("parallel"),
        name="modulation",
    )(cond8, ada_w, ada_b)


def _lower_bounds(lbl_ref):
    l0 = lbl_ref[0]
    l1 = lbl_ref[1]
    m = jnp.maximum(l0, l1)
    e0 = jnp.exp(l0 - m)
    e1 = jnp.exp(l1 - m)
    return e0 / (e0 + e1)


def _inproj_kernel(x_ref, mod_ref, w_ref, lbl_ref, of_ref, ob_ref, h_ref):
    j = pl.program_id(1)

    @pl.when(j == 0)
    def _():
        h = x_ref[...] * (1.0 + mod_ref[0, 1:2, :]) + mod_ref[0, 0:1, :]
        h_ref[...] = h.astype(BF16)

    def project(o_ref, act):
        for r in range(x_ref.shape[0] // PROJ_ROWS):
            rows = pl.ds(r * PROJ_ROWS, PROJ_ROWS)
            o_ref[rows, :] = act(_dot(h_ref[rows, :], w_ref[...])).astype(o_ref.dtype)

    silu = lambda a: a * _sigmoid(a)
    step = STEP_COLS.index

    @pl.when(jnp.logical_or(j == step(CB_FF), j == step(CB_FB)))
    def _():
        lb2 = _lower_bounds(lbl_ref)
        lb = jnp.where(j == step(CB_FF), lb2[0:1, :], lb2[1:2, :])
        project(of_ref, lambda a: jnp.log(lb + (1.0 - lb) * _sigmoid(a)))

    @pl.when(jnp.logical_or(j == step(CB_Q), j == step(CB_G)))
    def _():
        project(ob_ref, silu)

    @pl.when(jnp.logical_or(j == step(CB_GA), j == step(CB_GB)))
    def _():
        project(ob_ref, _sigmoid)

    @pl.when(j >= step(CB_I))
    def _():
        project(ob_ref, lambda a: a)


def _weight_col(j):
    col = jnp.int32(STEP_COLS[-1])
    for step in reversed(range(len(STEP_COLS) - 1)):
        col = jnp.where(j == step, STEP_COLS[step], col)
    return col


def _inproj(x, mod3, w_bf, lb_logits, cond_row):
    tm = 2048
    t = x.shape[0]
    n_b16 = len(STEP_COLS) - N_F32_COLS
    return pl.pallas_call(
        _inproj_kernel,
        grid=(t // tm, len(STEP_COLS)),
        in_specs=[pl.BlockSpec((tm, D_MODEL), lambda i, j: (i, 0)),
                  pl.BlockSpec((1, N_MOD, D_MODEL), lambda i, j: (cond_row(i * tm), 0, 0)),
                  pl.BlockSpec((D_MODEL, D_MODEL), lambda i, j: (0, _weight_col(j))),
                  pl.BlockSpec((2, 2, D_MODEL), lambda i, j: (0, 0, 0))],
        out_specs=[pl.BlockSpec((tm, D_MODEL), lambda i, j: (i, jnp.minimum(j, N_F32_COLS - 1))),
                   pl.BlockSpec((tm, D_MODEL), lambda i, j: (i, jnp.maximum(j - N_F32_COLS, 0)))],
        out_shape=[jax.ShapeDtypeStruct((t, N_F32_COLS * D_MODEL), F32),
                   jax.ShapeDtypeStruct((t, n_b16 * D_MODEL), BF16)],
        scratch_shapes=[pltpu.VMEM((tm, D_MODEL), BF16)],
        compiler_params=_params("parallel", "arbitrary"),
        name="inproj",
    )(x, mod3, w_bf, lb_logits)


def _chunk_cumsum(lf, tri):
    lf_hi = lf.astype(BF16)
    lf_lo = (lf - lf_hi.astype(F32)).astype(BF16)
    return _dot(tri, lf_hi) + _dot(tri, lf_lo)


def _hgrn_direction(q, lf, b, v, st, lv, reverse, use_state):
    k = 1.0 - jnp.exp(lf)
    qe = q.astype(F32) * jnp.exp(b)
    k0 = k * jnp.exp(-b)

    order = [NCHUNK - 1 - i for i in range(NCHUNK)] if reverse else list(range(NCHUNK))
    chunk_of = {p: i for i, p in enumerate(order)}
    sl = lambda i: slice(i * CHUNK, (i + 1) * CHUNK)
    last_row = lambda i: (i * CHUNK) if reverse else (i * CHUNK + CHUNK - 1)
    c = [None] * NCHUNK
    for p in range(NCHUNK):
        r = last_row(chunk_of[p])
        c[p] = b[r:r + 1, :]
    cum = [jnp.zeros_like(c[0])]
    for p in range(NCHUNK):
        cum.append(cum[p] + c[p])
    total = cum[NCHUNK]

    qe_c, ke_c = {}, {}
    for p in range(NCHUNK):
        i = chunk_of[p]
        qe_c[p] = qe[sl(i), :]
        ke_c[p] = k0[sl(i), :] * jnp.exp(c[p])

    def assemble(parts):
        return jnp.concatenate([parts[order[i]] for i in range(NCHUNK)], axis=0).astype(BF16)

    cph = NCHUNK // 2
    nlev = NCHUNK.bit_length() - 1

    def half_rows(hh):
        first = (1 - hh) if reverse else hh
        return slice(first * cph * CHUNK, (first + 1) * cph * CHUNK)

    def assemble_half(parts, hh):
        ps = range(hh * cph, (hh + 1) * cph)
        return jnp.concatenate([parts[p] for p in (reversed(ps) if reverse else ps)], axis=0).astype(BF16)

    zero = jnp.zeros((CHUNK, LANE), F32)
    qe_bf, k0_bf = qe.astype(BF16), k0.astype(BF16)
    s_half = [jnp.where(lv == 0, _dot_nt(qe_bf[half_rows(hh), :], k0_bf[half_rows(hh), :]), 0.0)
              for hh in range(2)]
    for lev in range(1, nlev):
        mid = 1 << (lev - 1)
        qp, kp = {}, {}
        for p in range(NCHUNK):
            pm = ((p >> lev) << lev) + mid
            if p >= pm:
                qp[p] = qe_c[p] * jnp.exp(cum[p] - cum[pm])
                kp[p] = zero
            else:
                qp[p] = zero
                kp[p] = ke_c[p] * jnp.exp(cum[pm] - cum[p + 1])
        for hh in range(2):
            s_lev = _dot_nt(assemble_half(qp, hh), assemble_half(kp, hh))
            s_half[hh] = jnp.where(lv == lev, s_lev, s_half[hh])
    q_top = assemble_half({p: qe_c[p] * jnp.exp(cum[p] - cum[cph]) for p in range(cph, NCHUNK)}, 1)
    k_top = assemble_half({p: ke_c[p] * jnp.exp(cum[cph] - cum[p + 1]) for p in range(cph)}, 0)
    s_top = _dot_nt(q_top, k_top)

    v_bf = v.astype(BF16)
    v_half = [v_bf[half_rows(hh), :] for hh in range(2)]
    out_half = [_dot(s_half[0].astype(BF16), v_half[0]),
                _dot(jnp.concatenate([s_top, s_half[1]], axis=1).astype(BF16),
                     jnp.concatenate(v_half, axis=0))]
    out = jnp.concatenate(out_half[::-1] if reverse else out_half, axis=0)
    if use_state:
        q_start = assemble({p: qe_c[p] * jnp.exp(cum[p]) for p in range(NCHUNK)})
        out = out + _dot_nt(q_start, st.astype(BF16))
    k_end = assemble({p: ke_c[p] * jnp.exp(total - cum[p + 1]) for p in range(NCHUNK)})
    upd = _dot_tn(v_bf, k_end)
    new_st = st * jnp.exp(total) + upd if use_state else upd
    return out, new_st


def _hgrn_kernel(*refs, nb, zero_init, emit_state):
    it = iter(refs)
    qf_ref, lff_ref, vf_ref, qb_ref, lfb_ref, vb_ref = [next(it) for _ in range(6)]
    s0_ref = None if zero_init else next(it)
    lvf_ref, lvb_ref, trif_ref, trib_ref = [next(it) for _ in range(4)]
    of_ref = next(it)
    ob_ref = next(it) if nb > 1 else None
    so_ref = next(it) if emit_state else None
    st_ref = next(it)
    i = pl.program_id(2)

    use_state = not (zero_init and nb == 1)
    if use_state:
        @pl.when(i == 0)
        def _():
            for d in range(2):
                for h in range(HPS):
                    st_ref[d, h] = jnp.zeros((DV, DK), F32) if zero_init else s0_ref[0, 0, d, h].T

    lf_f = lff_ref[...]
    lf_b = lfb_ref[...]
    b_f = _chunk_cumsum(lf_f, trif_ref[...])
    b_b = _chunk_cumsum(lf_b, trib_ref[...])
    lv_f = lvf_ref[...]
    lv_b = lvb_ref[...]
    for h in range(HPS):
        hs = slice(h * LANE, (h + 1) * LANE)
        of, stf = _hgrn_direction(qf_ref[:, hs], lf_f[:, hs], b_f[:, hs], vf_ref[:, hs], st_ref[0, h],
                                  lv_f, False, use_state)
        ob, stb = _hgrn_direction(qb_ref[:, hs], lf_b[:, hs], b_b[:, hs], vb_ref[:, hs], st_ref[1, h],
                                  lv_b, True, use_state)
        if nb > 1:
            of_ref[:, hs] = of
            ob_ref[:, hs] = ob
        else:
            of_ref[:, hs] = of + ob
        if nb > 1:
            st_ref[0, h] = stf
            st_ref[1, h] = stb
        if emit_state:
            @pl.when(i == nb - 1)
            def _():
                so_ref[0, 0, 0, h] = stf.T
                so_ref[0, 0, 1, h] = stb.T


def _hgrn(pf, pb, state, nseq, nb, emit_state):
    zero_init = state is None
    t = pf.shape[0]
    tri_f, tri_b, lv_f, lv_b = _scan_consts()
    wid = HPS * LANE
    per = D_MODEL // wid
    fwd = lambda cb: pl.BlockSpec((BLK, wid), lambda b, h, i, cb=cb: (b * nb + i, cb * per + h))
    bwd = lambda cb: pl.BlockSpec((BLK, wid), lambda b, h, i, cb=cb: (b * nb + nb - 1 - i, cb * per + h))
    const = lambda n=BLK: pl.BlockSpec((n, n), lambda b, h, i: (0, 0))
    hl = BLK // 2
    st_spec = pl.BlockSpec((1, 1, 2, HPS, DK, DV), lambda b, h, i: (b, 0, 0, h, 0, 0))
    in_specs = [fwd(OB_Q), fwd(OF_FF), fwd(OB_I), bwd(OB_Q), bwd(OF_FB), bwd(OB_I)]
    args = [pb, pf, pb, pb, pf, pb]
    if not zero_init:
        in_specs.append(st_spec)
        args.append(state)
    in_specs += [const(hl), const(hl), const(), const()]
    args += [jnp.asarray(lv_f[:hl, :hl]), jnp.asarray(lv_b[:hl, :hl]),
             jnp.asarray(tri_f, BF16), jnp.asarray(tri_b, BF16)]
    out_specs = [pl.BlockSpec((BLK, wid), lambda b, h, i: (b * nb + i, h))]
    out_shape = [jax.ShapeDtypeStruct((t, D_MODEL), F32)]
    if nb > 1:
        out_specs.append(pl.BlockSpec((BLK, wid), lambda b, h, i: (b * nb + nb - 1 - i, h)))
        out_shape.append(jax.ShapeDtypeStruct((t, D_MODEL), F32))
    if emit_state:
        out_specs.append(st_spec)
        out_shape.append(jax.ShapeDtypeStruct((nseq, DEPTH, 2, H_A, DK, DV), F32))
    return pl.pallas_call(
        functools.partial(_hgrn_kernel, nb=nb, zero_init=zero_init, emit_state=emit_state),
        grid=(nseq, H_A // HPS, nb),
        in_specs=in_specs,
        out_specs=out_specs,
        out_shape=out_shape,
        scratch_shapes=[pltpu.VMEM((2, HPS, DV, DK), F32)],
        compiler_params=_params("parallel", "parallel", "arbitrary"),
        name="hgrn_scan",
    )(*args)


def _split_bf16(x):
    hi = x.astype(BF16)
    return hi, (x - hi.astype(F32)).astype(BF16)


def _dot3(a, b):
    a_hi, a_lo = _split_bf16(a)
    b_hi, b_lo = _split_bf16(b)
    return _dot(a_hi, b_hi) + (_dot(a_hi, b_lo) + _dot(a_lo, b_hi))


def _filter_kernel(z_ref, zt_ref, w1_ref, b1_ref, w2_ref, b2_ref, w3_ref, b3_ref, fq_ref, w4_ref, dec_ref,
                   fhi_ref, flo_ref, sgn_ref, o_ref, prev_ref):
    zp = z_ref[...]
    pad = jnp.zeros((LANE - FILT_ORDER, BLK), F32)

    def layer(x, w_ref, b_ref, k):
        h = jnp.sin(fq_ref[:, k:k + 1] * (_dot_hi(w_ref[...], x) + b_ref[...]))
        return jnp.concatenate([h, pad], axis=0)

    h = layer(zt_ref[...], w1_ref, b1_ref, 0)
    h = layer(h, w2_ref, b2_ref, 1)
    h = layer(h, w3_ref, b3_ref, 2)
    a = _dot3(h.T, w4_ref[...])
    window = jnp.exp(-zp[:, 0:1] * dec_ref[...]) + DECAY_SHIFT
    a = a * window * zp[:, FILT_EMB:FILT_EMB + 1]
    a_hi, a_lo = _split_bf16(a)
    f_hi = fhi_ref[...]
    ah = _dot(f_hi, a_hi) + (_dot(f_hi, a_lo) + _dot(flo_ref[...], a_hi))
    o_ref[0] = ah + sgn_ref[...] * prev_ref[...]
    prev_ref[...] = ah


def _filter_spectra(seq_len, w1, b1, w2, b2, w3, b3, freq, w4):
    nb = seq_len // BLK
    fwd_dft, _, sgn = _dft_consts()
    zpos_np = _filter_positions(seq_len)
    zpos = jnp.asarray(zpos_np)
    zpos_t = jnp.asarray(np.ascontiguousarray(zpos_np.T))
    f_hi, f_lo = _split_bf16(jnp.asarray(fwd_dft))
    wt = lambda w: jnp.pad(w, ((0, LANE - w.shape[0]), (0, 0))).T
    colv = lambda b: b[:, None]
    w1p, w2p, w3p = wt(w1), wt(w2), wt(w3)
    b1p, b2p, b3p = colv(b1), colv(b2), colv(b3)
    fqp = freq.T
    w4p = jnp.pad(w4, ((0, LANE - w4.shape[0]), (0, 0)))
    small = lambda shape: pl.BlockSpec(shape, lambda m: (0, 0))
    return pl.pallas_call(
        _filter_kernel,
        grid=(2 * nb,),
        in_specs=[pl.BlockSpec((BLK, LANE), lambda m: (m, 0)),
                  pl.BlockSpec((LANE, BLK), lambda m: (0, m)),
                  small((FILT_ORDER, LANE)), small((FILT_ORDER, 1)), small((FILT_ORDER, LANE)),
                  small((FILT_ORDER, 1)), small((FILT_ORDER, LANE)), small((FILT_ORDER, 1)),
                  small((FILT_ORDER, 3)),
                  pl.BlockSpec((LANE, D_B), lambda m: (0, jnp.where(m < nb, 1, 0))),
                  small((1, D_B)), small((NFREQ, BLK)), small((NFREQ, BLK)), small((NFREQ, 1))],
        out_specs=pl.BlockSpec((1, NFREQ, D_B), lambda m: (jnp.maximum(m - 1, 0), 0, 0)),
        out_shape=jax.ShapeDtypeStruct((2 * nb - 1, NFREQ, D_B), F32),
        scratch_shapes=[pltpu.VMEM((NFREQ, D_B), F32)],
        compiler_params=_params("arbitrary"),
        name="hyena_filter",
    )(zpos, zpos_t, w1p, b1p, w2p, b2p, w3p, b3p, fqp, w4p, jnp.asarray(_decay_rates()),
      f_hi, f_lo, jnp.asarray(sgn))


def _short_conv_gate(u0, u1, uv, w_refs, b_refs, row_len):
    sublane = lax.broadcasted_iota(jnp.int32, (SUBLANES, 1), 0)

    def zero_rows(x, offset):
        slabs = []
        for g in range(BLK // SUBLANES):
            slab = x[g * SUBLANES:(g + 1) * SUBLANES, :]
            hit = (offset - g * SUBLANES) % row_len
            if hit < SUBLANES:
                slab = jnp.where(sublane == hit, 0.0, slab)
            slabs.append(slab)
        return jnp.concatenate(slabs, axis=0)

    def conv(u, w_ref, b_ref):
        u = u.astype(F32)
        up = zero_rows(pltpu.roll(u, 1, 0), 0)
        dn = zero_rows(pltpu.roll(u, BLK - 1, 0), row_len - 1)
        return up * w_ref[0:1, :] + u * w_ref[1:2, :] + dn * w_ref[2:3, :] + b_ref[...]

    return (conv(u0, w_refs[0], b_refs[0]),
            conv(uv, w_refs[2], b_refs[2]) * conv(u1, w_refs[1], b_refs[1]))


def _hy_single_kernel(x0_ref, x1_ref, v_ref, w0_ref, w1_ref, wv_ref, b0_ref, b1_ref, bv_ref, f_ref, kh_ref,
                      skip_ref, g_ref, o_ref, *, row_len):
    x0, z = _short_conv_gate(x0_ref[...], x1_ref[...], v_ref[...], (w0_ref, w1_ref, wv_ref),
                             (b0_ref, b1_ref, bv_ref), row_len)
    zh = _dot(f_ref[...], z.astype(BF16))
    zr, zi = zh[:BLK, :], zh[BLK:, :]
    kr, ki = kh_ref[0, :BLK, :], kh_ref[0, BLK:, :]
    p = zr * kr
    q = zi * ki
    r = zr * ki + zi * kr
    row0 = lax.broadcasted_iota(jnp.int32, (BLK, 1), 0) == 0
    yh = jnp.concatenate([jnp.where(row0, p, p - q), jnp.where(row0, q, r)], axis=0).astype(BF16)
    y = _dot(g_ref[...], yh) + z * skip_ref[...]
    o_ref[...] = (x0 * y).astype(o_ref.dtype)


def _hyena_single(pb, khat, conv_w, conv_b, skip, row_len):
    t = pb.shape[0]
    fwd_dft, inv_dft, _ = _dft_consts()
    col = lambda cb: pl.BlockSpec((BLK, D_B), lambda i, cb=cb: (i, cb))
    wcol = lambda r, k: pl.BlockSpec((r, D_B), lambda i, k=k: (0, k))
    conv_b = conv_b[None, :]
    return pl.pallas_call(
        functools.partial(_hy_single_kernel, row_len=row_len),
        grid=(t // BLK,),
        in_specs=[col(OB_X0), col(OB_X1), col(OB_V),
                  wcol(3, 0), wcol(3, 1), wcol(3, 2), wcol(1, 0), wcol(1, 1), wcol(1, 2),
                  pl.BlockSpec((NFREQ, BLK), lambda i: (0, 0)),
                  pl.BlockSpec((1, NFREQ, D_B), lambda i: (0, 0, 0)),
                  pl.BlockSpec((1, D_B), lambda i: (0, 0)),
                  pl.BlockSpec((BLK, NFREQ), lambda i: (0, 0))],
        out_specs=pl.BlockSpec((BLK, D_B), lambda i: (i, 0)),
        out_shape=jax.ShapeDtypeStruct((t, D_B), BF16),
        compiler_params=_params("parallel"),
        name="hyena_single",
    )(pb, pb, pb, conv_w, conv_w, conv_w, conv_b, conv_b, conv_b, jnp.asarray(fwd_dft).astype(BF16), khat,
      skip[None, :], jnp.asarray(inv_dft).astype(BF16))


HY_DT = 2 * LANE
ROWG = 16


def _hy_multi_kernel(x0_ref, x1_ref, v_ref, w0_ref, w1_ref, wv_ref, b0_ref, b1_ref, bv_ref, f_ref, kh_ref,
                     skip_ref, g_ref, o_ref, zh_ref, z_ref, x0s_ref, yh_ref, *, nb, row_len):
    dt = o_ref.shape[1]

    def front(blk, carry):
        rows = pl.ds(pl.multiple_of(blk * BLK, BLK), BLK)
        x0, z = _short_conv_gate(x0_ref[rows, :], x1_ref[rows, :], v_ref[rows, :],
                                 (w0_ref, w1_ref, wv_ref), (b0_ref, b1_ref, bv_ref), row_len)
        x0s_ref[rows, :] = x0
        z_ref[rows, :] = z
        zh_ref[blk] = _dot(f_ref[...], z.astype(BF16))
        return carry

    lax.fori_loop(0, nb, front, 0)

    row_in_group = lax.broadcasted_iota(jnp.int32, (ROWG, 1), 0)

    def back(i, carry):
        def row_group(rg, carry2):
            re = pl.ds(pl.multiple_of(rg * ROWG, ROWG), ROWG)
            im = pl.ds(pl.multiple_of(BLK + rg * ROWG, ROWG), ROWG)

            def body(j, acc):
                p, q, r = acc
                kidx = i - j + (nb - 1)
                zr = zh_ref[j, re, :]
                zi = zh_ref[j, im, :]
                kr = kh_ref[kidx, re, :]
                ki = kh_ref[kidx, im, :]
                return (p + zr * kr, q + zi * ki, r + (zr * ki + zi * kr))

            zeros = jnp.zeros((ROWG, dt), F32)
            p, q, r = lax.fori_loop(0, nb, body, (zeros, zeros, zeros), unroll=True)
            row0 = (row_in_group + rg * ROWG) == 0
            yh_ref[re, :] = jnp.where(row0, p, p - q).astype(BF16)
            yh_ref[im, :] = jnp.where(row0, q, r).astype(BF16)
            return carry2

        lax.fori_loop(0, BLK // ROWG, row_group, 0)
        rows = pl.ds(pl.multiple_of(i * BLK, BLK), BLK)
        y = _dot(g_ref[...], yh_ref[...]) + z_ref[rows, :] * skip_ref[...]
        o_ref[rows, :] = (x0s_ref[rows, :] * y).astype(o_ref.dtype)
        return carry

    lax.fori_loop(0, nb, back, 0)


def _hyena_multi(pb, khat, conv_w, conv_b, skip, nseq, nb, row_len):
    t = pb.shape[0]
    seq_len = nb * BLK
    dt = HY_DT
    per = D_B // dt
    fwd_dft, inv_dft, _ = _dft_consts()
    seq = lambda cb: pl.BlockSpec((seq_len, dt), lambda d, b, cb=cb: (b, cb * per + d))
    wcol = lambda r, k: pl.BlockSpec((r, dt), lambda d, b, k=k: (0, k * per + d))
    conv_b = conv_b[None, :]
    return pl.pallas_call(
        functools.partial(_hy_multi_kernel, nb=nb, row_len=row_len),
        grid=(per, nseq),
        in_specs=[seq(OB_X0), seq(OB_X1), seq(OB_V),
                  wcol(3, 0), wcol(3, 1), wcol(3, 2), wcol(1, 0), wcol(1, 1), wcol(1, 2),
                  pl.BlockSpec((NFREQ, BLK), lambda d, b: (0, 0)),
                  pl.BlockSpec((2 * nb - 1, NFREQ, dt), lambda d, b: (0, 0, d), pipeline_mode=pl.Buffered(1)),
                  pl.BlockSpec((1, dt), lambda d, b: (0, d)),
                  pl.BlockSpec((BLK, NFREQ), lambda d, b: (0, 0))],
        out_specs=pl.BlockSpec((seq_len, dt), lambda d, b: (b, d)),
        out_shape=jax.ShapeDtypeStruct((t, D_B), BF16),
        scratch_shapes=[pltpu.VMEM((nb, NFREQ, dt), F32), pltpu.VMEM((seq_len, dt), F32),
                        pltpu.VMEM((seq_len, dt), F32), pltpu.VMEM((NFREQ, dt), BF16)],
        compiler_params=_params("parallel", "parallel"),
        name="hyena_multi",
    )(pb, pb, pb, conv_w, conv_w, conv_w, conv_b, conv_b, conv_b, jnp.asarray(fwd_dft).astype(BF16), khat,
      skip[None, :], jnp.asarray(inv_dft).astype(BF16))


def _layer_norm(y, g, b):
    mu = jnp.mean(y, axis=-1, keepdims=True)
    yc = y - mu
    var = jnp.mean(yc * yc, axis=-1, keepdims=True)
    return yc * lax.rsqrt(var + LN_EPS) * g + b


def _post_kernel(*refs, n_dir):
    (g_ref, ga_ref, gb_ref, hy_ref, x_ref, mod_ref, nw_ref, pa_ref, pb_ref, wo_ref, lg_ref, lb_ref,
     wg_ref, wu_ref, wo2_ref, lg2_ref, lb2_ref, o_ref) = refs[n_dir:]
    o = refs[0][...]
    for d_ref in refs[1:n_dir]:
        o = o + d_ref[...]
    nw = nw_ref[...]
    parts = []
    for h in range(H_A):
        oh = o[:, h * DV:(h + 1) * DV]
        ms = jnp.mean(oh * oh, axis=-1, keepdims=True)
        parts.append(oh * lax.rsqrt(ms + RMS_EPS) * nw)
    oa = jnp.concatenate(parts, axis=1) * g_ref[...]
    a = _dot(oa.astype(BF16), pa_ref[...])
    b = _dot(hy_ref[...].astype(BF16), pb_ref[...])
    merged = ga_ref[...] * a + gb_ref[...] * b
    mix = _dot(merged.astype(BF16), wo_ref[...])
    y = ALPHA * x_ref[...] + mod_ref[0, 2:3, :] * mix
    x1 = _layer_norm(y, lg_ref[...], lb_ref[...])

    h = (x1 * (1.0 + mod_ref[0, 4:5, :]) + mod_ref[0, 3:4, :]).astype(BF16)
    gt = _dot(h, wg_ref[...])
    up = _dot(h, wu_ref[...])
    act = (gt * _sigmoid(gt) * up).astype(BF16)
    ff = _dot(act, wo2_ref[...])
    y2 = ALPHA * x1 + mod_ref[0, 5:6, :] * ff
    o_ref[...] = _layer_norm(y2, lg2_ref[...], lb2_ref[...])


def _post(o_dirs, pb16, o_hy, x, mod3, cond_row, norm_w, pa, pb, wo, ln_g, ln_b,
          ffn_w_in, ffn_w_out, ln2_g, ln2_b):
    tm = 256
    t = x.shape[0]
    n_dir = len(o_dirs)
    once = lambda shape, idx: pl.BlockSpec(shape, idx, pipeline_mode=pl.Buffered(1))
    tok = pl.BlockSpec((tm, D_MODEL), lambda i: (i, 0))
    col = lambda cb: pl.BlockSpec((tm, D_MODEL), lambda i, cb=cb: (i, cb))
    mat = once((D_MODEL, D_MODEL), lambda i: (0, 0))
    vec = pl.BlockSpec((1, D_MODEL), lambda i: (0, 0))
    return pl.pallas_call(
        functools.partial(_post_kernel, n_dir=n_dir),
        grid=(t // tm,),
        in_specs=[tok] * n_dir + [col(OB_G), col(OB_GA), col(OB_GB), tok, tok,
                  pl.BlockSpec((1, N_MOD, D_MODEL), lambda i: (cond_row(i * tm), 0, 0)),
                  pl.BlockSpec((1, DV), lambda i: (0, 0)),
                  mat, mat, mat, vec, vec,
                  once((D_MODEL, D_FF), lambda i: (0, 0)),
                  once((D_MODEL, D_FF), lambda i: (0, 1)),
                  once((D_FF, D_MODEL), lambda i: (0, 0)),
                  vec, vec],
        out_specs=tok,
        out_shape=jax.ShapeDtypeStruct((t, D_MODEL), F32),
        compiler_params=_params("parallel"),
        name="merge_ffn",
    )(*o_dirs, pb16, pb16, pb16, o_hy, x, mod3, norm_w[None, :], pa, pb, wo, ln_g[None, :], ln_b[None, :],
      ffn_w_in, ffn_w_in, ffn_w_out, ln2_g[None, :], ln2_b[None, :])


def _trunk(x, mod3, cond_row, nseq, seq_len, row_len, state, emit_state, w, lb_logits):
    nb = seq_len // BLK
    pf, pb = _inproj(x, mod3, w["w_in"], lb_logits, cond_row)
    hg = _hgrn(pf, pb, state, nseq, nb, emit_state)
    n_dir = 2 if nb > 1 else 1
    khat = _filter_spectra(seq_len, w["filt_w1"], w["filt_b1"], w["filt_w2"], w["filt_b2"],
                           w["filt_w3"], w["filt_b3"], w["filt_freq"], w["filt_w4"])
    if nb == 1:
        o_hy = _hyena_single(pb, khat, w["hy_conv_w"], w["hy_conv_b"], w["hy_skip"], row_len)
    else:
        o_hy = _hyena_multi(pb, khat, w["hy_conv_w"], w["hy_conv_b"], w["hy_skip"], nseq, nb, row_len)
    x2 = _post(hg[:n_dir], pb, o_hy, x, mod3, cond_row, w["hgrn_norm_w"], w["proj_a"], w["proj_b"],
               w["w_out"], w["ln1_g"], w["ln1_b"], w["ffn_w_in"], w["ffn_w_out"], w["ln2_g"], w["ln2_b"])
    return x2, (hg[n_dir] if emit_state else None)


def kernel(x_prompt, x_sample, state_hgrn, c, c_ctx, ada_w, ada_b, w_in, hgrn_lb_logits, hgrn_norm_w,
           hy_conv_w, hy_conv_b, filt_w1, filt_b1, filt_w2, filt_b2, filt_w3, filt_b3, filt_freq, filt_w4,
           hy_skip, proj_a, proj_b, w_out, ln1_g, ln1_b, ffn_w_in, ffn_w_out, ln2_g, ln2_b):
    assert ada_w.shape[0] == DEPTH == 1
    batch, seq, _ = x_prompt.shape
    dec_batch, dec_seq, _ = x_sample.shape
    assert seq % BLK == 0 and dec_seq % BLK == 0 and BLK % GRID_W == 0 and dec_batch + 1 <= 8

    w = dict(w_in=w_in[0].astype(BF16), hy_conv_w=hy_conv_w[0], hy_conv_b=hy_conv_b[0],
             filt_w1=filt_w1[0], filt_b1=filt_b1[0], filt_w2=filt_w2[0], filt_b2=filt_b2[0],
             filt_w3=filt_w3[0], filt_b3=filt_b3[0], filt_freq=filt_freq[0], filt_w4=filt_w4[0],
             hy_skip=hy_skip[0], hgrn_norm_w=hgrn_norm_w[0], proj_a=proj_a[0].astype(BF16),
             proj_b=proj_b[0].astype(BF16), w_out=w_out[0].astype(BF16), ln1_g=ln1_g[0], ln1_b=ln1_b[0],
             ffn_w_in=ffn_w_in[0].astype(BF16), ffn_w_out=ffn_w_out[0].astype(BF16),
             ln2_g=ln2_g[0], ln2_b=ln2_b[0])

    cond8 = jnp.zeros((8, D_MODEL), F32).at[0].set(c_ctx).at[1:1 + dec_batch].set(c)
    mod3 = _modulation(cond8, ada_w[0], ada_b[0][None, :]).reshape(8, N_MOD, D_MODEL)

    xp = x_prompt.reshape(batch * seq, D_MODEL)
    xs = x_sample.reshape(dec_batch * dec_seq, D_MODEL)
    yp, new_state = _trunk(xp, mod3, lambda tok: 0, batch, seq, seq, None, True, w, hgrn_lb_logits)
    ys, _ = _trunk(xs, mod3, lambda tok: 1 + tok // dec_seq, dec_batch, dec_seq, GRID_W, state_hgrn, False,
                   w, hgrn_lb_logits)
    return (yp.reshape(batch, seq, D_MODEL), ys.reshape(dec_batch, dec_seq, D_MODEL), new_state)
```

```python
import functools
import math

import numpy as np
import jax
import jax.numpy as jnp
from jax import lax
from jax.experimental import pallas as pl
from jax.experimental.pallas import tpu as pltpu

F32 = jnp.float32
BF16 = jnp.bfloat16

D_MODEL = 1024
DEPTH = 1
GRID_W = 64
H_A = 8
DK = 128
DV = 128
D_B = 1024
FILT_EMB = 33
FILT_BANDS = 16
FILT_ORDER = 64
DECAY_FAST = 0.3
DECAY_SLOW = 1.5
DECAY_TARGET = 1e-2
DECAY_SHIFT = 0.05
D_FF = 2816
N_MOD = 6
W_IN_COLS = 10 * D_MODEL
ALPHA = (2.0 * DEPTH) ** 0.25
LN_EPS = 1e-5
RMS_EPS = 1e-6

LANE = 128
SUBLANES = 8
BLK = 256
NFREQ = 2 * BLK
CHUNK = 32
NCHUNK = BLK // CHUNK
HPS = 8
PROJ_ROWS = 128
VMEM_LIMIT = 56 * 1024 * 1024

CB_Q, CB_FF, CB_FB, CB_I, CB_G, CB_X0, CB_X1, CB_V, CB_GA, CB_GB = range(10)
STEP_COLS = (CB_FF, CB_FB, CB_Q, CB_G, CB_GA, CB_GB, CB_I, CB_X0, CB_X1, CB_V)
N_F32_COLS = 2
OF_FF, OF_FB = range(N_F32_COLS)
OB_Q, OB_G, OB_GA, OB_GB, OB_I, OB_X0, OB_X1, OB_V = range(len(STEP_COLS) - N_F32_COLS)


def _sigmoid(x):
    return 1.0 / (1.0 + jnp.exp(-x))


def _dot(a, b):
    return jnp.dot(a, b, preferred_element_type=F32)


def _dot_nt(a, b):
    return lax.dot_general(a, b, (((1,), (1,)), ((), ())), preferred_element_type=F32)


def _dot_tn(a, b):
    return lax.dot_general(a, b, (((0,), (0,)), ((), ())), preferred_element_type=F32)


def _dot_hi(a, b):
    return jnp.dot(a, b, preferred_element_type=F32, precision=lax.Precision.HIGHEST)


def _params(*sem):
    return pltpu.CompilerParams(dimension_semantics=sem, vmem_limit_bytes=VMEM_LIMIT)


@functools.lru_cache(maxsize=None)
def _dft_consts():
    n = np.arange(BLK, dtype=np.float64)
    f = np.arange(BLK, dtype=np.float64)
    ang = 2.0 * np.pi * np.outer(f, n) / NFREQ
    fwd = np.zeros((NFREQ, BLK), np.float64)
    fwd[:BLK] = np.cos(ang)
    fwd[BLK + 1:] = -np.sin(ang[1:])
    fwd[BLK] = np.cos(np.pi * n)
    inv = np.zeros((BLK, NFREQ), np.float64)
    scale = np.full((BLK,), 2.0)
    scale[0] = 1.0
    inv[:, :BLK] = np.cos(ang.T) * scale[None, :]
    inv[:, BLK + 1:] = -2.0 * np.sin(ang.T[:, 1:])
    inv[:, BLK] = np.cos(np.pi * n)
    inv /= NFREQ
    fr = np.arange(NFREQ)
    freq_of_row = np.where(fr < BLK, fr, np.where(fr == BLK, BLK, fr - BLK))
    sgn = np.where(freq_of_row % 2 == 0, 1.0, -1.0)[:, None]
    return fwd.astype(np.float32), inv.astype(np.float32), sgn.astype(np.float32)


@functools.lru_cache(maxsize=None)
def _scan_consts():
    t = np.arange(BLK)
    ct = t // CHUNK
    same = ct[:, None] == ct[None, :]
    tri_f = (same & (t[None, :] <= t[:, None])).astype(np.float32)
    tri_b = (same & (t[None, :] >= t[:, None])).astype(np.float32)

    def levels(p, diag):
        x = p[:, None] ^ p[None, :]
        lvl = np.zeros_like(x)
        for bit in range(1, NCHUNK.bit_length()):
            lvl = np.where(x >= (1 << (bit - 1)), bit, lvl)
        lv = np.where(p[:, None] > p[None, :], lvl, -1)
        return np.where(same, np.where(diag, 0, -1), lv).astype(np.int32)

    lv_f = levels(ct, t[None, :] <= t[:, None])
    lv_b = levels(NCHUNK - 1 - ct, t[None, :] >= t[:, None])
    return tri_f, tri_b, lv_f, lv_b


@functools.lru_cache(maxsize=None)
def _filter_positions(seq_len):
    f32 = np.float32
    j = np.arange(-seq_len, seq_len)
    p = np.abs(j)
    valid = (j > -seq_len)
    pc = np.minimum(p, seq_len - 1)
    t = np.linspace(0.0, 1.0, seq_len, dtype=f32)[pc]
    wpos = (f32(2.0 * math.pi / seq_len) * np.arange(seq_len, dtype=f32))[pc]
    bands = np.linspace(1e-4, FILT_BANDS - 1, FILT_BANDS, dtype=f32)
    arg = (bands[None, :] * wpos[:, None]).astype(f32)
    z = np.zeros((2 * seq_len, LANE), f32)
    z[:, 0] = t
    z[:, 1:1 + FILT_BANDS] = np.cos(arg)
    z[:, 1 + FILT_BANDS:FILT_EMB] = -np.sin(arg)
    z[:, FILT_EMB] = valid.astype(f32)
    return z


@functools.lru_cache(maxsize=None)
def _decay_rates():
    max_decay = math.log(DECAY_TARGET) / DECAY_FAST
    min_decay = math.log(DECAY_TARGET) / DECAY_SLOW
    return np.abs(np.linspace(min_decay, max_decay, D_B, dtype=np.float32))[None, :]


def _mod_kernel(c_ref, w_ref, b_ref, o_ref):
    c = c_ref[...]
    s = (c * _sigmoid(c)).astype(BF16)
    o_ref[...] = _dot(s, w_ref[...].astype(BF16)) + b_ref[...]


def _modulation(cond8, ada_w, ada_b):
    tn = 1536
    n = N_MOD * D_MODEL
    return pl.pallas_call(
        _mod_kernel,
        grid=(n // tn,),
        in_specs=[pl.BlockSpec((8, D_MODEL), lambda j: (0, 0)),
                  pl.BlockSpec((D_MODEL, tn), lambda j: (0, j)),
                  pl.BlockSpec((1, tn), lambda j: (0, j))],
        out_specs=pl.BlockSpec((8, tn), lambda j: (0, j)),
        out_shape=jax.ShapeDtypeStruct((8, n), F32),
        compiler_params=_params("parallel"),
        name="modulation",
    )(cond8, ada_w, ada_b)


def _lower_bounds(lbl_ref):
    l0 = lbl_ref[0]
    l1 = lbl_ref[1]
    m = jnp.maximum(l0, l1)
    e0 = jnp.exp(l0 - m)
    e1 = jnp.exp(l1 - m)
    return e0 / (e0 + e1)


def _inproj_kernel(x_ref, mod_ref, w_ref, lbl_ref, of_ref, ob_ref, h_ref):
    j = pl.program_id(1)

    @pl.when(j == 0)
    def _():
        h = x_ref[...] * (1.0 + mod_ref[0, 1:2, :]) + mod_ref[0, 0:1, :]
        h_ref[...] = h.astype(BF16)

    def project(o_ref, act):
        for r in range(x_ref.shape[0] // PROJ_ROWS):
            rows = pl.ds(r * PROJ_ROWS, PROJ_ROWS)
            o_ref[rows, :] = act(_dot(h_ref[rows, :], w_ref[...])).astype(o_ref.dtype)

    silu = lambda a: a * _sigmoid(a)
    step = STEP_COLS.index

    @pl.when(jnp.logical_or(j == step(CB_FF), j == step(CB_FB)))
    def _():
        lb2 = _lower_bounds(lbl_ref)
        lb = jnp.where(j == step(CB_FF), lb2[0:1, :], lb2[1:2, :])
        project(of_ref, lambda a: jnp.log(lb + (1.0 - lb) * _sigmoid(a)))

    @pl.when(jnp.logical_or(j == step(CB_Q), j == step(CB_G)))
    def _():
        project(ob_ref, silu)

    @pl.when(jnp.logical_or(j == step(CB_GA), j == step(CB_GB)))
    def _():
        project(ob_ref, _sigmoid)

    @pl.when(j >= step(CB_I))
    def _():
        project(ob_ref, lambda a: a)


def _weight_col(j):
    col = jnp.int32(STEP_COLS[-1])
    for step in reversed(range(len(STEP_COLS) - 1)):
        col = jnp.where(j == step, STEP_COLS[step], col)
    return col


def _inproj(x, mod3, w_bf, lb_logits, cond_row):
    tm = 2048
    t = x.shape[0]
    n_b16 = len(STEP_COLS) - N_F32_COLS
    return pl.pallas_call(
        _inproj_kernel,
        grid=(t // tm, len(STEP_COLS)),
        in_specs=[pl.BlockSpec((tm, D_MODEL), lambda i, j: (i, 0)),
                  pl.BlockSpec((1, N_MOD, D_MODEL), lambda i, j: (cond_row(i * tm), 0, 0)),
                  pl.BlockSpec((D_MODEL, D_MODEL), lambda i, j: (0, _weight_col(j))),
                  pl.BlockSpec((2, 2, D_MODEL), lambda i, j: (0, 0, 0))],
        out_specs=[pl.BlockSpec((tm, D_MODEL), lambda i, j: (i, jnp.minimum(j, N_F32_COLS - 1))),
                   pl.BlockSpec((tm, D_MODEL), lambda i, j: (i, jnp.maximum(j - N_F32_COLS, 0)))],
        out_shape=[jax.ShapeDtypeStruct((t, N_F32_COLS * D_MODEL), F32),
                   jax.ShapeDtypeStruct((t, n_b16 * D_MODEL), BF16)],
        scratch_shapes=[pltpu.VMEM((tm, D_MODEL), BF16)],
        compiler_params=_params("parallel", "arbitrary"),
        name="inproj",
    )(x, mod3, w_bf, lb_logits)


def _chunk_cumsum(lf, tri):
    lf_hi = lf.astype(BF16)
    lf_lo = (lf - lf_hi.astype(F32)).astype(BF16)
    return _dot(tri, lf_hi) + _dot(tri, lf_lo)


def _wavefront(units):
    waiting, active = list(units), []
    while waiting or active:
        if waiting:
            active.append(waiting.pop(0))
        for unit in list(active):
            try:
                next(unit)
            except StopIteration:
                active.remove(unit)


def _hgrn_direction(load, lv, reverse, use_state, finish):
    q, lf, b, v, st = load()
    k = 1.0 - jnp.exp(lf)
    qe = q.astype(F32) * jnp.exp(b)
    k0 = k * jnp.exp(-b)
    yield

    order = [NCHUNK - 1 - i for i in range(NCHUNK)] if reverse else list(range(NCHUNK))
    chunk_of = {p: i for i, p in enumerate(order)}
    sl = lambda i: slice(i * CHUNK, (i + 1) * CHUNK)
    last_row = lambda i: (i * CHUNK) if reverse else (i * CHUNK + CHUNK - 1)
    c = [None] * NCHUNK
    for p in range(NCHUNK):
        r = last_row(chunk_of[p])
        c[p] = b[r:r + 1, :]
    cum = [jnp.zeros_like(c[0])]
    for p in range(NCHUNK):
        cum.append(cum[p] + c[p])
    total = cum[NCHUNK]

    qe_c, ke_c = {}, {}
    for p in range(NCHUNK):
        i = chunk_of[p]
        qe_c[p] = qe[sl(i), :]
        ke_c[p] = k0[sl(i), :] * jnp.exp(c[p])

    def assemble(parts):
        return jnp.concatenate([parts[order[i]] for i in range(NCHUNK)], axis=0).astype(BF16)

    cph = NCHUNK // 2
    nlev = NCHUNK.bit_length() - 1

    def half_rows(hh):
        first = (1 - hh) if reverse else hh
        return slice(first * cph * CHUNK, (first + 1) * cph * CHUNK)

    def assemble_half(parts, hh):
        ps = range(hh * cph, (hh + 1) * cph)
        return jnp.concatenate([parts[p] for p in (reversed(ps) if reverse else ps)], axis=0).astype(BF16)

    zero = jnp.zeros((CHUNK, LANE), F32)
    qe_bf, k0_bf = qe.astype(BF16), k0.astype(BF16)
    yield
    s_half = [jnp.where(lv == 0, _dot_nt(qe_bf[half_rows(hh), :], k0_bf[half_rows(hh), :]), 0.0)
              for hh in range(2)]
    for lev in range(1, nlev):
        yield
        mid = 1 << (lev - 1)
        qp, kp = {}, {}
        for p in range(NCHUNK):
            pm = ((p >> lev) << lev) + mid
            if p >= pm:
                qp[p] = qe_c[p] * jnp.exp(cum[p] - cum[pm])
                kp[p] = zero
            else:
                qp[p] = zero
                kp[p] = ke_c[p] * jnp.exp(cum[pm] - cum[p + 1])
        for hh in range(2):
            s_lev = _dot_nt(assemble_half(qp, hh), assemble_half(kp, hh))
            s_half[hh] = jnp.where(lv == lev, s_lev, s_half[hh])
    yield
    q_top = assemble_half({p: qe_c[p] * jnp.exp(cum[p] - cum[cph]) for p in range(cph, NCHUNK)}, 1)
    k_top = assemble_half({p: ke_c[p] * jnp.exp(cum[cph] - cum[p + 1]) for p in range(cph)}, 0)
    s_top = _dot_nt(q_top, k_top)
    yield

    v_bf = v.astype(BF16)
    v_half = [v_bf[half_rows(hh), :] for hh in range(2)]
    out_half = [_dot(s_half[0].astype(BF16), v_half[0]),
                _dot(jnp.concatenate([s_top, s_half[1]], axis=1).astype(BF16),
                     jnp.concatenate(v_half, axis=0))]
    out = jnp.concatenate(out_half[::-1] if reverse else out_half, axis=0)
    yield
    if use_state:
        q_start = assemble({p: qe_c[p] * jnp.exp(cum[p]) for p in range(NCHUNK)})
        out = out + _dot_nt(q_start, st.astype(BF16))
    k_end = assemble({p: ke_c[p] * jnp.exp(total - cum[p + 1]) for p in range(NCHUNK)})
    upd = _dot_tn(v_bf, k_end)
    new_st = st * jnp.exp(total) + upd if use_state else upd
    finish(out, new_st)


def _hgrn_kernel(*refs, nb, zero_init, emit_state):
    it = iter(refs)
    qf_ref, lff_ref, vf_ref, qb_ref, lfb_ref, vb_ref = [next(it) for _ in range(6)]
    s0_ref = None if zero_init else next(it)
    lvf_ref, lvb_ref, trif_ref, trib_ref = [next(it) for _ in range(4)]
    of_ref = next(it)
    ob_ref = next(it) if nb > 1 else None
    so_ref = next(it) if emit_state else None
    st_ref = next(it)
    i = pl.program_id(2)

    use_state = not (zero_init and nb == 1)
    if use_state:
        @pl.when(i == 0)
        def _():
            for d in range(2):
                for h in range(HPS):
                    st_ref[d, h] = jnp.zeros((DV, DK), F32) if zero_init else s0_ref[0, 0, d, h].T

    lf_f = lff_ref[...]
    lf_b = lfb_ref[...]
    b_f = _chunk_cumsum(lf_f, trif_ref[...])
    b_b = _chunk_cumsum(lf_b, trib_ref[...])
    lv_f = lvf_ref[...]
    lv_b = lvb_ref[...]
    def unit(d, h):
        hs = slice(h * LANE, (h + 1) * LANE)
        q_ref, lf, b, v_ref, o_ref = ((qf_ref, lf_f, b_f, vf_ref, of_ref), (qb_ref, lf_b, b_b, vb_ref, ob_ref))[d]

        def finish(out, new_st):
            if nb > 1:
                o_ref[:, hs] = out
                st_ref[d, h] = new_st
            elif d == 0:
                of_ref[:, hs] = out
            else:
                of_ref[:, hs] += out
            if emit_state and nb == 1:
                so_ref[0, 0, d, h] = new_st.T
            elif emit_state:
                @pl.when(i == nb - 1)
                def _():
                    so_ref[0, 0, d, h] = new_st.T

        return _hgrn_direction(lambda: (q_ref[:, hs], lf[:, hs], b[:, hs], v_ref[:, hs], st_ref[d, h]),
                               (lv_f, lv_b)[d], d == 1, use_state, finish)

    _wavefront([unit(d, h) for h in range(HPS) for d in range(2)])


def _hgrn(pf, pb, state, nseq, nb, emit_state):
    zero_init = state is None
    t = pf.shape[0]
    tri_f, tri_b, lv_f, lv_b = _scan_consts()
    wid = HPS * LANE
    per = D_MODEL // wid
    fwd = lambda cb: pl.BlockSpec((BLK, wid), lambda b, h, i, cb=cb: (b * nb + i, cb * per + h))
    bwd = lambda cb: pl.BlockSpec((BLK, wid), lambda b, h, i, cb=cb: (b * nb + nb - 1 - i, cb * per + h))
    const = lambda n=BLK: pl.BlockSpec((n, n), lambda b, h, i: (0, 0))
    hl = BLK // 2
    st_spec = pl.BlockSpec((1, 1, 2, HPS, DK, DV), lambda b, h, i: (b, 0, 0, h, 0, 0))
    in_specs = [fwd(OB_Q), fwd(OF_FF), fwd(OB_I), bwd(OB_Q), bwd(OF_FB), bwd(OB_I)]
    args = [pb, pf, pb, pb, pf, pb]
    if not zero_init:
        in_specs.append(st_spec)
        args.append(state)
    in_specs += [const(hl), const(hl), const(), const()]
    args += [jnp.asarray(lv_f[:hl, :hl]), jnp.asarray(lv_b[:hl, :hl]),
             jnp.asarray(tri_f, BF16), jnp.asarray(tri_b, BF16)]
    out_specs = [pl.BlockSpec((BLK, wid), lambda b, h, i: (b * nb + i, h))]
    out_shape = [jax.ShapeDtypeStruct((t, D_MODEL), F32)]
    if nb > 1:
        out_specs.append(pl.BlockSpec((BLK, wid), lambda b, h, i: (b * nb + nb - 1 - i, h)))
        out_shape.append(jax.ShapeDtypeStruct((t, D_MODEL), F32))
    if emit_state:
        out_specs.append(st_spec)
        out_shape.append(jax.ShapeDtypeStruct((nseq, DEPTH, 2, H_A, DK, DV), F32))
    return pl.pallas_call(
        functools.partial(_hgrn_kernel, nb=nb, zero_init=zero_init, emit_state=emit_state),
        grid=(nseq, H_A // HPS, nb),
        in_specs=in_specs,
        out_specs=out_specs,
        out_shape=out_shape,
        scratch_shapes=[pltpu.VMEM((2, HPS, DV, DK), F32)],
        compiler_params=_params("parallel", "parallel", "arbitrary"),
        name="hgrn_scan",
    )(*args)


def _split_bf16(x):
    hi = x.astype(BF16)
    return hi, (x - hi.astype(F32)).astype(BF16)


def _dot3(a, b):
    a_hi, a_lo = _split_bf16(a)
    b_hi, b_lo = _split_bf16(b)
    return _dot(a_hi, b_hi) + (_dot(a_hi, b_lo) + _dot(a_lo, b_hi))


def _filter_kernel(z_ref, zt_ref, w1_ref, b1_ref, w2_ref, b2_ref, w3_ref, b3_ref, fq_ref, w4_ref, dec_ref,
                   fhi_ref, flo_ref, sgn_ref, o_ref, prev_ref):
    zp = z_ref[...]
    pad = jnp.zeros((LANE - FILT_ORDER, BLK), F32)

    def layer(x, w_ref, b_ref, k):
        h = jnp.sin(fq_ref[:, k:k + 1] * (_dot_hi(w_ref[...], x) + b_ref[...]))
        return jnp.concatenate([h, pad], axis=0)

    h = layer(zt_ref[...], w1_ref, b1_ref, 0)
    h = layer(h, w2_ref, b2_ref, 1)
    h = layer(h, w3_ref, b3_ref, 2)
    a = _dot3(h.T, w4_ref[...])
    window = jnp.exp(-zp[:, 0:1] * dec_ref[...]) + DECAY_SHIFT
    a = a * window * zp[:, FILT_EMB:FILT_EMB + 1]
    a_hi, a_lo = _split_bf16(a)
    f_hi = fhi_ref[...]
    ah = _dot(f_hi, a_hi) + (_dot(f_hi, a_lo) + _dot(flo_ref[...], a_hi))
    o_ref[0] = ah + sgn_ref[...] * prev_ref[...]
    prev_ref[...] = ah


def _filter_spectra(seq_len, w1, b1, w2, b2, w3, b3, freq, w4):
    nb = seq_len // BLK
    fwd_dft, _, sgn = _dft_consts()
    zpos_np = _filter_positions(seq_len)
    zpos = jnp.asarray(zpos_np)
    zpos_t = jnp.asarray(np.ascontiguousarray(zpos_np.T))
    f_hi, f_lo = _split_bf16(jnp.asarray(fwd_dft))
    wt = lambda w: jnp.pad(w, ((0, LANE - w.shape[0]), (0, 0))).T
    colv = lambda b: b[:, None]
    w1p, w2p, w3p = wt(w1), wt(w2), wt(w3)
    b1p, b2p, b3p = colv(b1), colv(b2), colv(b3)
    fqp = freq.T
    w4p = jnp.pad(w4, ((0, LANE - w4.shape[0]), (0, 0)))
    small = lambda shape: pl.BlockSpec(shape, lambda m: (0, 0))
    return pl.pallas_call(
        _filter_kernel,
        grid=(2 * nb,),
        in_specs=[pl.BlockSpec((BLK, LANE), lambda m: (m, 0)),
                  pl.BlockSpec((LANE, BLK), lambda m: (0, m)),
                  small((FILT_ORDER, LANE)), small((FILT_ORDER, 1)), small((FILT_ORDER, LANE)),
                  small((FILT_ORDER, 1)), small((FILT_ORDER, LANE)), small((FILT_ORDER, 1)),
                  small((FILT_ORDER, 3)),
                  pl.BlockSpec((LANE, D_B), lambda m: (0, jnp.where(m < nb, 1, 0))),
                  small((1, D_B)), small((NFREQ, BLK)), small((NFREQ, BLK)), small((NFREQ, 1))],
        out_specs=pl.BlockSpec((1, NFREQ, D_B), lambda m: (jnp.maximum(m - 1, 0), 0, 0)),
        out_shape=jax.ShapeDtypeStruct((2 * nb - 1, NFREQ, D_B), F32),
        scratch_shapes=[pltpu.VMEM((NFREQ, D_B), F32)],
        compiler_params=_params("arbitrary"),
        name="hyena_filter",
    )(zpos, zpos_t, w1p, b1p, w2p, b2p, w3p, b3p, fqp, w4p, jnp.asarray(_decay_rates()),
      f_hi, f_lo, jnp.asarray(sgn))


def _short_conv_gate(u0, u1, uv, w_refs, b_refs, row_len):
    sublane = lax.broadcasted_iota(jnp.int32, (SUBLANES, 1), 0)

    def zero_rows(x, offset):
        slabs = []
        for g in range(BLK // SUBLANES):
            slab = x[g * SUBLANES:(g + 1) * SUBLANES, :]
            hit = (offset - g * SUBLANES) % row_len
            if hit < SUBLANES:
                slab = jnp.where(sublane == hit, 0.0, slab)
            slabs.append(slab)
        return jnp.concatenate(slabs, axis=0)

    def conv(u, w_ref, b_ref):
        u = u.astype(F32)
        up = zero_rows(pltpu.roll(u, 1, 0), 0)
        dn = zero_rows(pltpu.roll(u, BLK - 1, 0), row_len - 1)
        return up * w_ref[0:1, :] + u * w_ref[1:2, :] + dn * w_ref[2:3, :] + b_ref[...]

    return (conv(u0, w_refs[0], b_refs[0]),
            conv(uv, w_refs[2], b_refs[2]) * conv(u1, w_refs[1], b_refs[1]))


def _hy_single_kernel(x0_ref, x1_ref, v_ref, w0_ref, w1_ref, wv_ref, b0_ref, b1_ref, bv_ref, f_ref, kh_ref,
                      skip_ref, g_ref, o_ref, *, row_len):
    x0, z = _short_conv_gate(x0_ref[...], x1_ref[...], v_ref[...], (w0_ref, w1_ref, wv_ref),
                             (b0_ref, b1_ref, bv_ref), row_len)
    zh = _dot(f_ref[...], z.astype(BF16))
    zr, zi = zh[:BLK, :], zh[BLK:, :]
    kr, ki = kh_ref[0, :BLK, :], kh_ref[0, BLK:, :]
    p = zr * kr
    q = zi * ki
    r = zr * ki + zi * kr
    row0 = lax.broadcasted_iota(jnp.int32, (BLK, 1), 0) == 0
    yh = jnp.concatenate([jnp.where(row0, p, p - q), jnp.where(row0, q, r)], axis=0).astype(BF16)
    y = _dot(g_ref[...], yh) + z * skip_ref[...]
    o_ref[...] = (x0 * y).astype(o_ref.dtype)


def _hyena_single(pb, khat, conv_w, conv_b, skip, row_len):
    t = pb.shape[0]
    fwd_dft, inv_dft, _ = _dft_consts()
    col = lambda cb: pl.BlockSpec((BLK, D_B), lambda i, cb=cb: (i, cb))
    wcol = lambda r, k: pl.BlockSpec((r, D_B), lambda i, k=k: (0, k))
    conv_b = conv_b[None, :]
    return pl.pallas_call(
        functools.partial(_hy_single_kernel, row_len=row_len),
        grid=(t // BLK,),
        in_specs=[col(OB_X0), col(OB_X1), col(OB_V),
                  wcol(3, 0), wcol(3, 1), wcol(3, 2), wcol(1, 0), wcol(1, 1), wcol(1, 2),
                  pl.BlockSpec((NFREQ, BLK), lambda i: (0, 0)),
                  pl.BlockSpec((1, NFREQ, D_B), lambda i: (0, 0, 0)),
                  pl.BlockSpec((1, D_B), lambda i: (0, 0)),
                  pl.BlockSpec((BLK, NFREQ), lambda i: (0, 0))],
        out_specs=pl.BlockSpec((BLK, D_B), lambda i: (i, 0)),
        out_shape=jax.ShapeDtypeStruct((t, D_B), BF16),
        compiler_params=_params("parallel"),
        name="hyena_single",
    )(pb, pb, pb, conv_w, conv_w, conv_w, conv_b, conv_b, conv_b, jnp.asarray(fwd_dft).astype(BF16), khat,
      skip[None, :], jnp.asarray(inv_dft).astype(BF16))


HY_DT = 2 * LANE
ROWG = 16


def _hy_multi_kernel(x0_ref, x1_ref, v_ref, w0_ref, w1_ref, wv_ref, b0_ref, b1_ref, bv_ref, f_ref, kh_ref,
                     skip_ref, g_ref, o_ref, zh_ref, z_ref, x0s_ref, yh_ref, *, nb, row_len):
    dt = o_ref.shape[1]

    def front(blk, carry):
        rows = pl.ds(pl.multiple_of(blk * BLK, BLK), BLK)
        x0, z = _short_conv_gate(x0_ref[rows, :], x1_ref[rows, :], v_ref[rows, :],
                                 (w0_ref, w1_ref, wv_ref), (b0_ref, b1_ref, bv_ref), row_len)
        x0s_ref[rows, :] = x0
        z_ref[rows, :] = z
        zh_ref[blk] = _dot(f_ref[...], z.astype(BF16))
        return carry

    lax.fori_loop(0, nb, front, 0)

    row_in_group = lax.broadcasted_iota(jnp.int32, (ROWG, 1), 0)

    def back(i, carry):
        def row_group(rg, carry2):
            re = pl.ds(pl.multiple_of(rg * ROWG, ROWG), ROWG)
            im = pl.ds(pl.multiple_of(BLK + rg * ROWG, ROWG), ROWG)

            def body(j, acc):
                p, q, r = acc
                kidx = i - j + (nb - 1)
                zr = zh_ref[j, re, :]
                zi = zh_ref[j, im, :]
                kr = kh_ref[kidx, re, :]
                ki = kh_ref[kidx, im, :]
                return (p + zr * kr, q + zi * ki, r + (zr * ki + zi * kr))

            zeros = jnp.zeros((ROWG, dt), F32)
            p, q, r = lax.fori_loop(0, nb, body, (zeros, zeros, zeros), unroll=True)
            row0 = (row_in_group + rg * ROWG) == 0
            yh_ref[re, :] = jnp.where(row0, p, p - q).astype(BF16)
            yh_ref[im, :] = jnp.where(row0, q, r).astype(BF16)
            return carry2

        lax.fori_loop(0, BLK // ROWG, row_group, 0)
        rows = pl.ds(pl.multiple_of(i * BLK, BLK), BLK)
        y = _dot(g_ref[...], yh_ref[...]) + z_ref[rows, :] * skip_ref[...]
        o_ref[rows, :] = (x0s_ref[rows, :] * y).astype(o_ref.dtype)
        return carry

    lax.fori_loop(0, nb, back, 0)


def _hyena_multi(pb, khat, conv_w, conv_b, skip, nseq, nb, row_len):
    t = pb.shape[0]
    seq_len = nb * BLK
    dt = HY_DT
    per = D_B // dt
    fwd_dft, inv_dft, _ = _dft_consts()
    seq = lambda cb: pl.BlockSpec((seq_len, dt), lambda d, b, cb=cb: (b, cb * per + d))
    wcol = lambda r, k: pl.BlockSpec((r, dt), lambda d, b, k=k: (0, k * per + d))
    conv_b = conv_b[None, :]
    return pl.pallas_call(
        functools.partial(_hy_multi_kernel, nb=nb, row_len=row_len),
        grid=(per, nseq),
        in_specs=[seq(OB_X0), seq(OB_X1), seq(OB_V),
                  wcol(3, 0), wcol(3, 1), wcol(3, 2), wcol(1, 0), wcol(1, 1), wcol(1, 2),
                  pl.BlockSpec((NFREQ, BLK), lambda d, b: (0, 0)),
                  pl.BlockSpec((2 * nb - 1, NFREQ, dt), lambda d, b: (0, 0, d), pipeline_mode=pl.Buffered(1)),
                  pl.BlockSpec((1, dt), lambda d, b: (0, d)),
                  pl.BlockSpec((BLK, NFREQ), lambda d, b: (0, 0))],
        out_specs=pl.BlockSpec((seq_len, dt), lambda d, b: (b, d)),
        out_shape=jax.ShapeDtypeStruct((t, D_B), BF16),
        scratch_shapes=[pltpu.VMEM((nb, NFREQ, dt), F32), pltpu.VMEM((seq_len, dt), F32),
                        pltpu.VMEM((seq_len, dt), F32), pltpu.VMEM((NFREQ, dt), BF16)],
        compiler_params=_params("parallel", "parallel"),
        name="hyena_multi",
    )(pb, pb, pb, conv_w, conv_w, conv_w, conv_b, conv_b, conv_b, jnp.asarray(fwd_dft).astype(BF16), khat,
      skip[None, :], jnp.asarray(inv_dft).astype(BF16))


def _layer_norm(y, g, b):
    mu = jnp.mean(y, axis=-1, keepdims=True)
    yc = y - mu
    var = jnp.mean(yc * yc, axis=-1, keepdims=True)
    return yc * lax.rsqrt(var + LN_EPS) * g + b


def _post_kernel(*refs, n_dir):
    (g_ref, ga_ref, gb_ref, hy_ref, x_ref, mod_ref, nw_ref, pa_ref, pb_ref, wo_ref, lg_ref, lb_ref,
     wg_ref, wu_ref, wo2_ref, lg2_ref, lb2_ref, o_ref) = refs[n_dir:]
    o = refs[0][...]
    for d_ref in refs[1:n_dir]:
        o = o + d_ref[...]
    nw = nw_ref[...]
    parts = []
    for h in range(H_A):
        oh = o[:, h * DV:(h + 1) * DV]
        ms = jnp.mean(oh * oh, axis=-1, keepdims=True)
        parts.append(oh * lax.rsqrt(ms + RMS_EPS) * nw)
    oa = jnp.concatenate(parts, axis=1) * g_ref[...]
    a = _dot(oa.astype(BF16), pa_ref[...])
    b = _dot(hy_ref[...].astype(BF16), pb_ref[...])
    merged = ga_ref[...] * a + gb_ref[...] * b
    mix = _dot(merged.astype(BF16), wo_ref[...])
    y = ALPHA * x_ref[...] + mod_ref[0, 2:3, :] * mix
    x1 = _layer_norm(y, lg_ref[...], lb_ref[...])

    h = (x1 * (1.0 + mod_ref[0, 4:5, :]) + mod_ref[0, 3:4, :]).astype(BF16)
    gt = _dot(h, wg_ref[...])
    up = _dot(h, wu_ref[...])
    act = (gt * _sigmoid(gt) * up).astype(BF16)
    ff = _dot(act, wo2_ref[...])
    y2 = ALPHA * x1 + mod_ref[0, 5:6, :] * ff
    o_ref[...] = _layer_norm(y2, lg2_ref[...], lb2_ref[...])


def _post(o_dirs, pb16, o_hy, x, mod3, cond_row, norm_w, pa, pb, wo, ln_g, ln_b,
          ffn_w_in, ffn_w_out, ln2_g, ln2_b):
    tm = 256
    t = x.shape[0]
    n_dir = len(o_dirs)
    once = lambda shape, idx: pl.BlockSpec(shape, idx, pipeline_mode=pl.Buffered(1))
    tok = pl.BlockSpec((tm, D_MODEL), lambda i: (i, 0))
    col = lambda cb: pl.BlockSpec((tm, D_MODEL), lambda i, cb=cb: (i, cb))
    mat = once((D_MODEL, D_MODEL), lambda i: (0, 0))
    vec = pl.BlockSpec((1, D_MODEL), lambda i: (0, 0))
    return pl.pallas_call(
        functools.partial(_post_kernel, n_dir=n_dir),
        grid=(t // tm,),
        in_specs=[tok] * n_dir + [col(OB_G), col(OB_GA), col(OB_GB), tok, tok,
                  pl.BlockSpec((1, N_MOD, D_MODEL), lambda i: (cond_row(i * tm), 0, 0)),
                  pl.BlockSpec((1, DV), lambda i: (0, 0)),
                  mat, mat, mat, vec, vec,
                  once((D_MODEL, D_FF), lambda i: (0, 0)),
                  once((D_MODEL, D_FF), lambda i: (0, 1)),
                  once((D_FF, D_MODEL), lambda i: (0, 0)),
                  vec, vec],
        out_specs=tok,
        out_shape=jax.ShapeDtypeStruct((t, D_MODEL), F32),
        compiler_params=_params("parallel"),
        name="merge_ffn",
    )(*o_dirs, pb16, pb16, pb16, o_hy, x, mod3, norm_w[None, :], pa, pb, wo, ln_g[None, :], ln_b[None, :],
      ffn_w_in, ffn_w_in, ffn_w_out, ln2_g[None, :], ln2_b[None, :])


def _trunk(x, mod3, cond_row, nseq, seq_len, row_len, state, emit_state, w, lb_logits):
    nb = seq_len // BLK
    pf, pb = _inproj(x, mod3, w["w_in"], lb_logits, cond_row)
    hg = _hgrn(pf, pb, state, nseq, nb, emit_state)
    n_dir = 2 if nb > 1 else 1
    khat = _filter_spectra(seq_len, w["filt_w1"], w["filt_b1"], w["filt_w2"], w["filt_b2"],
                           w["filt_w3"], w["filt_b3"], w["filt_freq"], w["filt_w4"])
    if nb == 1:
        o_hy = _hyena_single(pb, khat, w["hy_conv_w"], w["hy_conv_b"], w["hy_skip"], row_len)
    else:
        o_hy = _hyena_multi(pb, khat, w["hy_conv_w"], w["hy_conv_b"], w["hy_skip"], nseq, nb, row_len)
    x2 = _post(hg[:n_dir], pb, o_hy, x, mod3, cond_row, w["hgrn_norm_w"], w["proj_a"], w["proj_b"],
               w["w_out"], w["ln1_g"], w["ln1_b"], w["ffn_w_in"], w["ffn_w_out"], w["ln2_g"], w["ln2_b"])
    return x2, (hg[n_dir] if emit_state else None)


def kernel(x_prompt, x_sample, state_hgrn, c, c_ctx, ada_w, ada_b, w_in, hgrn_lb_logits, hgrn_norm_w,
           hy_conv_w, hy_conv_b, filt_w1, filt_b1, filt_w2, filt_b2, filt_w3, filt_b3, filt_freq, filt_w4,
           hy_skip, proj_a, proj_b, w_out, ln1_g, ln1_b, ffn_w_in, ffn_w_out, ln2_g, ln2_b):
    assert ada_w.shape[0] == DEPTH == 1
    batch, seq, _ = x_prompt.shape
    dec_batch, dec_seq, _ = x_sample.shape
    assert seq % BLK == 0 and dec_seq % BLK == 0 and BLK % GRID_W == 0 and dec_batch + 1 <= 8

    w = dict(w_in=w_in[0].astype(BF16), hy_conv_w=hy_conv_w[0], hy_conv_b=hy_conv_b[0],
             filt_w1=filt_w1[0], filt_b1=filt_b1[0], filt_w2=filt_w2[0], filt_b2=filt_b2[0],
             filt_w3=filt_w3[0], filt_b3=filt_b3[0], filt_freq=filt_freq[0], filt_w4=filt_w4[0],
             hy_skip=hy_skip[0], hgrn_norm_w=hgrn_norm_w[0], proj_a=proj_a[0].astype(BF16),
             proj_b=proj_b[0].astype(BF16), w_out=w_out[0].astype(BF16), ln1_g=ln1_g[0], ln1_b=ln1_b[0],
             ffn_w_in=ffn_w_in[0].astype(BF16), ffn_w_out=ffn_w_out[0].astype(BF16),
             ln2_g=ln2_g[0], ln2_b=ln2_b[0])

    cond8 = jnp.zeros((8, D_MODEL), F32).at[0].set(c_ctx).at[1:1 + dec_batch].set(c)
    mod3 = _modulation(cond8, ada_w[0], ada_b[0][None, :]).reshape(8, N_MOD, D_MODEL)

    xp = x_prompt.reshape(batch * seq, D_MODEL)
    xs = x_sample.reshape(dec_batch * dec_seq, D_MODEL)
    yp, new_state = _trunk(xp, mod3, lambda tok: 0, batch, seq, seq, None, True, w, hgrn_lb_logits)
    ys, _ = _trunk(xs, mod3, lambda tok: 1 + tok // dec_seq, dec_batch, dec_seq, GRID_W, state_hgrn, False,
                   w, hgrn_lb_logits)
    return (yp.reshape(batch, seq, D_MODEL), ys.reshape(dec_batch, dec_seq, D_MODEL), new_state)
```

```python
import functools
import math

import numpy as np
import jax
import jax.numpy as jnp
from jax import lax
from jax.experimental import pallas as pl
from jax.experimental.pallas import tpu as pltpu

F32 = jnp.float32
BF16 = jnp.bfloat16

D_MODEL = 1024
DEPTH = 1
GRID_W = 64
H_A = 8
DK = 128
DV = 128
D_B = 1024
FILT_EMB = 33
FILT_BANDS = 16
FILT_ORDER = 64
DECAY_FAST = 0.3
DECAY_SLOW = 1.5
DECAY_TARGET = 1e-2
DECAY_SHIFT = 0.05
D_FF = 2816
N_MOD = 6
W_IN_COLS = 10 * D_MODEL
ALPHA = (2.0 * DEPTH) ** 0.25
LN_EPS = 1e-5
RMS_EPS = 1e-6

LANE = 128
SUBLANES = 8
BLK = 256
NFREQ = 2 * BLK
CHUNK = 32
NCHUNK = BLK // CHUNK
HPS = 8
PROJ_ROWS = 128
POST_ROWS = 128
HY_SEQS = 2
FILT_BLOCKS = 4
VMEM_LIMIT = 56 * 1024 * 1024

CB_Q, CB_FF, CB_FB, CB_I, CB_G, CB_X0, CB_X1, CB_V, CB_GA, CB_GB = range(10)
STEP_COLS = (CB_FF, CB_FB, CB_Q, CB_G, CB_GA, CB_GB, CB_I, CB_X0, CB_X1, CB_V)
N_F32_COLS = 2
OF_FF, OF_FB = range(N_F32_COLS)
OB_Q, OB_G, OB_GA, OB_GB, OB_I, OB_X0, OB_X1, OB_V = range(len(STEP_COLS) - N_F32_COLS)


def _sigmoid(x):
    return 1.0 / (1.0 + jnp.exp(-x))


def _dot(a, b):
    return jnp.dot(a, b, preferred_element_type=F32)


def _dot_nt(a, b):
    return lax.dot_general(a, b, (((1,), (1,)), ((), ())), preferred_element_type=F32)


def _dot_tn(a, b):
    return lax.dot_general(a, b, (((0,), (0,)), ((), ())), preferred_element_type=F32)


def _dot_hi(a, b):
    return jnp.dot(a, b, preferred_element_type=F32, precision=lax.Precision.HIGHEST)


def _params(*sem):
    return pltpu.CompilerParams(dimension_semantics=sem, vmem_limit_bytes=VMEM_LIMIT)


@functools.lru_cache(maxsize=None)
def _dft_consts():
    n = np.arange(BLK, dtype=np.float64)
    f = np.arange(BLK, dtype=np.float64)
    ang = 2.0 * np.pi * np.outer(f, n) / NFREQ
    fwd = np.zeros((NFREQ, BLK), np.float64)
    fwd[:BLK] = np.cos(ang)
    fwd[BLK + 1:] = -np.sin(ang[1:])
    fwd[BLK] = np.cos(np.pi * n)
    inv = np.zeros((BLK, NFREQ), np.float64)
    scale = np.full((BLK,), 2.0)
    scale[0] = 1.0
    inv[:, :BLK] = np.cos(ang.T) * scale[None, :]
    inv[:, BLK + 1:] = -2.0 * np.sin(ang.T[:, 1:])
    inv[:, BLK] = np.cos(np.pi * n)
    inv /= NFREQ
    fr = np.arange(NFREQ)
    freq_of_row = np.where(fr < BLK, fr, np.where(fr == BLK, BLK, fr - BLK))
    sgn = np.where(freq_of_row % 2 == 0, 1.0, -1.0)[:, None]
    return fwd.astype(np.float32), inv.astype(np.float32), sgn.astype(np.float32)


@functools.lru_cache(maxsize=None)
def _scan_consts():
    t = np.arange(BLK)
    ct = t // CHUNK
    same = ct[:, None] == ct[None, :]
    tri_f = (same & (t[None, :] <= t[:, None])).astype(np.float32)
    tri_b = (same & (t[None, :] >= t[:, None])).astype(np.float32)

    def levels(p, diag):
        x = p[:, None] ^ p[None, :]
        lvl = np.zeros_like(x)
        for bit in range(1, NCHUNK.bit_length()):
            lvl = np.where(x >= (1 << (bit - 1)), bit, lvl)
        lv = np.where(p[:, None] > p[None, :], lvl, -1)
        return np.where(same, np.where(diag, 0, -1), lv).astype(np.int32)

    lv_f = levels(ct, t[None, :] <= t[:, None])
    lv_b = levels(NCHUNK - 1 - ct, t[None, :] >= t[:, None])
    return tri_f, tri_b, lv_f, lv_b


@functools.lru_cache(maxsize=None)
def _filter_positions(seq_len):
    f32 = np.float32
    j = np.arange(-seq_len, seq_len)
    p = np.abs(j)
    valid = (j > -seq_len)
    pc = np.minimum(p, seq_len - 1)
    t = np.linspace(0.0, 1.0, seq_len, dtype=f32)[pc]
    wpos = (f32(2.0 * math.pi / seq_len) * np.arange(seq_len, dtype=f32))[pc]
    bands = np.linspace(1e-4, FILT_BANDS - 1, FILT_BANDS, dtype=f32)
    arg = (bands[None, :] * wpos[:, None]).astype(f32)
    z = np.zeros((2 * seq_len, LANE), f32)
    z[:, 0] = t
    z[:, 1:1 + FILT_BANDS] = np.cos(arg)
    z[:, 1 + FILT_BANDS:FILT_EMB] = -np.sin(arg)
    z[:, FILT_EMB] = valid.astype(f32)
    return z


@functools.lru_cache(maxsize=None)
def _decay_rates():
    max_decay = math.log(DECAY_TARGET) / DECAY_FAST
    min_decay = math.log(DECAY_TARGET) / DECAY_SLOW
    return np.abs(np.linspace(min_decay, max_decay, D_B, dtype=np.float32))[None, :]


def _mod_kernel(c_ref, w_ref, b_ref, o_ref):
    c = c_ref[...]
    s = (c * _sigmoid(c)).astype(BF16)
    o_ref[...] = _dot(s, w_ref[...].astype(BF16)) + b_ref[...]


def _modulation(cond8, ada_w, ada_b):
    tn = 1536
    n = N_MOD * D_MODEL
    return pl.pallas_call(
        _mod_kernel,
        grid=(n // tn,),
        in_specs=[pl.BlockSpec((8, D_MODEL), lambda j: (0, 0)),
                  pl.BlockSpec((D_MODEL, tn), lambda j: (0, j)),
                  pl.BlockSpec((1, tn), lambda j: (0, j))],
        out_specs=pl.BlockSpec((8, tn), lambda j: (0, j)),
        out_shape=jax.ShapeDtypeStruct((8, n), F32),
        compiler_params=_params("parallel"),
        name="modulation",
    )(cond8, ada_w, ada_b)


def _lower_bounds(lbl_ref):
    l0 = lbl_ref[0]
    l1 = lbl_ref[1]
    m = jnp.maximum(l0, l1)
    e0 = jnp.exp(l0 - m)
    e1 = jnp.exp(l1 - m)
    return e0 / (e0 + e1)


def _inproj_kernel(x_ref, mod_ref, w_ref, lbl_ref, of_ref, ob_ref, h_ref):
    j = pl.program_id(1)

    @pl.when(j == 0)
    def _():
        h = x_ref[...] * (1.0 + mod_ref[0, 1:2, :]) + mod_ref[0, 0:1, :]
        h_ref[...] = h.astype(BF16)

    def project(o_ref, act):
        for r in range(x_ref.shape[0] // PROJ_ROWS):
            rows = pl.ds(r * PROJ_ROWS, PROJ_ROWS)
            o_ref[rows, :] = act(_dot(h_ref[rows, :], w_ref[...])).astype(o_ref.dtype)

    silu = lambda a: a * _sigmoid(a)
    step = STEP_COLS.index

    @pl.when(jnp.logical_or(j == step(CB_FF), j == step(CB_FB)))
    def _():
        lb2 = _lower_bounds(lbl_ref)
        lb = jnp.where(j == step(CB_FF), lb2[0:1, :], lb2[1:2, :])
        project(of_ref, lambda a: jnp.log(lb + (1.0 - lb) * _sigmoid(a)))

    @pl.when(jnp.logical_or(j == step(CB_Q), j == step(CB_G)))
    def _():
        project(ob_ref, silu)

    @pl.when(jnp.logical_or(j == step(CB_GA), j == step(CB_GB)))
    def _():
        project(ob_ref, _sigmoid)

    @pl.when(j >= step(CB_I))
    def _():
        project(ob_ref, lambda a: a)


def _weight_col(j):
    col = jnp.int32(STEP_COLS[-1])
    for step in reversed(range(len(STEP_COLS) - 1)):
        col = jnp.where(j == step, STEP_COLS[step], col)
    return col


def _inproj(x, mod3, w_bf, lb_logits, cond_row):
    tm = 2048
    t = x.shape[0]
    n_b16 = len(STEP_COLS) - N_F32_COLS
    return pl.pallas_call(
        _inproj_kernel,
        grid=(t // tm, len(STEP_COLS)),
        in_specs=[pl.BlockSpec((tm, D_MODEL), lambda i, j: (i, 0)),
                  pl.BlockSpec((1, N_MOD, D_MODEL), lambda i, j: (cond_row(i * tm), 0, 0)),
                  pl.BlockSpec((D_MODEL, D_MODEL), lambda i, j: (0, _weight_col(j))),
                  pl.BlockSpec((2, 2, D_MODEL), lambda i, j: (0, 0, 0))],
        out_specs=[pl.BlockSpec((tm, D_MODEL), lambda i, j: (i, jnp.minimum(j, N_F32_COLS - 1))),
                   pl.BlockSpec((tm, D_MODEL), lambda i, j: (i, jnp.maximum(j - N_F32_COLS, 0)))],
        out_shape=[jax.ShapeDtypeStruct((t, N_F32_COLS * D_MODEL), F32),
                   jax.ShapeDtypeStruct((t, n_b16 * D_MODEL), BF16)],
        scratch_shapes=[pltpu.VMEM((tm, D_MODEL), BF16)],
        compiler_params=_params("parallel", "arbitrary"),
        name="inproj",
    )(x, mod3, w_bf, lb_logits)


def _chunk_cumsum(lf, tri):
    lf_hi = lf.astype(BF16)
    lf_lo = (lf - lf_hi.astype(F32)).astype(BF16)
    return _dot(tri, lf_hi) + _dot(tri, lf_lo)


def _wavefront(units):
    waiting, active = list(units), []
    while waiting or active:
        if waiting:
            active.append(waiting.pop(0))
        for unit in list(active):
            try:
                next(unit)
            except StopIteration:
                active.remove(unit)


def _hgrn_direction(load, lv, reverse, use_state, finish):
    q, lf, b, v, st = load()
    k = 1.0 - jnp.exp(lf)
    qe = q.astype(F32) * jnp.exp(b)
    k0 = k * jnp.exp(-b)
    yield

    order = [NCHUNK - 1 - i for i in range(NCHUNK)] if reverse else list(range(NCHUNK))
    chunk_of = {p: i for i, p in enumerate(order)}
    sl = lambda i: slice(i * CHUNK, (i + 1) * CHUNK)
    last_row = lambda i: (i * CHUNK) if reverse else (i * CHUNK + CHUNK - 1)
    c = [None] * NCHUNK
    for p in range(NCHUNK):
        r = last_row(chunk_of[p])
        c[p] = b[r:r + 1, :]
    cum = [jnp.zeros_like(c[0])]
    for p in range(NCHUNK):
        cum.append(cum[p] + c[p])
    total = cum[NCHUNK]

    qe_c, ke_c = {}, {}
    for p in range(NCHUNK):
        i = chunk_of[p]
        qe_c[p] = qe[sl(i), :]
        ke_c[p] = k0[sl(i), :] * jnp.exp(c[p])

    def assemble(parts):
        return jnp.concatenate([parts[order[i]] for i in range(NCHUNK)], axis=0).astype(BF16)

    cph = NCHUNK // 2
    nlev = NCHUNK.bit_length() - 1

    def half_rows(hh):
        first = (1 - hh) if reverse else hh
        return slice(first * cph * CHUNK, (first + 1) * cph * CHUNK)

    def assemble_half(parts, hh):
        ps = range(hh * cph, (hh + 1) * cph)
        return jnp.concatenate([parts[p] for p in (reversed(ps) if reverse else ps)], axis=0).astype(BF16)

    zero = jnp.zeros((CHUNK, LANE), F32)
    qe_bf, k0_bf = qe.astype(BF16), k0.astype(BF16)
    yield
    s_half = [jnp.where(lv == 0, _dot_nt(qe_bf[half_rows(hh), :], k0_bf[half_rows(hh), :]), 0.0)
              for hh in range(2)]
    for lev in range(1, nlev):
        yield
        mid = 1 << (lev - 1)
        qp, kp = {}, {}
        for p in range(NCHUNK):
            pm = ((p >> lev) << lev) + mid
            if p >= pm:
                qp[p] = qe_c[p] * jnp.exp(cum[p] - cum[pm])
                kp[p] = zero
            else:
                qp[p] = zero
                kp[p] = ke_c[p] * jnp.exp(cum[pm] - cum[p + 1])
        for hh in range(2):
            s_lev = _dot_nt(assemble_half(qp, hh), assemble_half(kp, hh))
            s_half[hh] = jnp.where(lv == lev, s_lev, s_half[hh])
    yield
    q_top = assemble_half({p: qe_c[p] * jnp.exp(cum[p] - cum[cph]) for p in range(cph, NCHUNK)}, 1)
    k_top = assemble_half({p: ke_c[p] * jnp.exp(cum[cph] - cum[p + 1]) for p in range(cph)}, 0)
    s_top = _dot_nt(q_top, k_top)
    yield

    v_bf = v.astype(BF16)
    v_half = [v_bf[half_rows(hh), :] for hh in range(2)]
    out_half = [_dot(s_half[0].astype(BF16), v_half[0]),
                _dot(jnp.concatenate([s_top, s_half[1]], axis=1).astype(BF16),
                     jnp.concatenate(v_half, axis=0))]
    out = jnp.concatenate(out_half[::-1] if reverse else out_half, axis=0)
    yield
    if use_state:
        q_start = assemble({p: qe_c[p] * jnp.exp(cum[p]) for p in range(NCHUNK)})
        out = out + _dot_nt(q_start, st.astype(BF16))
    k_end = assemble({p: ke_c[p] * jnp.exp(total - cum[p + 1]) for p in range(NCHUNK)})
    upd = _dot_tn(v_bf, k_end)
    new_st = st * jnp.exp(total) + upd if use_state else upd
    finish(out, new_st)


def _hgrn_kernel(*refs, nb, zero_init, emit_state):
    it = iter(refs)
    qf_ref, lff_ref, vf_ref, qb_ref, lfb_ref, vb_ref = [next(it) for _ in range(6)]
    s0_ref = None if zero_init else next(it)
    lvf_ref, lvb_ref, trif_ref, trib_ref = [next(it) for _ in range(4)]
    of_ref = next(it)
    ob_ref = next(it) if nb > 1 else None
    so_ref = next(it) if emit_state else None
    st_ref = next(it)
    i = pl.program_id(2)

    use_state = not (zero_init and nb == 1)
    if use_state:
        @pl.when(i == 0)
        def _():
            for d in range(2):
                for h in range(HPS):
                    st_ref[d, h] = jnp.zeros((DV, DK), F32) if zero_init else s0_ref[0, 0, d, h].T

    lf_f = lff_ref[...]
    lf_b = lfb_ref[...]
    b_f = _chunk_cumsum(lf_f, trif_ref[...])
    b_b = _chunk_cumsum(lf_b, trib_ref[...])
    lv_f = lvf_ref[...]
    lv_b = lvb_ref[...]
    def unit(d, h):
        hs = slice(h * LANE, (h + 1) * LANE)
        q_ref, lf, b, v_ref, o_ref = ((qf_ref, lf_f, b_f, vf_ref, of_ref), (qb_ref, lf_b, b_b, vb_ref, ob_ref))[d]

        def finish(out, new_st):
            if nb > 1:
                o_ref[:, hs] = out
                st_ref[d, h] = new_st
            elif d == 0:
                of_ref[:, hs] = out
            else:
                of_ref[:, hs] += out
            if emit_state and nb == 1:
                so_ref[0, 0, d, h] = new_st.T
            elif emit_state:
                @pl.when(i == nb - 1)
                def _():
                    so_ref[0, 0, d, h] = new_st.T

        return _hgrn_direction(lambda: (q_ref[:, hs], lf[:, hs], b[:, hs], v_ref[:, hs], st_ref[d, h]),
                               (lv_f, lv_b)[d], d == 1, use_state, finish)

    _wavefront([unit(d, h) for h in range(HPS) for d in range(2)])


def _hgrn(pf, pb, state, nseq, nb, emit_state):
    zero_init = state is None
    t = pf.shape[0]
    tri_f, tri_b, lv_f, lv_b = _scan_consts()
    wid = HPS * LANE
    per = D_MODEL // wid
    fwd = lambda cb: pl.BlockSpec((BLK, wid), lambda b, h, i, cb=cb: (b * nb + i, cb * per + h))
    bwd = lambda cb: pl.BlockSpec((BLK, wid), lambda b, h, i, cb=cb: (b * nb + nb - 1 - i, cb * per + h))
    const = lambda n=BLK: pl.BlockSpec((n, n), lambda b, h, i: (0, 0))
    hl = BLK // 2
    st_spec = pl.BlockSpec((1, 1, 2, HPS, DK, DV), lambda b, h, i: (b, 0, 0, h, 0, 0))
    in_specs = [fwd(OB_Q), fwd(OF_FF), fwd(OB_I), bwd(OB_Q), bwd(OF_FB), bwd(OB_I)]
    args = [pb, pf, pb, pb, pf, pb]
    if not zero_init:
        in_specs.append(st_spec)
        args.append(state)
    in_specs += [const(hl), const(hl), const(), const()]
    args += [jnp.asarray(lv_f[:hl, :hl]), jnp.asarray(lv_b[:hl, :hl]),
             jnp.asarray(tri_f, BF16), jnp.asarray(tri_b, BF16)]
    out_specs = [pl.BlockSpec((BLK, wid), lambda b, h, i: (b * nb + i, h))]
    out_shape = [jax.ShapeDtypeStruct((t, D_MODEL), F32)]
    if nb > 1:
        out_specs.append(pl.BlockSpec((BLK, wid), lambda b, h, i: (b * nb + nb - 1 - i, h)))
        out_shape.append(jax.ShapeDtypeStruct((t, D_MODEL), F32))
    if emit_state:
        out_specs.append(st_spec)
        out_shape.append(jax.ShapeDtypeStruct((nseq, DEPTH, 2, H_A, DK, DV), F32))
    return pl.pallas_call(
        functools.partial(_hgrn_kernel, nb=nb, zero_init=zero_init, emit_state=emit_state),
        grid=(nseq, H_A // HPS, nb),
        in_specs=in_specs,
        out_specs=out_specs,
        out_shape=out_shape,
        scratch_shapes=[pltpu.VMEM((2, HPS, DV, DK), F32)],
        compiler_params=_params("parallel", "parallel", "arbitrary"),
        name="hgrn_scan",
    )(*args)


def _split_bf16(x):
    hi = x.astype(BF16)
    return hi, (x - hi.astype(F32)).astype(BF16)


def _dot3(a, b):
    a_hi, a_lo = _split_bf16(a)
    b_hi, b_lo = _split_bf16(b)
    return _dot(a_hi, b_hi) + (_dot(a_hi, b_lo) + _dot(a_lo, b_hi))


def _filter_kernel(z_ref, zt_ref, w1_ref, b1_ref, w2_ref, b2_ref, w3_ref, b3_ref, fq_ref, w4_ref, dec_ref,
                   fhi_ref, flo_ref, sgn_ref, o_ref, prev_ref, *, blocks):
    pad = jnp.zeros((LANE - FILT_ORDER, BLK), F32)

    def layer(x, w_ref, b_ref, k):
        h = jnp.sin(fq_ref[:, k:k + 1] * (_dot_hi(w_ref[...], x) + b_ref[...]))
        return jnp.concatenate([h, pad], axis=0)

    @pl.when(pl.program_id(0) == 0)
    def _():
        prev_ref[...] = jnp.zeros_like(prev_ref)

    spectra = [prev_ref[...]]

    def tap_block(blk):
        pos = slice(blk * BLK, (blk + 1) * BLK)
        h = layer(zt_ref[:, pos], w1_ref, b1_ref, 0)
        yield
        h = layer(h, w2_ref, b2_ref, 1)
        yield
        h = layer(h, w3_ref, b3_ref, 2)
        yield
        a = _dot3(h.T, w4_ref[...])
        yield
        zp = z_ref[pos, :]
        window = jnp.exp(-zp[:, 0:1] * dec_ref[...]) + DECAY_SHIFT
        a = a * window * zp[:, FILT_EMB:FILT_EMB + 1]
        a_hi, a_lo = _split_bf16(a)
        yield
        f_hi = fhi_ref[...]
        ah = _dot(f_hi, a_hi) + (_dot(f_hi, a_lo) + _dot(flo_ref[...], a_hi))
        yield
        o_ref[blk] = ah + sgn_ref[...] * spectra[blk]
        spectra.append(ah)

    _wavefront([tap_block(blk) for blk in range(blocks)])
    prev_ref[...] = spectra[blocks]


def _filter_spectra(seq_len, w1, b1, w2, b2, w3, b3, freq, w4):
    nb = seq_len // BLK
    fwd_dft, _, sgn = _dft_consts()
    zpos_np = _filter_positions(seq_len)
    zpos = jnp.asarray(zpos_np)
    zpos_t = jnp.asarray(np.ascontiguousarray(zpos_np.T))
    f_hi, f_lo = _split_bf16(jnp.asarray(fwd_dft))
    wt = lambda w: jnp.pad(w, ((0, LANE - w.shape[0]), (0, 0))).T
    colv = lambda b: b[:, None]
    w1p, w2p, w3p = wt(w1), wt(w2), wt(w3)
    b1p, b2p, b3p = colv(b1), colv(b2), colv(b3)
    fqp = freq.T
    w4p = jnp.pad(w4, ((0, LANE - w4.shape[0]), (0, 0)))
    small = lambda shape: pl.BlockSpec(shape, lambda s: (0, 0))
    blocks = math.gcd(nb, FILT_BLOCKS)
    return pl.pallas_call(
        functools.partial(_filter_kernel, blocks=blocks),
        grid=(2 * nb // blocks,),
        in_specs=[pl.BlockSpec((blocks * BLK, LANE), lambda s: (s, 0)),
                  pl.BlockSpec((LANE, blocks * BLK), lambda s: (0, s)),
                  small((FILT_ORDER, LANE)), small((FILT_ORDER, 1)), small((FILT_ORDER, LANE)),
                  small((FILT_ORDER, 1)), small((FILT_ORDER, LANE)), small((FILT_ORDER, 1)),
                  small((FILT_ORDER, 3)),
                  pl.BlockSpec((LANE, D_B), lambda s: (0, jnp.where(s * blocks < nb, 1, 0))),
                  small((1, D_B)), small((NFREQ, BLK)), small((NFREQ, BLK)), small((NFREQ, 1))],
        out_specs=pl.BlockSpec((blocks, NFREQ, D_B), lambda s: (s, 0, 0)),
        out_shape=jax.ShapeDtypeStruct((2 * nb, NFREQ, D_B), F32),
        scratch_shapes=[pltpu.VMEM((NFREQ, D_B), F32)],
        compiler_params=_params("arbitrary"),
        name="hyena_filter",
    )(zpos, zpos_t, w1p, b1p, w2p, b2p, w3p, b3p, fqp, w4p, jnp.asarray(_decay_rates()),
      f_hi, f_lo, jnp.asarray(sgn))


def _short_conv_gate(u0, u1, uv, w_refs, b_refs, row_len):
    sublane = lax.broadcasted_iota(jnp.int32, (SUBLANES, 1), 0)

    def zero_rows(x, offset):
        slabs = []
        for g in range(BLK // SUBLANES):
            slab = x[g * SUBLANES:(g + 1) * SUBLANES, :]
            hit = (offset - g * SUBLANES) % row_len
            if hit < SUBLANES:
                slab = jnp.where(sublane == hit, 0.0, slab)
            slabs.append(slab)
        return jnp.concatenate(slabs, axis=0)

    def conv(u, w_ref, b_ref):
        u = u.astype(F32)
        up = zero_rows(pltpu.roll(u, 1, 0), 0)
        dn = zero_rows(pltpu.roll(u, BLK - 1, 0), row_len - 1)
        return up * w_ref[0:1, :] + u * w_ref[1:2, :] + dn * w_ref[2:3, :] + b_ref[...]

    return (conv(u0, w_refs[0], b_refs[0]),
            conv(uv, w_refs[2], b_refs[2]) * conv(u1, w_refs[1], b_refs[1]))


def _hy_single_kernel(x0_ref, x1_ref, v_ref, w0_ref, w1_ref, wv_ref, b0_ref, b1_ref, bv_ref, f_ref, kh_ref,
                      skip_ref, g_ref, o_ref, *, row_len):
    row0 = lax.broadcasted_iota(jnp.int32, (BLK, 1), 0) == 0

    def sequence(s):
        rows = pl.ds(s * BLK, BLK)
        x0, z = _short_conv_gate(x0_ref[rows, :], x1_ref[rows, :], v_ref[rows, :], (w0_ref, w1_ref, wv_ref),
                                 (b0_ref, b1_ref, bv_ref), row_len)
        z_bf = z.astype(BF16)
        yield
        zh = _dot(f_ref[...], z_bf)
        yield
        zr, zi = zh[:BLK, :], zh[BLK:, :]
        kr, ki = kh_ref[0, :BLK, :], kh_ref[0, BLK:, :]
        p = zr * kr
        q = zi * ki
        r = zr * ki + zi * kr
        yh = jnp.concatenate([jnp.where(row0, p, p - q), jnp.where(row0, q, r)], axis=0).astype(BF16)
        yield
        y = _dot(g_ref[...], yh)
        yield
        o_ref[rows, :] = (x0 * (y + z * skip_ref[...])).astype(o_ref.dtype)

    _wavefront([sequence(s) for s in range(o_ref.shape[0] // BLK)])


def _hyena_single(pb, khat, conv_w, conv_b, skip, row_len):
    t = pb.shape[0]
    fwd_dft, inv_dft, _ = _dft_consts()
    rows = HY_SEQS * BLK
    col = lambda cb: pl.BlockSpec((rows, D_B), lambda i, cb=cb: (i, cb))
    wcol = lambda r, k: pl.BlockSpec((r, D_B), lambda i, k=k: (0, k))
    conv_b = conv_b[None, :]
    return pl.pallas_call(
        functools.partial(_hy_single_kernel, row_len=row_len),
        grid=(t // rows,),
        in_specs=[col(OB_X0), col(OB_X1), col(OB_V),
                  wcol(3, 0), wcol(3, 1), wcol(3, 2), wcol(1, 0), wcol(1, 1), wcol(1, 2),
                  pl.BlockSpec((NFREQ, BLK), lambda i: (0, 0)),
                  pl.BlockSpec((1, NFREQ, D_B), lambda i: (1, 0, 0)),
                  pl.BlockSpec((1, D_B), lambda i: (0, 0)),
                  pl.BlockSpec((BLK, NFREQ), lambda i: (0, 0))],
        out_specs=pl.BlockSpec((rows, D_B), lambda i: (i, 0)),
        out_shape=jax.ShapeDtypeStruct((t, D_B), BF16),
        compiler_params=_params("parallel"),
        name="hyena_single",
    )(pb, pb, pb, conv_w, conv_w, conv_w, conv_b, conv_b, conv_b, jnp.asarray(fwd_dft).astype(BF16), khat,
      skip[None, :], jnp.asarray(inv_dft).astype(BF16))


HY_DT = 2 * LANE
ROWG = 16


def _hy_multi_kernel(x0_ref, x1_ref, v_ref, w0_ref, w1_ref, wv_ref, b0_ref, b1_ref, bv_ref, f_ref, kh_ref,
                     skip_ref, g_ref, o_ref, zh_ref, z_ref, x0s_ref, yh_ref, *, nb, row_len):
    dt = o_ref.shape[1]

    def front(blk, carry):
        rows = pl.ds(pl.multiple_of(blk * BLK, BLK), BLK)
        x0, z = _short_conv_gate(x0_ref[rows, :], x1_ref[rows, :], v_ref[rows, :],
                                 (w0_ref, w1_ref, wv_ref), (b0_ref, b1_ref, bv_ref), row_len)
        x0s_ref[rows, :] = x0
        z_ref[rows, :] = z
        zh_ref[blk] = _dot(f_ref[...], z.astype(BF16))
        return carry

    lax.fori_loop(0, nb, front, 0)

    row_in_group = lax.broadcasted_iota(jnp.int32, (ROWG, 1), 0)

    def back(i, carry):
        def row_group(rg, carry2):
            re = pl.ds(pl.multiple_of(rg * ROWG, ROWG), ROWG)
            im = pl.ds(pl.multiple_of(BLK + rg * ROWG, ROWG), ROWG)

            def body(j, acc):
                p, q, r = acc
                kidx = i - j + nb
                zr = zh_ref[j, re, :]
                zi = zh_ref[j, im, :]
                kr = kh_ref[kidx, re, :]
                ki = kh_ref[kidx, im, :]
                return (p + zr * kr, q + zi * ki, r + (zr * ki + zi * kr))

            zeros = jnp.zeros((ROWG, dt), F32)
            p, q, r = lax.fori_loop(0, nb, body, (zeros, zeros, zeros), unroll=True)
            row0 = (row_in_group + rg * ROWG) == 0
            yh_ref[re, :] = jnp.where(row0, p, p - q).astype(BF16)
            yh_ref[im, :] = jnp.where(row0, q, r).astype(BF16)
            return carry2

        lax.fori_loop(0, BLK // ROWG, row_group, 0)
        rows = pl.ds(pl.multiple_of(i * BLK, BLK), BLK)
        y = _dot(g_ref[...], yh_ref[...]) + z_ref[rows, :] * skip_ref[...]
        o_ref[rows, :] = (x0s_ref[rows, :] * y).astype(o_ref.dtype)
        return carry

    lax.fori_loop(0, nb, back, 0)


def _hyena_multi(pb, khat, conv_w, conv_b, skip, nseq, nb, row_len):
    t = pb.shape[0]
    seq_len = nb * BLK
    dt = HY_DT
    per = D_B // dt
    fwd_dft, inv_dft, _ = _dft_consts()
    seq = lambda cb: pl.BlockSpec((seq_len, dt), lambda d, b, cb=cb: (b, cb * per + d))
    wcol = lambda r, k: pl.BlockSpec((r, dt), lambda d, b, k=k: (0, k * per + d))
    conv_b = conv_b[None, :]
    return pl.pallas_call(
        functools.partial(_hy_multi_kernel, nb=nb, row_len=row_len),
        grid=(per, nseq),
        in_specs=[seq(OB_X0), seq(OB_X1), seq(OB_V),
                  wcol(3, 0), wcol(3, 1), wcol(3, 2), wcol(1, 0), wcol(1, 1), wcol(1, 2),
                  pl.BlockSpec((NFREQ, BLK), lambda d, b: (0, 0)),
                  pl.BlockSpec((2 * nb, NFREQ, dt), lambda d, b: (0, 0, d), pipeline_mode=pl.Buffered(1)),
                  pl.BlockSpec((1, dt), lambda d, b: (0, d)),
                  pl.BlockSpec((BLK, NFREQ), lambda d, b: (0, 0))],
        out_specs=pl.BlockSpec((seq_len, dt), lambda d, b: (b, d)),
        out_shape=jax.ShapeDtypeStruct((t, D_B), BF16),
        scratch_shapes=[pltpu.VMEM((nb, NFREQ, dt), F32), pltpu.VMEM((seq_len, dt), F32),
                        pltpu.VMEM((seq_len, dt), F32), pltpu.VMEM((NFREQ, dt), BF16)],
        compiler_params=_params("parallel", "parallel"),
        name="hyena_multi",
    )(pb, pb, pb, conv_w, conv_w, conv_w, conv_b, conv_b, conv_b, jnp.asarray(fwd_dft).astype(BF16), khat,
      skip[None, :], jnp.asarray(inv_dft).astype(BF16))


def _layer_norm(y, g, b):
    mu = jnp.mean(y, axis=-1, keepdims=True)
    yc = y - mu
    var = jnp.mean(yc * yc, axis=-1, keepdims=True)
    return yc * lax.rsqrt(var + LN_EPS) * g + b


def _post_kernel(*refs, n_dir):
    (g_ref, ga_ref, gb_ref, hy_ref, x_ref, mod_ref, nw_ref, pa_ref, pb_ref, wo_ref, lg_ref, lb_ref,
     wg_ref, wu_ref, wo2_ref, lg2_ref, lb2_ref, o_ref) = refs[n_dir:]
    nw = nw_ref[...]

    def row_group(r):
        rows = pl.ds(r * POST_ROWS, POST_ROWS)
        o = refs[0][rows, :]
        for d_ref in refs[1:n_dir]:
            o = o + d_ref[rows, :]
        parts = []
        for h in range(H_A):
            oh = o[:, h * DV:(h + 1) * DV]
            ms = jnp.mean(oh * oh, axis=-1, keepdims=True)
            parts.append(oh * lax.rsqrt(ms + RMS_EPS) * nw)
        oa = (jnp.concatenate(parts, axis=1) * g_ref[rows, :]).astype(BF16)
        yield
        a = _dot(oa, pa_ref[...])
        b = _dot(hy_ref[rows, :].astype(BF16), pb_ref[...])
        yield
        merged = (ga_ref[rows, :] * a + gb_ref[rows, :] * b).astype(BF16)
        yield
        mix = _dot(merged, wo_ref[...])
        yield
        y = ALPHA * x_ref[rows, :] + mod_ref[0, 2:3, :] * mix
        x1 = _layer_norm(y, lg_ref[...], lb_ref[...])
        h = (x1 * (1.0 + mod_ref[0, 4:5, :]) + mod_ref[0, 3:4, :]).astype(BF16)
        yield
        gt = _dot(h, wg_ref[...])
        up = _dot(h, wu_ref[...])
        yield
        act = (gt * _sigmoid(gt) * up).astype(BF16)
        yield
        ff = _dot(act, wo2_ref[...])
        yield
        y2 = ALPHA * x1 + mod_ref[0, 5:6, :] * ff
        o_ref[rows, :] = _layer_norm(y2, lg2_ref[...], lb2_ref[...])

    _wavefront([row_group(r) for r in range(x_ref.shape[0] // POST_ROWS)])


def _post(o_dirs, pb16, o_hy, x, mod3, cond_row, norm_w, pa, pb, wo, ln_g, ln_b,
          ffn_w_in, ffn_w_out, ln2_g, ln2_b):
    tm = 256
    t = x.shape[0]
    n_dir = len(o_dirs)
    once = lambda shape, idx: pl.BlockSpec(shape, idx, pipeline_mode=pl.Buffered(1))
    tok = pl.BlockSpec((tm, D_MODEL), lambda i: (i, 0))
    col = lambda cb: pl.BlockSpec((tm, D_MODEL), lambda i, cb=cb: (i, cb))
    mat = once((D_MODEL, D_MODEL), lambda i: (0, 0))
    vec = pl.BlockSpec((1, D_MODEL), lambda i: (0, 0))
    return pl.pallas_call(
        functools.partial(_post_kernel, n_dir=n_dir),
        grid=(t // tm,),
        in_specs=[tok] * n_dir + [col(OB_G), col(OB_GA), col(OB_GB), tok, tok,
                  pl.BlockSpec((1, N_MOD, D_MODEL), lambda i: (cond_row(i * tm), 0, 0)),
                  pl.BlockSpec((1, DV), lambda i: (0, 0)),
                  mat, mat, mat, vec, vec,
                  once((D_MODEL, D_FF), lambda i: (0, 0)),
                  once((D_MODEL, D_FF), lambda i: (0, 1)),
                  once((D_FF, D_MODEL), lambda i: (0, 0)),
                  vec, vec],
        out_specs=tok,
        out_shape=jax.ShapeDtypeStruct((t, D_MODEL), F32),
        compiler_params=_params("parallel"),
        name="merge_ffn",
    )(*o_dirs, pb16, pb16, pb16, o_hy, x, mod3, norm_w[None, :], pa, pb, wo, ln_g[None, :], ln_b[None, :],
      ffn_w_in, ffn_w_in, ffn_w_out, ln2_g[None, :], ln2_b[None, :])


def _trunk(x, mod3, cond_row, nseq, seq_len, row_len, state, emit_state, w, lb_logits):
    nb = seq_len // BLK
    pf, pb = _inproj(x, mod3, w["w_in"], lb_logits, cond_row)
    hg = _hgrn(pf, pb, state, nseq, nb, emit_state)
    n_dir = 2 if nb > 1 else 1
    khat = _filter_spectra(seq_len, w["filt_w1"], w["filt_b1"], w["filt_w2"], w["filt_b2"],
                           w["filt_w3"], w["filt_b3"], w["filt_freq"], w["filt_w4"])
    if nb == 1:
        o_hy = _hyena_single(pb, khat, w["hy_conv_w"], w["hy_conv_b"], w["hy_skip"], row_len)
    else:
        o_hy = _hyena_multi(pb, khat, w["hy_conv_w"], w["hy_conv_b"], w["hy_skip"], nseq, nb, row_len)
    x2 = _post(hg[:n_dir], pb, o_hy, x, mod3, cond_row, w["hgrn_norm_w"], w["proj_a"], w["proj_b"],
               w["w_out"], w["ln1_g"], w["ln1_b"], w["ffn_w_in"], w["ffn_w_out"], w["ln2_g"], w["ln2_b"])
    return x2, (hg[n_dir] if emit_state else None)


def kernel(x_prompt, x_sample, state_hgrn, c, c_ctx, ada_w, ada_b, w_in, hgrn_lb_logits, hgrn_norm_w,
           hy_conv_w, hy_conv_b, filt_w1, filt_b1, filt_w2, filt_b2, filt_w3, filt_b3, filt_freq, filt_w4,
           hy_skip, proj_a, proj_b, w_out, ln1_g, ln1_b, ffn_w_in, ffn_w_out, ln2_g, ln2_b):
    assert ada_w.shape[0] == DEPTH == 1
    batch, seq, _ = x_prompt.shape
    dec_batch, dec_seq, _ = x_sample.shape
    assert seq % BLK == 0 and dec_seq % BLK == 0 and BLK % GRID_W == 0 and dec_batch + 1 <= 8

    w = dict(w_in=w_in[0].astype(BF16), hy_conv_w=hy_conv_w[0], hy_conv_b=hy_conv_b[0],
             filt_w1=filt_w1[0], filt_b1=filt_b1[0], filt_w2=filt_w2[0], filt_b2=filt_b2[0],
             filt_w3=filt_w3[0], filt_b3=filt_b3[0], filt_freq=filt_freq[0], filt_w4=filt_w4[0],
             hy_skip=hy_skip[0], hgrn_norm_w=hgrn_norm_w[0], proj_a=proj_a[0].astype(BF16),
             proj_b=proj_b[0].astype(BF16), w_out=w_out[0].astype(BF16), ln1_g=ln1_g[0], ln1_b=ln1_b[0],
             ffn_w_in=ffn_w_in[0].astype(BF16), ffn_w_out=ffn_w_out[0].astype(BF16),
             ln2_g=ln2_g[0], ln2_b=ln2_b[0])

    cond8 = jnp.zeros((8, D_MODEL), F32).at[0].set(c_ctx).at[1:1 + dec_batch].set(c)
    mod3 = _modulation(cond8, ada_w[0], ada_b[0][None, :]).reshape(8, N_MOD, D_MODEL)

    xp = x_prompt.reshape(batch * seq, D_MODEL)
    xs = x_sample.reshape(dec_batch * dec_seq, D_MODEL)
    yp, new_state = _trunk(xp, mod3, lambda tok: 0, batch, seq, seq, None, True, w, hgrn_lb_logits)
    ys, _ = _trunk(xs, mod3, lambda tok: 1 + tok // dec_seq, dec_batch, dec_seq, GRID_W, state_hgrn, False,
                   w, hgrn_lb_logits)
    return (yp.reshape(batch, seq, D_MODEL), ys.reshape(dec_batch, dec_seq, D_MODEL), new_state)
```

```python
import functools
import math

import numpy as np
import jax
import jax.numpy as jnp
from jax import lax
from jax.experimental import pallas as pl
from jax.experimental.pallas import tpu as pltpu

F32 = jnp.float32
BF16 = jnp.bfloat16

D_MODEL = 1024
DEPTH = 1
GRID_W = 64
H_A = 8
DK = 128
DV = 128
D_B = 1024
FILT_EMB = 33
FILT_BANDS = 16
FILT_ORDER = 64
DECAY_FAST = 0.3
DECAY_SLOW = 1.5
DECAY_TARGET = 1e-2
DECAY_SHIFT = 0.05
D_FF = 2816
N_MOD = 6
W_IN_COLS = 10 * D_MODEL
ALPHA = (2.0 * DEPTH) ** 0.25
LN_EPS = 1e-5
RMS_EPS = 1e-6

LANE = 128
SUBLANES = 8
BLK = 256
NFREQ = 2 * BLK
CHUNK = 32
NCHUNK = BLK // CHUNK
HPS = 8
PROJ_ROWS = 128
POST_ROWS = 128
HY_SEQS = 1
FILT_BLOCKS = 8
VMEM_LIMIT = 56 * 1024 * 1024

CB_Q, CB_FF, CB_FB, CB_I, CB_G, CB_X0, CB_X1, CB_V, CB_GA, CB_GB = range(10)
STEP_COLS = (CB_FF, CB_FB, CB_Q, CB_G, CB_GA, CB_GB, CB_I, CB_X0, CB_X1, CB_V)
N_F32_COLS = 2
OF_FF, OF_FB = range(N_F32_COLS)
OB_Q, OB_G, OB_GA, OB_GB, OB_I, OB_X0, OB_X1, OB_V = range(len(STEP_COLS) - N_F32_COLS)


def _sigmoid(x):
    return 1.0 / (1.0 + jnp.exp(-x))


def _dot(a, b):
    return jnp.dot(a, b, preferred_element_type=F32)


def _dot_nt(a, b):
    return lax.dot_general(a, b, (((1,), (1,)), ((), ())), preferred_element_type=F32)


def _dot_tn(a, b):
    return lax.dot_general(a, b, (((0,), (0,)), ((), ())), preferred_element_type=F32)


def _dot_hi(a, b):
    return jnp.dot(a, b, preferred_element_type=F32, precision=lax.Precision.HIGHEST)


def _params(*sem):
    return pltpu.CompilerParams(dimension_semantics=sem, vmem_limit_bytes=VMEM_LIMIT)


@functools.lru_cache(maxsize=None)
def _dft_consts():
    n = np.arange(BLK, dtype=np.float64)
    f = np.arange(BLK, dtype=np.float64)
    ang = 2.0 * np.pi * np.outer(f, n) / NFREQ
    fwd = np.zeros((NFREQ, BLK), np.float64)
    fwd[:BLK] = np.cos(ang)
    fwd[BLK + 1:] = -np.sin(ang[1:])
    fwd[BLK] = np.cos(np.pi * n)
    inv = np.zeros((BLK, NFREQ), np.float64)
    scale = np.full((BLK,), 2.0)
    scale[0] = 1.0
    inv[:, :BLK] = np.cos(ang.T) * scale[None, :]
    inv[:, BLK + 1:] = -2.0 * np.sin(ang.T[:, 1:])
    inv[:, BLK] = np.cos(np.pi * n)
    inv /= NFREQ
    fr = np.arange(NFREQ)
    freq_of_row = np.where(fr < BLK, fr, np.where(fr == BLK, BLK, fr - BLK))
    sgn = np.where(freq_of_row % 2 == 0, 1.0, -1.0)[:, None]
    return fwd.astype(np.float32), inv.astype(np.float32), sgn.astype(np.float32)


@functools.lru_cache(maxsize=None)
def _scan_consts():
    t = np.arange(BLK)
    ct = t // CHUNK
    same = ct[:, None] == ct[None, :]
    tri_f = (same & (t[None, :] <= t[:, None])).astype(np.float32)
    tri_b = (same & (t[None, :] >= t[:, None])).astype(np.float32)

    def levels(p, diag):
        x = p[:, None] ^ p[None, :]
        lvl = np.zeros_like(x)
        for bit in range(1, NCHUNK.bit_length()):
            lvl = np.where(x >= (1 << (bit - 1)), bit, lvl)
        lv = np.where(p[:, None] > p[None, :], lvl, -1)
        return np.where(same, np.where(diag, 0, -1), lv).astype(np.int32)

    lv_f = levels(ct, t[None, :] <= t[:, None])
    lv_b = levels(NCHUNK - 1 - ct, t[None, :] >= t[:, None])
    return tri_f, tri_b, lv_f, lv_b


@functools.lru_cache(maxsize=None)
def _filter_positions(seq_len):
    f32 = np.float32
    j = np.arange(-seq_len, seq_len)
    p = np.abs(j)
    valid = (j > -seq_len)
    pc = np.minimum(p, seq_len - 1)
    t = np.linspace(0.0, 1.0, seq_len, dtype=f32)[pc]
    wpos = (f32(2.0 * math.pi / seq_len) * np.arange(seq_len, dtype=f32))[pc]
    bands = np.linspace(1e-4, FILT_BANDS - 1, FILT_BANDS, dtype=f32)
    arg = (bands[None, :] * wpos[:, None]).astype(f32)
    z = np.zeros((2 * seq_len, LANE), f32)
    z[:, 0] = t
    z[:, 1:1 + FILT_BANDS] = np.cos(arg)
    z[:, 1 + FILT_BANDS:FILT_EMB] = -np.sin(arg)
    z[:, FILT_EMB] = valid.astype(f32)
    return z


@functools.lru_cache(maxsize=None)
def _decay_rates():
    max_decay = math.log(DECAY_TARGET) / DECAY_FAST
    min_decay = math.log(DECAY_TARGET) / DECAY_SLOW
    return np.abs(np.linspace(min_decay, max_decay, D_B, dtype=np.float32))[None, :]


def _mod_kernel(c_ref, w_ref, b_ref, o_ref):
    c = c_ref[...]
    s = (c * _sigmoid(c)).astype(BF16)
    o_ref[...] = _dot(s, w_ref[...].astype(BF16)) + b_ref[...]


def _modulation(cond8, ada_w, ada_b):
    tn = 1536
    n = N_MOD * D_MODEL
    return pl.pallas_call(
        _mod_kernel,
        grid=(n // tn,),
        in_specs=[pl.BlockSpec((8, D_MODEL), lambda j: (0, 0)),
                  pl.BlockSpec((D_MODEL, tn), lambda j: (0, j)),
                  pl.BlockSpec((1, tn), lambda j: (0, j))],
        out_specs=pl.BlockSpec((8, tn), lambda j: (0, j)),
        out_shape=jax.ShapeDtypeStruct((8, n), F32),
        compiler_params=_params("parallel"),
        name="modulation",
    )(cond8, ada_w, ada_b)


def _lower_bounds(lbl_ref):
    l0 = lbl_ref[0]
    l1 = lbl_ref[1]
    m = jnp.maximum(l0, l1)
    e0 = jnp.exp(l0 - m)
    e1 = jnp.exp(l1 - m)
    return e0 / (e0 + e1)


def _inproj_kernel(x_ref, mod_ref, w_ref, lbl_ref, of_ref, ob_ref, h_ref):
    j = pl.program_id(1)

    @pl.when(j == 0)
    def _():
        h = x_ref[...] * (1.0 + mod_ref[0, 1:2, :]) + mod_ref[0, 0:1, :]
        h_ref[...] = h.astype(BF16)

    def project(o_ref, act):
        for r in range(x_ref.shape[0] // PROJ_ROWS):
            rows = pl.ds(r * PROJ_ROWS, PROJ_ROWS)
            o_ref[rows, :] = act(_dot(h_ref[rows, :], w_ref[...])).astype(o_ref.dtype)

    silu = lambda a: a * _sigmoid(a)
    step = STEP_COLS.index

    @pl.when(jnp.logical_or(j == step(CB_FF), j == step(CB_FB)))
    def _():
        lb2 = _lower_bounds(lbl_ref)
        lb = jnp.where(j == step(CB_FF), lb2[0:1, :], lb2[1:2, :])
        project(of_ref, lambda a: jnp.log(lb + (1.0 - lb) * _sigmoid(a)))

    @pl.when(jnp.logical_or(j == step(CB_Q), j == step(CB_G)))
    def _():
        project(ob_ref, silu)

    @pl.when(jnp.logical_or(j == step(CB_GA), j == step(CB_GB)))
    def _():
        project(ob_ref, _sigmoid)

    @pl.when(j >= step(CB_I))
    def _():
        project(ob_ref, lambda a: a)


def _weight_col(j):
    col = jnp.int32(STEP_COLS[-1])
    for step in reversed(range(len(STEP_COLS) - 1)):
        col = jnp.where(j == step, STEP_COLS[step], col)
    return col


def _inproj(x, mod3, w_bf, lb_logits, cond_row):
    tm = 2048
    t = x.shape[0]
    n_b16 = len(STEP_COLS) - N_F32_COLS
    return pl.pallas_call(
        _inproj_kernel,
        grid=(t // tm, len(STEP_COLS)),
        in_specs=[pl.BlockSpec((tm, D_MODEL), lambda i, j: (i, 0)),
                  pl.BlockSpec((1, N_MOD, D_MODEL), lambda i, j: (cond_row(i * tm), 0, 0)),
                  pl.BlockSpec((D_MODEL, D_MODEL), lambda i, j: (0, _weight_col(j))),
                  pl.BlockSpec((2, 2, D_MODEL), lambda i, j: (0, 0, 0))],
        out_specs=[pl.BlockSpec((tm, D_MODEL), lambda i, j: (i, jnp.minimum(j, N_F32_COLS - 1))),
                   pl.BlockSpec((tm, D_MODEL), lambda i, j: (i, jnp.maximum(j - N_F32_COLS, 0)))],
        out_shape=[jax.ShapeDtypeStruct((t, N_F32_COLS * D_MODEL), F32),
                   jax.ShapeDtypeStruct((t, n_b16 * D_MODEL), BF16)],
        scratch_shapes=[pltpu.VMEM((tm, D_MODEL), BF16)],
        compiler_params=_params("parallel", "arbitrary"),
        name="inproj",
    )(x, mod3, w_bf, lb_logits)


def _chunk_cumsum(lf, tri):
    lf_hi = lf.astype(BF16)
    lf_lo = (lf - lf_hi.astype(F32)).astype(BF16)
    return _dot(tri, lf_hi) + _dot(tri, lf_lo)


def _wavefront(units, newest_first=False):
    waiting, active = list(units), []
    while waiting or active:
        if waiting:
            active.append(waiting.pop(0))
        for unit in (reversed(list(active)) if newest_first else list(active)):
            try:
                next(unit)
            except StopIteration:
                active.remove(unit)


def _hgrn_direction(load, lv, reverse, use_state, finish):
    q, lf, b, v, st = load()
    k = 1.0 - jnp.exp(lf)
    qe = q.astype(F32) * jnp.exp(b)
    k0 = k * jnp.exp(-b)
    yield

    order = [NCHUNK - 1 - i for i in range(NCHUNK)] if reverse else list(range(NCHUNK))
    chunk_of = {p: i for i, p in enumerate(order)}
    sl = lambda i: slice(i * CHUNK, (i + 1) * CHUNK)
    last_row = lambda i: (i * CHUNK) if reverse else (i * CHUNK + CHUNK - 1)
    c = [None] * NCHUNK
    for p in range(NCHUNK):
        r = last_row(chunk_of[p])
        c[p] = b[r:r + 1, :]
    cum = [jnp.zeros_like(c[0])]
    for p in range(NCHUNK):
        cum.append(cum[p] + c[p])
    total = cum[NCHUNK]

    qe_c, ke_c = {}, {}
    for p in range(NCHUNK):
        i = chunk_of[p]
        qe_c[p] = qe[sl(i), :]
        ke_c[p] = k0[sl(i), :] * jnp.exp(c[p])

    def assemble(parts):
        return jnp.concatenate([parts[order[i]] for i in range(NCHUNK)], axis=0).astype(BF16)

    cph = NCHUNK // 2
    nlev = NCHUNK.bit_length() - 1

    def half_rows(hh):
        first = (1 - hh) if reverse else hh
        return slice(first * cph * CHUNK, (first + 1) * cph * CHUNK)

    def assemble_half(parts, hh):
        ps = range(hh * cph, (hh + 1) * cph)
        return jnp.concatenate([parts[p] for p in (reversed(ps) if reverse else ps)], axis=0).astype(BF16)

    zero = jnp.zeros((CHUNK, LANE), F32)
    qe_bf, k0_bf = qe.astype(BF16), k0.astype(BF16)
    yield
    s_half = [jnp.where(lv == 0, _dot_nt(qe_bf[half_rows(hh), :], k0_bf[half_rows(hh), :]), 0.0)
              for hh in range(2)]
    for lev in range(1, nlev):
        yield
        mid = 1 << (lev - 1)
        qp, kp = {}, {}
        for p in range(NCHUNK):
            pm = ((p >> lev) << lev) + mid
            if p >= pm:
                qp[p] = qe_c[p] * jnp.exp(cum[p] - cum[pm])
                kp[p] = zero
            else:
                qp[p] = zero
                kp[p] = ke_c[p] * jnp.exp(cum[pm] - cum[p + 1])
        for hh in range(2):
            s_lev = _dot_nt(assemble_half(qp, hh), assemble_half(kp, hh))
            s_half[hh] = jnp.where(lv == lev, s_lev, s_half[hh])
    yield
    q_top = assemble_half({p: qe_c[p] * jnp.exp(cum[p] - cum[cph]) for p in range(cph, NCHUNK)}, 1)
    k_top = assemble_half({p: ke_c[p] * jnp.exp(cum[cph] - cum[p + 1]) for p in range(cph)}, 0)
    s_top = _dot_nt(q_top, k_top)
    yield

    v_bf = v.astype(BF16)
    v_half = [v_bf[half_rows(hh), :] for hh in range(2)]
    out_half = [_dot(s_half[0].astype(BF16), v_half[0]),
                _dot(jnp.concatenate([s_top, s_half[1]], axis=1).astype(BF16),
                     jnp.concatenate(v_half, axis=0))]
    out = jnp.concatenate(out_half[::-1] if reverse else out_half, axis=0)
    yield
    if use_state:
        q_start = assemble({p: qe_c[p] * jnp.exp(cum[p]) for p in range(NCHUNK)})
        out = out + _dot_nt(q_start, st.astype(BF16))
    k_end = assemble({p: ke_c[p] * jnp.exp(total - cum[p + 1]) for p in range(NCHUNK)})
    upd = _dot_tn(v_bf, k_end)
    new_st = st * jnp.exp(total) + upd if use_state else upd
    finish(out, new_st)


def _hgrn_kernel(*refs, nb, zero_init, emit_state):
    it = iter(refs)
    qf_ref, lff_ref, vf_ref, qb_ref, lfb_ref, vb_ref = [next(it) for _ in range(6)]
    s0_ref = None if zero_init else next(it)
    lvf_ref, lvb_ref, trif_ref, trib_ref = [next(it) for _ in range(4)]
    of_ref = next(it)
    ob_ref = next(it) if nb > 1 else None
    so_ref = next(it) if emit_state else None
    st_ref = next(it)
    i = pl.program_id(2)

    use_state = not (zero_init and nb == 1)
    if use_state:
        @pl.when(i == 0)
        def _():
            for d in range(2):
                for h in range(HPS):
                    st_ref[d, h] = jnp.zeros((DV, DK), F32) if zero_init else s0_ref[0, 0, d, h].T

    lf_f = lff_ref[...]
    lf_b = lfb_ref[...]
    b_f = _chunk_cumsum(lf_f, trif_ref[...])
    b_b = _chunk_cumsum(lf_b, trib_ref[...])
    lv_f = lvf_ref[...]
    lv_b = lvb_ref[...]
    def unit(d, h):
        hs = slice(h * LANE, (h + 1) * LANE)
        q_ref, lf, b, v_ref, o_ref = ((qf_ref, lf_f, b_f, vf_ref, of_ref), (qb_ref, lf_b, b_b, vb_ref, ob_ref))[d]

        def finish(out, new_st):
            if nb > 1:
                o_ref[:, hs] = out
                st_ref[d, h] = new_st
            elif d == 0:
                of_ref[:, hs] = out
            else:
                of_ref[:, hs] += out
            if emit_state and nb == 1:
                so_ref[0, 0, d, h] = new_st.T
            elif emit_state:
                @pl.when(i == nb - 1)
                def _():
                    so_ref[0, 0, d, h] = new_st.T

        return _hgrn_direction(lambda: (q_ref[:, hs], lf[:, hs], b[:, hs], v_ref[:, hs], st_ref[d, h]),
                               (lv_f, lv_b)[d], d == 1, use_state, finish)

    _wavefront([unit(d, h) for d in range(2) for h in range(HPS)], newest_first=use_state)


def _hgrn(pf, pb, state, nseq, nb, emit_state):
    zero_init = state is None
    t = pf.shape[0]
    tri_f, tri_b, lv_f, lv_b = _scan_consts()
    wid = HPS * LANE
    per = D_MODEL // wid
    fwd = lambda cb: pl.BlockSpec((BLK, wid), lambda b, h, i, cb=cb: (b * nb + i, cb * per + h))
    bwd = lambda cb: pl.BlockSpec((BLK, wid), lambda b, h, i, cb=cb: (b * nb + nb - 1 - i, cb * per + h))
    const = lambda n=BLK: pl.BlockSpec((n, n), lambda b, h, i: (0, 0))
    hl = BLK // 2
    st_spec = pl.BlockSpec((1, 1, 2, HPS, DK, DV), lambda b, h, i: (b, 0, 0, h, 0, 0))
    in_specs = [fwd(OB_Q), fwd(OF_FF), fwd(OB_I), bwd(OB_Q), bwd(OF_FB), bwd(OB_I)]
    args = [pb, pf, pb, pb, pf, pb]
    if not zero_init:
        in_specs.append(st_spec)
        args.append(state)
    in_specs += [const(hl), const(hl), const(), const()]
    args += [jnp.asarray(lv_f[:hl, :hl]), jnp.asarray(lv_b[:hl, :hl]),
             jnp.asarray(tri_f, BF16), jnp.asarray(tri_b, BF16)]
    out_specs = [pl.BlockSpec((BLK, wid), lambda b, h, i: (b * nb + i, h))]
    out_shape = [jax.ShapeDtypeStruct((t, D_MODEL), F32)]
    if nb > 1:
        out_specs.append(pl.BlockSpec((BLK, wid), lambda b, h, i: (b * nb + nb - 1 - i, h)))
        out_shape.append(jax.ShapeDtypeStruct((t, D_MODEL), F32))
    if emit_state:
        out_specs.append(st_spec)
        out_shape.append(jax.ShapeDtypeStruct((nseq, DEPTH, 2, H_A, DK, DV), F32))
    return pl.pallas_call(
        functools.partial(_hgrn_kernel, nb=nb, zero_init=zero_init, emit_state=emit_state),
        grid=(nseq, H_A // HPS, nb),
        in_specs=in_specs,
        out_specs=out_specs,
        out_shape=out_shape,
        scratch_shapes=[pltpu.VMEM((2, HPS, DV, DK), F32)],
        compiler_params=_params("parallel", "parallel", "arbitrary"),
        name="hgrn_scan",
    )(*args)


def _split_bf16(x):
    hi = x.astype(BF16)
    return hi, (x - hi.astype(F32)).astype(BF16)


def _dot3(a, b):
    a_hi, a_lo = _split_bf16(a)
    b_hi, b_lo = _split_bf16(b)
    return _dot(a_hi, b_hi) + (_dot(a_hi, b_lo) + _dot(a_lo, b_hi))


def _filter_kernel(z_ref, zt_ref, w1_ref, b1_ref, w2_ref, b2_ref, w3_ref, b3_ref, fq_ref, w4_ref, dec_ref,
                   fhi_ref, flo_ref, sgn_ref, o_ref, prev_ref, *, blocks):
    pad = jnp.zeros((LANE - FILT_ORDER, BLK), F32)

    def layer(x, w_ref, b_ref, k):
        h = jnp.sin(fq_ref[:, k:k + 1] * (_dot_hi(w_ref[...], x) + b_ref[...]))
        return jnp.concatenate([h, pad], axis=0)

    @pl.when(pl.program_id(0) == 0)
    def _():
        prev_ref[...] = jnp.zeros_like(prev_ref)

    spectra = [prev_ref[...]]

    def tap_block(blk):
        pos = slice(blk * BLK, (blk + 1) * BLK)
        h = layer(zt_ref[:, pos], w1_ref, b1_ref, 0)
        yield
        h = layer(h, w2_ref, b2_ref, 1)
        yield
        h = layer(h, w3_ref, b3_ref, 2)
        yield
        a = _dot3(h.T, w4_ref[...])
        yield
        zp = z_ref[pos, :]
        window = jnp.exp(-zp[:, 0:1] * dec_ref[...]) + DECAY_SHIFT
        a = a * window * zp[:, FILT_EMB:FILT_EMB + 1]
        a_hi, a_lo = _split_bf16(a)
        yield
        f_hi = fhi_ref[...]
        ah = _dot(f_hi, a_hi) + (_dot(f_hi, a_lo) + _dot(flo_ref[...], a_hi))
        yield
        o_ref[blk] = ah + sgn_ref[...] * spectra[blk]
        spectra.append(ah)

    _wavefront([tap_block(blk) for blk in range(blocks)], newest_first=True)
    prev_ref[...] = spectra[blocks]


def _filter_spectra(seq_len, w1, b1, w2, b2, w3, b3, freq, w4):
    nb = seq_len // BLK
    fwd_dft, _, sgn = _dft_consts()
    zpos_np = _filter_positions(seq_len)
    zpos = jnp.asarray(zpos_np)
    zpos_t = jnp.asarray(np.ascontiguousarray(zpos_np.T))
    f_hi, f_lo = _split_bf16(jnp.asarray(fwd_dft))
    wt = lambda w: jnp.pad(w, ((0, LANE - w.shape[0]), (0, 0))).T
    colv = lambda b: b[:, None]
    w1p, w2p, w3p = wt(w1), wt(w2), wt(w3)
    b1p, b2p, b3p = colv(b1), colv(b2), colv(b3)
    fqp = freq.T
    w4p = jnp.pad(w4, ((0, LANE - w4.shape[0]), (0, 0)))
    small = lambda shape: pl.BlockSpec(shape, lambda s: (0, 0))
    blocks = math.gcd(nb, FILT_BLOCKS)
    return pl.pallas_call(
        functools.partial(_filter_kernel, blocks=blocks),
        grid=(2 * nb // blocks,),
        in_specs=[pl.BlockSpec((blocks * BLK, LANE), lambda s: (s, 0)),
                  pl.BlockSpec((LANE, blocks * BLK), lambda s: (0, s)),
                  small((FILT_ORDER, LANE)), small((FILT_ORDER, 1)), small((FILT_ORDER, LANE)),
                  small((FILT_ORDER, 1)), small((FILT_ORDER, LANE)), small((FILT_ORDER, 1)),
                  small((FILT_ORDER, 3)),
                  pl.BlockSpec((LANE, D_B), lambda s: (0, jnp.where(s * blocks < nb, 1, 0))),
                  small((1, D_B)), small((NFREQ, BLK)), small((NFREQ, BLK)), small((NFREQ, 1))],
        out_specs=pl.BlockSpec((blocks, NFREQ, D_B), lambda s: (s, 0, 0)),
        out_shape=jax.ShapeDtypeStruct((2 * nb, NFREQ, D_B), F32),
        scratch_shapes=[pltpu.VMEM((NFREQ, D_B), F32)],
        compiler_params=_params("arbitrary"),
        name="hyena_filter",
    )(zpos, zpos_t, w1p, b1p, w2p, b2p, w3p, b3p, fqp, w4p, jnp.asarray(_decay_rates()),
      f_hi, f_lo, jnp.asarray(sgn))


def _short_conv_gate(u0, u1, uv, w_refs, b_refs, row_len):
    sublane = lax.broadcasted_iota(jnp.int32, (SUBLANES, 1), 0)

    def zero_rows(x, offset):
        slabs = []
        for g in range(BLK // SUBLANES):
            slab = x[g * SUBLANES:(g + 1) * SUBLANES, :]
            hit = (offset - g * SUBLANES) % row_len
            if hit < SUBLANES:
                slab = jnp.where(sublane == hit, 0.0, slab)
            slabs.append(slab)
        return jnp.concatenate(slabs, axis=0)

    def conv(u, w_ref, b_ref):
        u = u.astype(F32)
        up = zero_rows(pltpu.roll(u, 1, 0), 0)
        dn = zero_rows(pltpu.roll(u, BLK - 1, 0), row_len - 1)
        return up * w_ref[0:1, :] + u * w_ref[1:2, :] + dn * w_ref[2:3, :] + b_ref[...]

    return (conv(u0, w_refs[0], b_refs[0]),
            conv(uv, w_refs[2], b_refs[2]) * conv(u1, w_refs[1], b_refs[1]))


def _hy_single_kernel(x0_ref, x1_ref, v_ref, w0_ref, w1_ref, wv_ref, b0_ref, b1_ref, bv_ref, f_ref, kh_ref,
                      skip_ref, g_ref, o_ref, *, row_len):
    row0 = lax.broadcasted_iota(jnp.int32, (BLK, 1), 0) == 0

    def sequence(s):
        rows = pl.ds(s * BLK, BLK)
        x0, z = _short_conv_gate(x0_ref[rows, :], x1_ref[rows, :], v_ref[rows, :], (w0_ref, w1_ref, wv_ref),
                                 (b0_ref, b1_ref, bv_ref), row_len)
        z_bf = z.astype(BF16)
        yield
        zh = _dot(f_ref[...], z_bf)
        yield
        zr, zi = zh[:BLK, :], zh[BLK:, :]
        kr, ki = kh_ref[0, :BLK, :], kh_ref[0, BLK:, :]
        p = zr * kr
        q = zi * ki
        r = zr * ki + zi * kr
        yh = jnp.concatenate([jnp.where(row0, p, p - q), jnp.where(row0, q, r)], axis=0).astype(BF16)
        yield
        y = _dot(g_ref[...], yh)
        yield
        o_ref[rows, :] = (x0 * (y + z * skip_ref[...])).astype(o_ref.dtype)

    _wavefront([sequence(s) for s in range(o_ref.shape[0] // BLK)])


def _hyena_single(pb, khat, conv_w, conv_b, skip, row_len):
    t = pb.shape[0]
    fwd_dft, inv_dft, _ = _dft_consts()
    rows = HY_SEQS * BLK
    col = lambda cb: pl.BlockSpec((rows, D_B), lambda i, cb=cb: (i, cb))
    wcol = lambda r, k: pl.BlockSpec((r, D_B), lambda i, k=k: (0, k))
    conv_b = conv_b[None, :]
    return pl.pallas_call(
        functools.partial(_hy_single_kernel, row_len=row_len),
        grid=(t // rows,),
        in_specs=[col(OB_X0), col(OB_X1), col(OB_V),
                  wcol(3, 0), wcol(3, 1), wcol(3, 2), wcol(1, 0), wcol(1, 1), wcol(1, 2),
                  pl.BlockSpec((NFREQ, BLK), lambda i: (0, 0)),
                  pl.BlockSpec((1, NFREQ, D_B), lambda i: (1, 0, 0)),
                  pl.BlockSpec((1, D_B), lambda i: (0, 0)),
                  pl.BlockSpec((BLK, NFREQ), lambda i: (0, 0))],
        out_specs=pl.BlockSpec((rows, D_B), lambda i: (i, 0)),
        out_shape=jax.ShapeDtypeStruct((t, D_B), BF16),
        compiler_params=_params("parallel"),
        name="hyena_single",
    )(pb, pb, pb, conv_w, conv_w, conv_w, conv_b, conv_b, conv_b, jnp.asarray(fwd_dft).astype(BF16), khat,
      skip[None, :], jnp.asarray(inv_dft).astype(BF16))


HY_DT = 2 * LANE
ROWG = 16


def _hy_multi_kernel(x0_ref, x1_ref, v_ref, w0_ref, w1_ref, wv_ref, b0_ref, b1_ref, bv_ref, f_ref, kh_ref,
                     skip_ref, g_ref, o_ref, zh_ref, z_ref, x0s_ref, yh_ref, *, nb, row_len):
    dt = o_ref.shape[1]

    def front(blk, carry):
        rows = pl.ds(pl.multiple_of(blk * BLK, BLK), BLK)
        x0, z = _short_conv_gate(x0_ref[rows, :], x1_ref[rows, :], v_ref[rows, :],
                                 (w0_ref, w1_ref, wv_ref), (b0_ref, b1_ref, bv_ref), row_len)
        x0s_ref[rows, :] = x0
        z_ref[rows, :] = z
        zh_ref[blk] = _dot(f_ref[...], z.astype(BF16))
        return carry

    lax.fori_loop(0, nb, front, 0)

    row_in_group = lax.broadcasted_iota(jnp.int32, (ROWG, 1), 0)

    def back(i, carry):
        def row_group(rg, carry2):
            re = pl.ds(pl.multiple_of(rg * ROWG, ROWG), ROWG)
            im = pl.ds(pl.multiple_of(BLK + rg * ROWG, ROWG), ROWG)

            def body(j, acc):
                p, q, r = acc
                kidx = i - j + nb
                zr = zh_ref[j, re, :]
                zi = zh_ref[j, im, :]
                kr = kh_ref[kidx, re, :]
                ki = kh_ref[kidx, im, :]
                return (p + zr * kr, q + zi * ki, r + (zr * ki + zi * kr))

            zeros = jnp.zeros((ROWG, dt), F32)
            p, q, r = lax.fori_loop(0, nb, body, (zeros, zeros, zeros), unroll=True)
            row0 = (row_in_group + rg * ROWG) == 0
            yh_ref[re, :] = jnp.where(row0, p, p - q).astype(BF16)
            yh_ref[im, :] = jnp.where(row0, q, r).astype(BF16)
            return carry2

        lax.fori_loop(0, BLK // ROWG, row_group, 0)
        rows = pl.ds(pl.multiple_of(i * BLK, BLK), BLK)
        y = _dot(g_ref[...], yh_ref[...]) + z_ref[rows, :] * skip_ref[...]
        o_ref[rows, :] = (x0s_ref[rows, :] * y).astype(o_ref.dtype)
        return carry

    lax.fori_loop(0, nb, back, 0)


def _hyena_multi(pb, khat, conv_w, conv_b, skip, nseq, nb, row_len):
    t = pb.shape[0]
    seq_len = nb * BLK
    dt = HY_DT
    per = D_B // dt
    fwd_dft, inv_dft, _ = _dft_consts()
    seq = lambda cb: pl.BlockSpec((seq_len, dt), lambda d, b, cb=cb: (b, cb * per + d))
    wcol = lambda r, k: pl.BlockSpec((r, dt), lambda d, b, k=k: (0, k * per + d))
    conv_b = conv_b[None, :]
    return pl.pallas_call(
        functools.partial(_hy_multi_kernel, nb=nb, row_len=row_len),
        grid=(per, nseq),
        in_specs=[seq(OB_X0), seq(OB_X1), seq(OB_V),
                  wcol(3, 0), wcol(3, 1), wcol(3, 2), wcol(1, 0), wcol(1, 1), wcol(1, 2),
                  pl.BlockSpec((NFREQ, BLK), lambda d, b: (0, 0)),
                  pl.BlockSpec((2 * nb, NFREQ, dt), lambda d, b: (0, 0, d), pipeline_mode=pl.Buffered(1)),
                  pl.BlockSpec((1, dt), lambda d, b: (0, d)),
                  pl.BlockSpec((BLK, NFREQ), lambda d, b: (0, 0))],
        out_specs=pl.BlockSpec((seq_len, dt), lambda d, b: (b, d)),
        out_shape=jax.ShapeDtypeStruct((t, D_B), BF16),
        scratch_shapes=[pltpu.VMEM((nb, NFREQ, dt), F32), pltpu.VMEM((seq_len, dt), F32),
                        pltpu.VMEM((seq_len, dt), F32), pltpu.VMEM((NFREQ, dt), BF16)],
        compiler_params=_params("parallel", "parallel"),
        name="hyena_multi",
    )(pb, pb, pb, conv_w, conv_w, conv_w, conv_b, conv_b, conv_b, jnp.asarray(fwd_dft).astype(BF16), khat,
      skip[None, :], jnp.asarray(inv_dft).astype(BF16))


def _layer_norm(y, g, b):
    mu = jnp.mean(y, axis=-1, keepdims=True)
    yc = y - mu
    var = jnp.mean(yc * yc, axis=-1, keepdims=True)
    return yc * lax.rsqrt(var + LN_EPS) * g + b


def _post_kernel(*refs, n_dir):
    (g_ref, ga_ref, gb_ref, hy_ref, x_ref, mod_ref, nw_ref, pa_ref, pb_ref, wo_ref, lg_ref, lb_ref,
     wg_ref, wu_ref, wo2_ref, lg2_ref, lb2_ref, o_ref) = refs[n_dir:]
    nw = nw_ref[...]

    def row_group(r):
        rows = pl.ds(r * POST_ROWS, POST_ROWS)
        o = refs[0][rows, :]
        for d_ref in refs[1:n_dir]:
            o = o + d_ref[rows, :]
        parts = []
        for h in range(H_A):
            oh = o[:, h * DV:(h + 1) * DV]
            ms = jnp.mean(oh * oh, axis=-1, keepdims=True)
            parts.append(oh * lax.rsqrt(ms + RMS_EPS) * nw)
        oa = (jnp.concatenate(parts, axis=1) * g_ref[rows, :]).astype(BF16)
        yield
        a = _dot(oa, pa_ref[...])
        b = _dot(hy_ref[rows, :].astype(BF16), pb_ref[...])
        yield
        merged = (ga_ref[rows, :] * a + gb_ref[rows, :] * b).astype(BF16)
        yield
        mix = _dot(merged, wo_ref[...])
        yield
        y = ALPHA * x_ref[rows, :] + mod_ref[0, 2:3, :] * mix
        x1 = _layer_norm(y, lg_ref[...], lb_ref[...])
        h = (x1 * (1.0 + mod_ref[0, 4:5, :]) + mod_ref[0, 3:4, :]).astype(BF16)
        yield
        gt = _dot(h, wg_ref[...])
        up = _dot(h, wu_ref[...])
        yield
        act = (gt * _sigmoid(gt) * up).astype(BF16)
        yield
        ff = _dot(act, wo2_ref[...])
        yield
        y2 = ALPHA * x1 + mod_ref[0, 5:6, :] * ff
        o_ref[rows, :] = _layer_norm(y2, lg2_ref[...], lb2_ref[...])

    _wavefront([row_group(r) for r in range(x_ref.shape[0] // POST_ROWS)])


def _post(o_dirs, pb16, o_hy, x, mod3, cond_row, norm_w, pa, pb, wo, ln_g, ln_b,
          ffn_w_in, ffn_w_out, ln2_g, ln2_b):
    tm = 256
    t = x.shape[0]
    n_dir = len(o_dirs)
    once = lambda shape, idx: pl.BlockSpec(shape, idx, pipeline_mode=pl.Buffered(1))
    tok = pl.BlockSpec((tm, D_MODEL), lambda i: (i, 0))
    col = lambda cb: pl.BlockSpec((tm, D_MODEL), lambda i, cb=cb: (i, cb))
    mat = once((D_MODEL, D_MODEL), lambda i: (0, 0))
    vec = pl.BlockSpec((1, D_MODEL), lambda i: (0, 0))
    return pl.pallas_call(
        functools.partial(_post_kernel, n_dir=n_dir),
        grid=(t // tm,),
        in_specs=[tok] * n_dir + [col(OB_G), col(OB_GA), col(OB_GB), tok, tok,
                  pl.BlockSpec((1, N_MOD, D_MODEL), lambda i: (cond_row(i * tm), 0, 0)),
                  pl.BlockSpec((1, DV), lambda i: (0, 0)),
                  mat, mat, mat, vec, vec,
                  once((D_MODEL, D_FF), lambda i: (0, 0)),
                  once((D_MODEL, D_FF), lambda i: (0, 1)),
                  once((D_FF, D_MODEL), lambda i: (0, 0)),
                  vec, vec],
        out_specs=tok,
        out_shape=jax.ShapeDtypeStruct((t, D_MODEL), F32),
        compiler_params=_params("parallel"),
        name="merge_ffn",
    )(*o_dirs, pb16, pb16, pb16, o_hy, x, mod3, norm_w[None, :], pa, pb, wo, ln_g[None, :], ln_b[None, :],
      ffn_w_in, ffn_w_in, ffn_w_out, ln2_g[None, :], ln2_b[None, :])


def _trunk(x, mod3, cond_row, nseq, seq_len, row_len, state, emit_state, w, lb_logits):
    nb = seq_len // BLK
    pf, pb = _inproj(x, mod3, w["w_in"], lb_logits, cond_row)
    hg = _hgrn(pf, pb, state, nseq, nb, emit_state)
    n_dir = 2 if nb > 1 else 1
    khat = _filter_spectra(seq_len, w["filt_w1"], w["filt_b1"], w["filt_w2"], w["filt_b2"],
                           w["filt_w3"], w["filt_b3"], w["filt_freq"], w["filt_w4"])
    if nb == 1:
        o_hy = _hyena_single(pb, khat, w["hy_conv_w"], w["hy_conv_b"], w["hy_skip"], row_len)
    else:
        o_hy = _hyena_multi(pb, khat, w["hy_conv_w"], w["hy_conv_b"], w["hy_skip"], nseq, nb, row_len)
    x2 = _post(hg[:n_dir], pb, o_hy, x, mod3, cond_row, w["hgrn_norm_w"], w["proj_a"], w["proj_b"],
               w["w_out"], w["ln1_g"], w["ln1_b"], w["ffn_w_in"], w["ffn_w_out"], w["ln2_g"], w["ln2_b"])
    return x2, (hg[n_dir] if emit_state else None)


def kernel(x_prompt, x_sample, state_hgrn, c, c_ctx, ada_w, ada_b, w_in, hgrn_lb_logits, hgrn_norm_w,
           hy_conv_w, hy_conv_b, filt_w1, filt_b1, filt_w2, filt_b2, filt_w3, filt_b3, filt_freq, filt_w4,
           hy_skip, proj_a, proj_b, w_out, ln1_g, ln1_b, ffn_w_in, ffn_w_out, ln2_g, ln2_b):
    assert ada_w.shape[0] == DEPTH == 1
    batch, seq, _ = x_prompt.shape
    dec_batch, dec_seq, _ = x_sample.shape
    assert seq % BLK == 0 and dec_seq % BLK == 0 and BLK % GRID_W == 0 and dec_batch + 1 <= 8

    w = dict(w_in=w_in[0].astype(BF16), hy_conv_w=hy_conv_w[0], hy_conv_b=hy_conv_b[0],
             filt_w1=filt_w1[0], filt_b1=filt_b1[0], filt_w2=filt_w2[0], filt_b2=filt_b2[0],
             filt_w3=filt_w3[0], filt_b3=filt_b3[0], filt_freq=filt_freq[0], filt_w4=filt_w4[0],
             hy_skip=hy_skip[0], hgrn_norm_w=hgrn_norm_w[0], proj_a=proj_a[0].astype(BF16),
             proj_b=proj_b[0].astype(BF16), w_out=w_out[0].astype(BF16), ln1_g=ln1_g[0], ln1_b=ln1_b[0],
             ffn_w_in=ffn_w_in[0].astype(BF16), ffn_w_out=ffn_w_out[0].astype(BF16),
             ln2_g=ln2_g[0], ln2_b=ln2_b[0])

    cond8 = jnp.zeros((8, D_MODEL), F32).at[0].set(c_ctx).at[1:1 + dec_batch].set(c)
    mod3 = _modulation(cond8, ada_w[0], ada_b[0][None, :]).reshape(8, N_MOD, D_MODEL)

    xp = x_prompt.reshape(batch * seq, D_MODEL)
    xs = x_sample.reshape(dec_batch * dec_seq, D_MODEL)
    yp, new_state = _trunk(xp, mod3, lambda tok: 0, batch, seq, seq, None, True, w, hgrn_lb_logits)
    ys, _ = _trunk(xs, mod3, lambda tok: 1 + tok // dec_seq, dec_batch, dec_seq, GRID_W, state_hgrn, False,
                   w, hgrn_lb_logits)
    return (yp.reshape(batch, seq, D_MODEL), ys.reshape(dec_batch, dec_seq, D_MODEL), new_state)
```

```python
import functools
import math

import numpy as np
import jax
import jax.numpy as jnp
from jax import lax
from jax.experimental import pallas as pl
from jax.experimental.pallas import tpu as pltpu

F32 = jnp.float32
BF16 = jnp.bfloat16

D_MODEL = 1024
DEPTH = 1
GRID_W = 64
H_A = 8
DK = 128
DV = 128
D_B = 1024
FILT_EMB = 33
FILT_BANDS = 16
FILT_ORDER = 64
DECAY_FAST = 0.3
DECAY_SLOW = 1.5
DECAY_TARGET = 1e-2
DECAY_SHIFT = 0.05
D_FF = 2816
N_MOD = 6
W_IN_COLS = 10 * D_MODEL
ALPHA = (2.0 * DEPTH) ** 0.25
LN_EPS = 1e-5
RMS_EPS = 1e-6

LANE = 128
SUBLANES = 8
BLK = 256
NFREQ = 2 * BLK
CHUNK = 32
NCHUNK = BLK // CHUNK
HPS = 8
PROJ_ROWS = 128
POST_ROWS = 256
HY_SEQS = 1
FILT_BLOCKS = 8
VMEM_LIMIT = 56 * 1024 * 1024

CB_Q, CB_FF, CB_FB, CB_I, CB_G, CB_X0, CB_X1, CB_V, CB_GA, CB_GB = range(10)
STEP_COLS = (CB_FF, CB_FB, CB_Q, CB_G, CB_GA, CB_GB, CB_I, CB_X0, CB_X1, CB_V)
N_F32_COLS = 2
OF_FF, OF_FB = range(N_F32_COLS)
OB_Q, OB_G, OB_GA, OB_GB, OB_I, OB_X0, OB_X1, OB_V = range(len(STEP_COLS) - N_F32_COLS)


def _sigmoid(x):
    return 1.0 / (1.0 + jnp.exp(-x))


def _dot(a, b):
    return jnp.dot(a, b, preferred_element_type=F32)


def _dot_nt(a, b):
    return lax.dot_general(a, b, (((1,), (1,)), ((), ())), preferred_element_type=F32)


def _dot_tn(a, b):
    return lax.dot_general(a, b, (((0,), (0,)), ((), ())), preferred_element_type=F32)


def _dot_hi(a, b):
    return jnp.dot(a, b, preferred_element_type=F32, precision=lax.Precision.HIGHEST)


def _params(*sem):
    return pltpu.CompilerParams(dimension_semantics=sem, vmem_limit_bytes=VMEM_LIMIT)


@functools.lru_cache(maxsize=None)
def _dft_consts():
    n = np.arange(BLK, dtype=np.float64)
    f = np.arange(BLK, dtype=np.float64)
    ang = 2.0 * np.pi * np.outer(f, n) / NFREQ
    fwd = np.zeros((NFREQ, BLK), np.float64)
    fwd[:BLK] = np.cos(ang)
    fwd[BLK + 1:] = -np.sin(ang[1:])
    fwd[BLK] = np.cos(np.pi * n)
    inv = np.zeros((BLK, NFREQ), np.float64)
    scale = np.full((BLK,), 2.0)
    scale[0] = 1.0
    inv[:, :BLK] = np.cos(ang.T) * scale[None, :]
    inv[:, BLK + 1:] = -2.0 * np.sin(ang.T[:, 1:])
    inv[:, BLK] = np.cos(np.pi * n)
    inv /= NFREQ
    fr = np.arange(NFREQ)
    freq_of_row = np.where(fr < BLK, fr, np.where(fr == BLK, BLK, fr - BLK))
    sgn = np.where(freq_of_row % 2 == 0, 1.0, -1.0)[:, None]
    return fwd.astype(np.float32), inv.astype(np.float32), sgn.astype(np.float32)


@functools.lru_cache(maxsize=None)
def _scan_consts():
    t = np.arange(BLK)
    ct = t // CHUNK
    same = ct[:, None] == ct[None, :]
    tri_f = (same & (t[None, :] <= t[:, None])).astype(np.float32)
    tri_b = (same & (t[None, :] >= t[:, None])).astype(np.float32)

    def levels(p, diag):
        x = p[:, None] ^ p[None, :]
        lvl = np.zeros_like(x)
        for bit in range(1, NCHUNK.bit_length()):
            lvl = np.where(x >= (1 << (bit - 1)), bit, lvl)
        lv = np.where(p[:, None] > p[None, :], lvl, -1)
        return np.where(same, np.where(diag, 0, -1), lv).astype(np.int32)

    lv_f = levels(ct, t[None, :] <= t[:, None])
    lv_b = levels(NCHUNK - 1 - ct, t[None, :] >= t[:, None])
    return tri_f, tri_b, lv_f, lv_b


@functools.lru_cache(maxsize=None)
def _filter_positions(seq_len):
    f32 = np.float32
    j = np.arange(-seq_len, seq_len)
    p = np.abs(j)
    valid = (j > -seq_len)
    pc = np.minimum(p, seq_len - 1)
    t = np.linspace(0.0, 1.0, seq_len, dtype=f32)[pc]
    wpos = (f32(2.0 * math.pi / seq_len) * np.arange(seq_len, dtype=f32))[pc]
    bands = np.linspace(1e-4, FILT_BANDS - 1, FILT_BANDS, dtype=f32)
    arg = (bands[None, :] * wpos[:, None]).astype(f32)
    z = np.zeros((2 * seq_len, LANE), f32)
    z[:, 0] = t
    z[:, 1:1 + FILT_BANDS] = np.cos(arg)
    z[:, 1 + FILT_BANDS:FILT_EMB] = -np.sin(arg)
    z[:, FILT_EMB] = valid.astype(f32)
    return z


@functools.lru_cache(maxsize=None)
def _decay_rates():
    max_decay = math.log(DECAY_TARGET) / DECAY_FAST
    min_decay = math.log(DECAY_TARGET) / DECAY_SLOW
    return np.abs(np.linspace(min_decay, max_decay, D_B, dtype=np.float32))[None, :]


def _mod_kernel(c_ref, w_ref, b_ref, o_ref):
    c = c_ref[...]
    s = (c * _sigmoid(c)).astype(BF16)
    o_ref[...] = _dot(s, w_ref[...].astype(BF16)) + b_ref[...]


def _modulation(cond8, ada_w, ada_b):
    tn = 1536
    n = N_MOD * D_MODEL
    return pl.pallas_call(
        _mod_kernel,
        grid=(n // tn,),
        in_specs=[pl.BlockSpec((8, D_MODEL), lambda j: (0, 0)),
                  pl.BlockSpec((D_MODEL, tn), lambda j: (0, j)),
                  pl.BlockSpec((1, tn), lambda j: (0, j))],
        out_specs=pl.BlockSpec((8, tn), lambda j: (0, j)),
        out_shape=jax.ShapeDtypeStruct((8, n), F32),
        compiler_params=_params("parallel"),
        name="modulation",
    )(cond8, ada_w, ada_b)


def _lower_bounds(lbl_ref):
    l0 = lbl_ref[0]
    l1 = lbl_ref[1]
    m = jnp.maximum(l0, l1)
    e0 = jnp.exp(l0 - m)
    e1 = jnp.exp(l1 - m)
    return e0 / (e0 + e1)


def _inproj_kernel(x_ref, mod_ref, w_ref, lbl_ref, of_ref, ob_ref, h_ref):
    j = pl.program_id(1)

    @pl.when(j == 0)
    def _():
        h = x_ref[...] * (1.0 + mod_ref[0, 1:2, :]) + mod_ref[0, 0:1, :]
        h_ref[...] = h.astype(BF16)

    def project(o_ref, act):
        for r in range(x_ref.shape[0] // PROJ_ROWS):
            rows = pl.ds(r * PROJ_ROWS, PROJ_ROWS)
            o_ref[rows, :] = act(_dot(h_ref[rows, :], w_ref[...])).astype(o_ref.dtype)

    silu = lambda a: a * _sigmoid(a)
    step = STEP_COLS.index

    @pl.when(jnp.logical_or(j == step(CB_FF), j == step(CB_FB)))
    def _():
        lb2 = _lower_bounds(lbl_ref)
        lb = jnp.where(j == step(CB_FF), lb2[0:1, :], lb2[1:2, :])
        project(of_ref, lambda a: jnp.log(lb + (1.0 - lb) * _sigmoid(a)))

    @pl.when(jnp.logical_or(j == step(CB_Q), j == step(CB_G)))
    def _():
        project(ob_ref, silu)

    @pl.when(jnp.logical_or(j == step(CB_GA), j == step(CB_GB)))
    def _():
        project(ob_ref, _sigmoid)

    @pl.when(j >= step(CB_I))
    def _():
        project(ob_ref, lambda a: a)


def _weight_col(j):
    col = jnp.int32(STEP_COLS[-1])
    for step in reversed(range(len(STEP_COLS) - 1)):
        col = jnp.where(j == step, STEP_COLS[step], col)
    return col


def _inproj(x, mod3, w_bf, lb_logits, cond_row):
    tm = 2048
    t = x.shape[0]
    n_b16 = len(STEP_COLS) - N_F32_COLS
    return pl.pallas_call(
        _inproj_kernel,
        grid=(t // tm, len(STEP_COLS)),
        in_specs=[pl.BlockSpec((tm, D_MODEL), lambda i, j: (i, 0)),
                  pl.BlockSpec((1, N_MOD, D_MODEL), lambda i, j: (cond_row(i * tm), 0, 0)),
                  pl.BlockSpec((D_MODEL, D_MODEL), lambda i, j: (0, _weight_col(j))),
                  pl.BlockSpec((2, 2, D_MODEL), lambda i, j: (0, 0, 0))],
        out_specs=[pl.BlockSpec((tm, D_MODEL), lambda i, j: (i, jnp.minimum(j, N_F32_COLS - 1))),
                   pl.BlockSpec((tm, D_MODEL), lambda i, j: (i, jnp.maximum(j - N_F32_COLS, 0)))],
        out_shape=[jax.ShapeDtypeStruct((t, N_F32_COLS * D_MODEL), F32),
                   jax.ShapeDtypeStruct((t, n_b16 * D_MODEL), BF16)],
        scratch_shapes=[pltpu.VMEM((tm, D_MODEL), BF16)],
        compiler_params=_params("parallel", "arbitrary"),
        name="inproj",
    )(x, mod3, w_bf, lb_logits)


def _chunk_cumsum(lf, tri):
    lf_hi = lf.astype(BF16)
    lf_lo = (lf - lf_hi.astype(F32)).astype(BF16)
    return _dot(tri, lf_hi) + _dot(tri, lf_lo)


def _wavefront(units, newest_first=False):
    waiting, active = list(units), []
    while waiting or active:
        if waiting:
            active.append(waiting.pop(0))
        for unit in (reversed(list(active)) if newest_first else list(active)):
            try:
                next(unit)
            except StopIteration:
                active.remove(unit)


def _hgrn_direction(load, lv, reverse, use_state, finish):
    q, lf, b, v, st = load()
    k = 1.0 - jnp.exp(lf)
    qe = q.astype(F32) * jnp.exp(b)
    k0 = k * jnp.exp(-b)
    yield

    order = [NCHUNK - 1 - i for i in range(NCHUNK)] if reverse else list(range(NCHUNK))
    chunk_of = {p: i for i, p in enumerate(order)}
    sl = lambda i: slice(i * CHUNK, (i + 1) * CHUNK)
    last_row = lambda i: (i * CHUNK) if reverse else (i * CHUNK + CHUNK - 1)
    c = [None] * NCHUNK
    for p in range(NCHUNK):
        r = last_row(chunk_of[p])
        c[p] = b[r:r + 1, :]
    cum = [jnp.zeros_like(c[0])]
    for p in range(NCHUNK):
        cum.append(cum[p] + c[p])
    total = cum[NCHUNK]

    qe_c, ke_c = {}, {}
    for p in range(NCHUNK):
        i = chunk_of[p]
        qe_c[p] = qe[sl(i), :]
        ke_c[p] = k0[sl(i), :] * jnp.exp(c[p])

    def assemble(parts):
        return jnp.concatenate([parts[order[i]] for i in range(NCHUNK)], axis=0).astype(BF16)

    cph = NCHUNK // 2
    nlev = NCHUNK.bit_length() - 1

    def half_rows(hh):
        first = (1 - hh) if reverse else hh
        return slice(first * cph * CHUNK, (first + 1) * cph * CHUNK)

    def assemble_half(parts, hh):
        ps = range(hh * cph, (hh + 1) * cph)
        return jnp.concatenate([parts[p] for p in (reversed(ps) if reverse else ps)], axis=0).astype(BF16)

    zero = jnp.zeros((CHUNK, LANE), F32)
    qe_bf, k0_bf = qe.astype(BF16), k0.astype(BF16)
    yield
    s_half = [jnp.where(lv == 0, _dot_nt(qe_bf[half_rows(hh), :], k0_bf[half_rows(hh), :]), 0.0)
              for hh in range(2)]
    for lev in range(1, nlev):
        yield
        mid = 1 << (lev - 1)
        qp, kp = {}, {}
        for p in range(NCHUNK):
            pm = ((p >> lev) << lev) + mid
            if p >= pm:
                qp[p] = qe_c[p] * jnp.exp(cum[p] - cum[pm])
                kp[p] = zero
            else:
                qp[p] = zero
                kp[p] = ke_c[p] * jnp.exp(cum[pm] - cum[p + 1])
        for hh in range(2):
            s_lev = _dot_nt(assemble_half(qp, hh), assemble_half(kp, hh))
            s_half[hh] = jnp.where(lv == lev, s_lev, s_half[hh])
    yield
    q_top = assemble_half({p: qe_c[p] * jnp.exp(cum[p] - cum[cph]) for p in range(cph, NCHUNK)}, 1)
    k_top = assemble_half({p: ke_c[p] * jnp.exp(cum[cph] - cum[p + 1]) for p in range(cph)}, 0)
    s_top = _dot_nt(q_top, k_top)
    yield

    v_bf = v.astype(BF16)
    v_half = [v_bf[half_rows(hh), :] for hh in range(2)]
    out_half = [_dot(s_half[0].astype(BF16), v_half[0]),
                _dot(jnp.concatenate([s_top, s_half[1]], axis=1).astype(BF16),
                     jnp.concatenate(v_half, axis=0))]
    out = jnp.concatenate(out_half[::-1] if reverse else out_half, axis=0)
    yield
    if use_state:
        q_start = assemble({p: qe_c[p] * jnp.exp(cum[p]) for p in range(NCHUNK)})
        out = out + _dot_nt(q_start, st.astype(BF16))
    k_end = assemble({p: ke_c[p] * jnp.exp(total - cum[p + 1]) for p in range(NCHUNK)})
    upd = _dot_tn(v_bf, k_end)
    new_st = st * jnp.exp(total) + upd if use_state else upd
    finish(out, new_st)


def _hgrn_kernel(*refs, nb, zero_init, emit_state):
    it = iter(refs)
    qf_ref, lff_ref, vf_ref, qb_ref, lfb_ref, vb_ref = [next(it) for _ in range(6)]
    s0_ref = None if zero_init else next(it)
    lvf_ref, lvb_ref, trif_ref, trib_ref = [next(it) for _ in range(4)]
    of_ref = next(it)
    ob_ref = next(it) if nb > 1 else None
    so_ref = next(it) if emit_state else None
    st_ref = next(it)
    i = pl.program_id(2)

    use_state = not (zero_init and nb == 1)
    if use_state:
        @pl.when(i == 0)
        def _():
            for d in range(2):
                for h in range(HPS):
                    st_ref[d, h] = jnp.zeros((DV, DK), F32) if zero_init else s0_ref[0, 0, d, h].T

    lf_f = lff_ref[...]
    lf_b = lfb_ref[...]
    b_f = _chunk_cumsum(lf_f, trif_ref[...])
    b_b = _chunk_cumsum(lf_b, trib_ref[...])
    lv_f = lvf_ref[...]
    lv_b = lvb_ref[...]
    def unit(d, h):
        hs = slice(h * LANE, (h + 1) * LANE)
        q_ref, lf, b, v_ref, o_ref = ((qf_ref, lf_f, b_f, vf_ref, of_ref), (qb_ref, lf_b, b_b, vb_ref, ob_ref))[d]

        def finish(out, new_st):
            if nb > 1:
                o_ref[:, hs] = out
                st_ref[d, h] = new_st
            elif d == 0:
                of_ref[:, hs] = out
            else:
                of_ref[:, hs] += out
            if emit_state and nb == 1:
                so_ref[0, 0, d, h] = new_st.T
            elif emit_state:
                @pl.when(i == nb - 1)
                def _():
                    so_ref[0, 0, d, h] = new_st.T

        return _hgrn_direction(lambda: (q_ref[:, hs], lf[:, hs], b[:, hs], v_ref[:, hs], st_ref[d, h]),
                               (lv_f, lv_b)[d], d == 1, use_state, finish)

    _wavefront([unit(d, h) for d in range(2) for h in range(HPS)], newest_first=use_state)


def _hgrn(pf, pb, state, nseq, nb, emit_state):
    zero_init = state is None
    t = pf.shape[0]
    tri_f, tri_b, lv_f, lv_b = _scan_consts()
    wid = HPS * LANE
    per = D_MODEL // wid
    fwd = lambda cb: pl.BlockSpec((BLK, wid), lambda b, h, i, cb=cb: (b * nb + i, cb * per + h))
    bwd = lambda cb: pl.BlockSpec((BLK, wid), lambda b, h, i, cb=cb: (b * nb + nb - 1 - i, cb * per + h))
    const = lambda n=BLK: pl.BlockSpec((n, n), lambda b, h, i: (0, 0))
    hl = BLK // 2
    st_spec = pl.BlockSpec((1, 1, 2, HPS, DK, DV), lambda b, h, i: (b, 0, 0, h, 0, 0))
    in_specs = [fwd(OB_Q), fwd(OF_FF), fwd(OB_I), bwd(OB_Q), bwd(OF_FB), bwd(OB_I)]
    args = [pb, pf, pb, pb, pf, pb]
    if not zero_init:
        in_specs.append(st_spec)
        args.append(state)
    in_specs += [const(hl), const(hl), const(), const()]
    args += [jnp.asarray(lv_f[:hl, :hl]), jnp.asarray(lv_b[:hl, :hl]),
             jnp.asarray(tri_f, BF16), jnp.asarray(tri_b, BF16)]
    out_specs = [pl.BlockSpec((BLK, wid), lambda b, h, i: (b * nb + i, h))]
    out_shape = [jax.ShapeDtypeStruct((t, D_MODEL), F32)]
    if nb > 1:
        out_specs.append(pl.BlockSpec((BLK, wid), lambda b, h, i: (b * nb + nb - 1 - i, h)))
        out_shape.append(jax.ShapeDtypeStruct((t, D_MODEL), F32))
    if emit_state:
        out_specs.append(st_spec)
        out_shape.append(jax.ShapeDtypeStruct((nseq, DEPTH, 2, H_A, DK, DV), F32))
    return pl.pallas_call(
        functools.partial(_hgrn_kernel, nb=nb, zero_init=zero_init, emit_state=emit_state),
        grid=(nseq, H_A // HPS, nb),
        in_specs=in_specs,
        out_specs=out_specs,
        out_shape=out_shape,
        scratch_shapes=[pltpu.VMEM((2, HPS, DV, DK), F32)],
        compiler_params=_params("parallel", "parallel", "arbitrary"),
        name="hgrn_scan",
    )(*args)


def _split_bf16(x):
    hi = x.astype(BF16)
    return hi, (x - hi.astype(F32)).astype(BF16)


def _dot3(a, b):
    a_hi, a_lo = _split_bf16(a)
    b_hi, b_lo = _split_bf16(b)
    return _dot(a_hi, b_hi) + (_dot(a_hi, b_lo) + _dot(a_lo, b_hi))


def _filter_kernel(z_ref, zt_ref, w1_ref, b1_ref, w2_ref, b2_ref, w3_ref, b3_ref, fq_ref, w4_ref, dec_ref,
                   fhi_ref, flo_ref, sgn_ref, o_ref, prev_ref, *, blocks):
    pad = jnp.zeros((LANE - FILT_ORDER, BLK), F32)

    def layer(x, w_ref, b_ref, k):
        h = jnp.sin(fq_ref[:, k:k + 1] * (_dot_hi(w_ref[...], x) + b_ref[...]))
        return jnp.concatenate([h, pad], axis=0)

    @pl.when(pl.program_id(0) == 0)
    def _():
        prev_ref[...] = jnp.zeros_like(prev_ref)

    spectra = [prev_ref[...]]

    def tap_block(blk):
        pos = slice(blk * BLK, (blk + 1) * BLK)
        h = layer(zt_ref[:, pos], w1_ref, b1_ref, 0)
        yield
        h = layer(h, w2_ref, b2_ref, 1)
        yield
        h = layer(h, w3_ref, b3_ref, 2)
        yield
        a = _dot3(h.T, w4_ref[...])
        yield
        zp = z_ref[pos, :]
        window = jnp.exp(-zp[:, 0:1] * dec_ref[...]) + DECAY_SHIFT
        a = a * window * zp[:, FILT_EMB:FILT_EMB + 1]
        a_hi, a_lo = _split_bf16(a)
        yield
        f_hi = fhi_ref[...]
        ah = _dot(f_hi, a_hi) + (_dot(f_hi, a_lo) + _dot(flo_ref[...], a_hi))
        yield
        o_ref[blk] = ah + sgn_ref[...] * spectra[blk]
        spectra.append(ah)

    _wavefront([tap_block(blk) for blk in range(blocks)], newest_first=True)
    prev_ref[...] = spectra[blocks]


def _filter_spectra(seq_len, w1, b1, w2, b2, w3, b3, freq, w4):
    nb = seq_len // BLK
    fwd_dft, _, sgn = _dft_consts()
    zpos_np = _filter_positions(seq_len)
    zpos = jnp.asarray(zpos_np)
    zpos_t = jnp.asarray(np.ascontiguousarray(zpos_np.T))
    f_hi, f_lo = _split_bf16(jnp.asarray(fwd_dft))
    wt = lambda w: jnp.pad(w, ((0, LANE - w.shape[0]), (0, 0))).T
    colv = lambda b: b[:, None]
    w1p, w2p, w3p = wt(w1), wt(w2), wt(w3)
    b1p, b2p, b3p = colv(b1), colv(b2), colv(b3)
    fqp = freq.T
    w4p = jnp.pad(w4, ((0, LANE - w4.shape[0]), (0, 0)))
    small = lambda shape: pl.BlockSpec(shape, lambda s: (0, 0))
    blocks = math.gcd(nb, FILT_BLOCKS)
    return pl.pallas_call(
        functools.partial(_filter_kernel, blocks=blocks),
        grid=(2 * nb // blocks,),
        in_specs=[pl.BlockSpec((blocks * BLK, LANE), lambda s: (s, 0)),
                  pl.BlockSpec((LANE, blocks * BLK), lambda s: (0, s)),
                  small((FILT_ORDER, LANE)), small((FILT_ORDER, 1)), small((FILT_ORDER, LANE)),
                  small((FILT_ORDER, 1)), small((FILT_ORDER, LANE)), small((FILT_ORDER, 1)),
                  small((FILT_ORDER, 3)),
                  pl.BlockSpec((LANE, D_B), lambda s: (0, jnp.where(s * blocks < nb, 1, 0))),
                  small((1, D_B)), small((NFREQ, BLK)), small((NFREQ, BLK)), small((NFREQ, 1))],
        out_specs=pl.BlockSpec((blocks, NFREQ, D_B), lambda s: (s, 0, 0)),
        out_shape=jax.ShapeDtypeStruct((2 * nb, NFREQ, D_B), F32),
        scratch_shapes=[pltpu.VMEM((NFREQ, D_B), F32)],
        compiler_params=_params("arbitrary"),
        name="hyena_filter",
    )(zpos, zpos_t, w1p, b1p, w2p, b2p, w3p, b3p, fqp, w4p, jnp.asarray(_decay_rates()),
      f_hi, f_lo, jnp.asarray(sgn))


def _short_conv_gate(u0, u1, uv, w_refs, b_refs, row_len):
    sublane = lax.broadcasted_iota(jnp.int32, (SUBLANES, 1), 0)

    def zero_rows(x, offset):
        slabs = []
        for g in range(BLK // SUBLANES):
            slab = x[g * SUBLANES:(g + 1) * SUBLANES, :]
            hit = (offset - g * SUBLANES) % row_len
            if hit < SUBLANES:
                slab = jnp.where(sublane == hit, 0.0, slab)
            slabs.append(slab)
        return jnp.concatenate(slabs, axis=0)

    def conv(u, w_ref, b_ref):
        u = u.astype(F32)
        up = zero_rows(pltpu.roll(u, 1, 0), 0)
        dn = zero_rows(pltpu.roll(u, BLK - 1, 0), row_len - 1)
        return up * w_ref[0:1, :] + u * w_ref[1:2, :] + dn * w_ref[2:3, :] + b_ref[...]

    return (conv(u0, w_refs[0], b_refs[0]),
            conv(uv, w_refs[2], b_refs[2]) * conv(u1, w_refs[1], b_refs[1]))


def _hy_single_kernel(x0_ref, x1_ref, v_ref, w0_ref, w1_ref, wv_ref, b0_ref, b1_ref, bv_ref, f_ref, kh_ref,
                      skip_ref, g_ref, o_ref, *, row_len):
    row0 = lax.broadcasted_iota(jnp.int32, (BLK, 1), 0) == 0

    def sequence(s):
        rows = pl.ds(s * BLK, BLK)
        x0, z = _short_conv_gate(x0_ref[rows, :], x1_ref[rows, :], v_ref[rows, :], (w0_ref, w1_ref, wv_ref),
                                 (b0_ref, b1_ref, bv_ref), row_len)
        z_bf = z.astype(BF16)
        yield
        zh = _dot(f_ref[...], z_bf)
        yield
        zr, zi = zh[:BLK, :], zh[BLK:, :]
        kr, ki = kh_ref[0, :BLK, :], kh_ref[0, BLK:, :]
        p = zr * kr
        q = zi * ki
        r = zr * ki + zi * kr
        yh = jnp.concatenate([jnp.where(row0, p, p - q), jnp.where(row0, q, r)], axis=0).astype(BF16)
        yield
        y = _dot(g_ref[...], yh)
        yield
        o_ref[rows, :] = (x0 * (y + z * skip_ref[...])).astype(o_ref.dtype)

    _wavefront([sequence(s) for s in range(o_ref.shape[0] // BLK)])


def _hyena_single(pb, khat, conv_w, conv_b, skip, row_len):
    t = pb.shape[0]
    fwd_dft, inv_dft, _ = _dft_consts()
    rows = HY_SEQS * BLK
    col = lambda cb: pl.BlockSpec((rows, D_B), lambda i, cb=cb: (i, cb))
    wcol = lambda r, k: pl.BlockSpec((r, D_B), lambda i, k=k: (0, k))
    conv_b = conv_b[None, :]
    return pl.pallas_call(
        functools.partial(_hy_single_kernel, row_len=row_len),
        grid=(t // rows,),
        in_specs=[col(OB_X0), col(OB_X1), col(OB_V),
                  wcol(3, 0), wcol(3, 1), wcol(3, 2), wcol(1, 0), wcol(1, 1), wcol(1, 2),
                  pl.BlockSpec((NFREQ, BLK), lambda i: (0, 0)),
                  pl.BlockSpec((1, NFREQ, D_B), lambda i: (1, 0, 0)),
                  pl.BlockSpec((1, D_B), lambda i: (0, 0)),
                  pl.BlockSpec((BLK, NFREQ), lambda i: (0, 0))],
        out_specs=pl.BlockSpec((rows, D_B), lambda i: (i, 0)),
        out_shape=jax.ShapeDtypeStruct((t, D_B), BF16),
        compiler_params=_params("parallel"),
        name="hyena_single",
    )(pb, pb, pb, conv_w, conv_w, conv_w, conv_b, conv_b, conv_b, jnp.asarray(fwd_dft).astype(BF16), khat,
      skip[None, :], jnp.asarray(inv_dft).astype(BF16))


HY_DT = 2 * LANE
ROWG = 16


def _hy_multi_kernel(x0_ref, x1_ref, v_ref, w0_ref, w1_ref, wv_ref, b0_ref, b1_ref, bv_ref, f_ref, kh_ref,
                     skip_ref, g_ref, o_ref, zh_ref, z_ref, x0s_ref, yh_ref, *, nb, row_len):
    dt = o_ref.shape[1]

    def front(blk, carry):
        rows = pl.ds(pl.multiple_of(blk * BLK, BLK), BLK)
        x0, z = _short_conv_gate(x0_ref[rows, :], x1_ref[rows, :], v_ref[rows, :],
                                 (w0_ref, w1_ref, wv_ref), (b0_ref, b1_ref, bv_ref), row_len)
        x0s_ref[rows, :] = x0
        z_ref[rows, :] = z
        zh_ref[blk] = _dot(f_ref[...], z.astype(BF16))
        return carry

    lax.fori_loop(0, nb, front, 0)

    row_in_group = lax.broadcasted_iota(jnp.int32, (ROWG, 1), 0)

    def back(i, carry):
        def row_group(rg, carry2):
            re = pl.ds(pl.multiple_of(rg * ROWG, ROWG), ROWG)
            im = pl.ds(pl.multiple_of(BLK + rg * ROWG, ROWG), ROWG)

            def body(j, acc):
                p, q, r = acc
                kidx = i - j + nb
                zr = zh_ref[j, re, :]
                zi = zh_ref[j, im, :]
                kr = kh_ref[kidx, re, :]
                ki = kh_ref[kidx, im, :]
                return (p + zr * kr, q + zi * ki, r + (zr * ki + zi * kr))

            zeros = jnp.zeros((ROWG, dt), F32)
            p, q, r = lax.fori_loop(0, nb, body, (zeros, zeros, zeros), unroll=True)
            row0 = (row_in_group + rg * ROWG) == 0
            yh_ref[re, :] = jnp.where(row0, p, p - q).astype(BF16)
            yh_ref[im, :] = jnp.where(row0, q, r).astype(BF16)
            return carry2

        lax.fori_loop(0, BLK // ROWG, row_group, 0)
        rows = pl.ds(pl.multiple_of(i * BLK, BLK), BLK)
        y = _dot(g_ref[...], yh_ref[...]) + z_ref[rows, :] * skip_ref[...]
        o_ref[rows, :] = (x0s_ref[rows, :] * y).astype(o_ref.dtype)
        return carry

    lax.fori_loop(0, nb, back, 0)


def _hyena_multi(pb, khat, conv_w, conv_b, skip, nseq, nb, row_len):
    t = pb.shape[0]
    seq_len = nb * BLK
    dt = HY_DT
    per = D_B // dt
    fwd_dft, inv_dft, _ = _dft_consts()
    seq = lambda cb: pl.BlockSpec((seq_len, dt), lambda d, b, cb=cb: (b, cb * per + d))
    wcol = lambda r, k: pl.BlockSpec((r, dt), lambda d, b, k=k: (0, k * per + d))
    conv_b = conv_b[None, :]
    return pl.pallas_call(
        functools.partial(_hy_multi_kernel, nb=nb, row_len=row_len),
        grid=(per, nseq),
        in_specs=[seq(OB_X0), seq(OB_X1), seq(OB_V),
                  wcol(3, 0), wcol(3, 1), wcol(3, 2), wcol(1, 0), wcol(1, 1), wcol(1, 2),
                  pl.BlockSpec((NFREQ, BLK), lambda d, b: (0, 0)),
                  pl.BlockSpec((2 * nb, NFREQ, dt), lambda d, b: (0, 0, d), pipeline_mode=pl.Buffered(1)),
                  pl.BlockSpec((1, dt), lambda d, b: (0, d)),
                  pl.BlockSpec((BLK, NFREQ), lambda d, b: (0, 0))],
        out_specs=pl.BlockSpec((seq_len, dt), lambda d, b: (b, d)),
        out_shape=jax.ShapeDtypeStruct((t, D_B), BF16),
        scratch_shapes=[pltpu.VMEM((nb, NFREQ, dt), F32), pltpu.VMEM((seq_len, dt), F32),
                        pltpu.VMEM((seq_len, dt), F32), pltpu.VMEM((NFREQ, dt), BF16)],
        compiler_params=_params("parallel", "parallel"),
        name="hyena_multi",
    )(pb, pb, pb, conv_w, conv_w, conv_w, conv_b, conv_b, conv_b, jnp.asarray(fwd_dft).astype(BF16), khat,
      skip[None, :], jnp.asarray(inv_dft).astype(BF16))


def _layer_norm(y, g, b):
    mu = jnp.mean(y, axis=-1, keepdims=True)
    yc = y - mu
    var = jnp.mean(yc * yc, axis=-1, keepdims=True)
    return yc * lax.rsqrt(var + LN_EPS) * g + b


def _post_kernel(*refs, n_dir):
    (g_ref, ga_ref, gb_ref, hy_ref, x_ref, mod_ref, nw_ref, pa_ref, pb_ref, wo_ref, lg_ref, lb_ref,
     wg_ref, wu_ref, wo2_ref, lg2_ref, lb2_ref, o_ref) = refs[n_dir:]
    nw = nw_ref[...]

    def row_group(r):
        rows = pl.ds(r * POST_ROWS, POST_ROWS)
        o = refs[0][rows, :]
        for d_ref in refs[1:n_dir]:
            o = o + d_ref[rows, :]
        parts = []
        for h in range(H_A):
            oh = o[:, h * DV:(h + 1) * DV]
            ms = jnp.mean(oh * oh, axis=-1, keepdims=True)
            parts.append(oh * lax.rsqrt(ms + RMS_EPS) * nw)
        oa = (jnp.concatenate(parts, axis=1) * g_ref[rows, :]).astype(BF16)
        yield
        a = _dot(oa, pa_ref[...])
        b = _dot(hy_ref[rows, :].astype(BF16), pb_ref[...])
        yield
        merged = (ga_ref[rows, :] * a + gb_ref[rows, :] * b).astype(BF16)
        yield
        mix = _dot(merged, wo_ref[...])
        yield
        y = ALPHA * x_ref[rows, :] + mod_ref[0, 2:3, :] * mix
        x1 = _layer_norm(y, lg_ref[...], lb_ref[...])
        h = (x1 * (1.0 + mod_ref[0, 4:5, :]) + mod_ref[0, 3:4, :]).astype(BF16)
        yield
        gt = _dot(h, wg_ref[...])
        up = _dot(h, wu_ref[...])
        yield
        act = (gt * _sigmoid(gt) * up).astype(BF16)
        yield
        ff = _dot(act, wo2_ref[...])
        yield
        y2 = ALPHA * x1 + mod_ref[0, 5:6, :] * ff
        o_ref[rows, :] = _layer_norm(y2, lg2_ref[...], lb2_ref[...])

    _wavefront([row_group(r) for r in range(x_ref.shape[0] // POST_ROWS)])


def _post(o_dirs, pb16, o_hy, x, mod3, cond_row, norm_w, pa, pb, wo, ln_g, ln_b,
          ffn_w_in, ffn_w_out, ln2_g, ln2_b):
    tm = 512
    t = x.shape[0]
    n_dir = len(o_dirs)
    once = lambda shape, idx: pl.BlockSpec(shape, idx, pipeline_mode=pl.Buffered(1))
    tok = pl.BlockSpec((tm, D_MODEL), lambda i: (i, 0))
    col = lambda cb: pl.BlockSpec((tm, D_MODEL), lambda i, cb=cb: (i, cb))
    mat = once((D_MODEL, D_MODEL), lambda i: (0, 0))
    vec = pl.BlockSpec((1, D_MODEL), lambda i: (0, 0))
    return pl.pallas_call(
        functools.partial(_post_kernel, n_dir=n_dir),
        grid=(t // tm,),
        in_specs=[tok] * n_dir + [col(OB_G), col(OB_GA), col(OB_GB), tok, tok,
                  pl.BlockSpec((1, N_MOD, D_MODEL), lambda i: (cond_row(i * tm), 0, 0)),
                  pl.BlockSpec((1, DV), lambda i: (0, 0)),
                  mat, mat, mat, vec, vec,
                  once((D_MODEL, D_FF), lambda i: (0, 0)),
                  once((D_MODEL, D_FF), lambda i: (0, 1)),
                  once((D_FF, D_MODEL), lambda i: (0, 0)),
                  vec, vec],
        out_specs=tok,
        out_shape=jax.ShapeDtypeStruct((t, D_MODEL), F32),
        compiler_params=_params("parallel"),
        name="merge_ffn",
    )(*o_dirs, pb16, pb16, pb16, o_hy, x, mod3, norm_w[None, :], pa, pb, wo, ln_g[None, :], ln_b[None, :],
      ffn_w_in, ffn_w_in, ffn_w_out, ln2_g[None, :], ln2_b[None, :])


def _trunk(x, mod3, cond_row, nseq, seq_len, row_len, state, emit_state, w, lb_logits):
    nb = seq_len // BLK
    pf, pb = _inproj(x, mod3, w["w_in"], lb_logits, cond_row)
    hg = _hgrn(pf, pb, state, nseq, nb, emit_state)
    n_dir = 2 if nb > 1 else 1
    khat = _filter_spectra(seq_len, w["filt_w1"], w["filt_b1"], w["filt_w2"], w["filt_b2"],
                           w["filt_w3"], w["filt_b3"], w["filt_freq"], w["filt_w4"])
    if nb == 1:
        o_hy = _hyena_single(pb, khat, w["hy_conv_w"], w["hy_conv_b"], w["hy_skip"], row_len)
    else:
        o_hy = _hyena_multi(pb, khat, w["hy_conv_w"], w["hy_conv_b"], w["hy_skip"], nseq, nb, row_len)
    x2 = _post(hg[:n_dir], pb, o_hy, x, mod3, cond_row, w["hgrn_norm_w"], w["proj_a"], w["proj_b"],
               w["w_out"], w["ln1_g"], w["ln1_b"], w["ffn_w_in"], w["ffn_w_out"], w["ln2_g"], w["ln2_b"])
    return x2, (hg[n_dir] if emit_state else None)


def kernel(x_prompt, x_sample, state_hgrn, c, c_ctx, ada_w, ada_b, w_in, hgrn_lb_logits, hgrn_norm_w,
           hy_conv_w, hy_conv_b, filt_w1, filt_b1, filt_w2, filt_b2, filt_w3, filt_b3, filt_freq, filt_w4,
           hy_skip, proj_a, proj_b, w_out, ln1_g, ln1_b, ffn_w_in, ffn_w_out, ln2_g, ln2_b):
    assert ada_w.shape[0] == DEPTH == 1
    batch, seq, _ = x_prompt.shape
    dec_batch, dec_seq, _ = x_sample.shape
    assert seq % BLK == 0 and dec_seq % BLK == 0 and BLK % GRID_W == 0 and dec_batch + 1 <= 8

    w = dict(w_in=w_in[0].astype(BF16), hy_conv_w=hy_conv_w[0], hy_conv_b=hy_conv_b[0],
             filt_w1=filt_w1[0], filt_b1=filt_b1[0], filt_w2=filt_w2[0], filt_b2=filt_b2[0],
             filt_w3=filt_w3[0], filt_b3=filt_b3[0], filt_freq=filt_freq[0], filt_w4=filt_w4[0],
             hy_skip=hy_skip[0], hgrn_norm_w=hgrn_norm_w[0], proj_a=proj_a[0].astype(BF16),
             proj_b=proj_b[0].astype(BF16), w_out=w_out[0].astype(BF16), ln1_g=ln1_g[0], ln1_b=ln1_b[0],
             ffn_w_in=ffn_w_in[0].astype(BF16), ffn_w_out=ffn_w_out[0].astype(BF16),
             ln2_g=ln2_g[0], ln2_b=ln2_b[0])

    cond8 = jnp.zeros((8, D_MODEL), F32).at[0].set(c_ctx).at[1:1 + dec_batch].set(c)
    mod3 = _modulation(cond8, ada_w[0], ada_b[0][None, :]).reshape(8, N_MOD, D_MODEL)

    xp = x_prompt.reshape(batch * seq, D_MODEL)
    xs = x_sample.reshape(dec_batch * dec_seq, D_MODEL)
    yp, new_state = _trunk(xp, mod3, lambda tok: 0, batch, seq, seq, None, True, w, hgrn_lb_logits)
    ys, _ = _trunk(xs, mod3, lambda tok: 1 + tok // dec_seq, dec_batch, dec_seq, GRID_W, state_hgrn, False,
                   w, hgrn_lb_logits)
    return (yp.reshape(batch, seq, D_MODEL), ys.reshape(dec_batch, dec_seq, D_MODEL), new_state)
```

```python
import functools
import math

import numpy as np
import jax
import jax.numpy as jnp
from jax import lax
from jax.experimental import pallas as pl
from jax.experimental.pallas import tpu as pltpu

F32 = jnp.float32
BF16 = jnp.bfloat16

D_MODEL = 1024
DEPTH = 1
GRID_W = 64
H_A = 8
DK = 128
DV = 128
D_B = 1024
FILT_EMB = 33
FILT_BANDS = 16
FILT_ORDER = 64
DECAY_FAST = 0.3
DECAY_SLOW = 1.5
DECAY_TARGET = 1e-2
DECAY_SHIFT = 0.05
D_FF = 2816
N_MOD = 6
W_IN_COLS = 10 * D_MODEL
ALPHA = (2.0 * DEPTH) ** 0.25
LN_EPS = 1e-5
RMS_EPS = 1e-6

LANE = 128
SUBLANES = 8
BLK = 256
NFREQ = 2 * BLK
CHUNK = 32
NCHUNK = BLK // CHUNK
HPS = 8
PROJ_ROWS = 128
POST_ROWS = 256
HY_SEQS = 1
FILT_BLOCKS = 8
VMEM_LIMIT = 56 * 1024 * 1024

CB_Q, CB_FF, CB_FB, CB_I, CB_G, CB_X0, CB_X1, CB_V, CB_GA, CB_GB = range(10)
STEP_COLS = (CB_FF, CB_FB, CB_Q, CB_G, CB_GA, CB_GB, CB_I, CB_X0, CB_X1, CB_V)
N_F32_COLS = 2
OF_FF, OF_FB = range(N_F32_COLS)
OB_Q, OB_G, OB_GA, OB_GB, OB_I, OB_X0, OB_X1, OB_V = range(len(STEP_COLS) - N_F32_COLS)


def _sigmoid(x):
    return 1.0 / (1.0 + jnp.exp(-x))


def _dot(a, b):
    return jnp.dot(a, b, preferred_element_type=F32)


def _dot_nt(a, b):
    return lax.dot_general(a, b, (((1,), (1,)), ((), ())), preferred_element_type=F32)


def _dot_tn(a, b):
    return lax.dot_general(a, b, (((0,), (0,)), ((), ())), preferred_element_type=F32)


def _dot_hi(a, b):
    return jnp.dot(a, b, preferred_element_type=F32, precision=lax.Precision.HIGHEST)


def _params(*sem):
    return pltpu.CompilerParams(dimension_semantics=sem, vmem_limit_bytes=VMEM_LIMIT)


@functools.lru_cache(maxsize=None)
def _dft_consts():
    n = np.arange(BLK, dtype=np.float64)
    f = np.arange(BLK, dtype=np.float64)
    ang = 2.0 * np.pi * np.outer(f, n) / NFREQ
    fwd = np.zeros((NFREQ, BLK), np.float64)
    fwd[:BLK] = np.cos(ang)
    fwd[BLK + 1:] = -np.sin(ang[1:])
    fwd[BLK] = np.cos(np.pi * n)
    inv = np.zeros((BLK, NFREQ), np.float64)
    scale = np.full((BLK,), 2.0)
    scale[0] = 1.0
    inv[:, :BLK] = np.cos(ang.T) * scale[None, :]
    inv[:, BLK + 1:] = -2.0 * np.sin(ang.T[:, 1:])
    inv[:, BLK] = np.cos(np.pi * n)
    inv /= NFREQ
    fr = np.arange(NFREQ)
    freq_of_row = np.where(fr < BLK, fr, np.where(fr == BLK, BLK, fr - BLK))
    sgn = np.where(freq_of_row % 2 == 0, 1.0, -1.0)[:, None]
    return fwd.astype(np.float32), inv.astype(np.float32), sgn.astype(np.float32)


@functools.lru_cache(maxsize=None)
def _scan_consts():
    t = np.arange(BLK)
    ct = t // CHUNK
    same = ct[:, None] == ct[None, :]
    tri_f = (same & (t[None, :] <= t[:, None])).astype(np.float32)
    tri_b = (same & (t[None, :] >= t[:, None])).astype(np.float32)

    def levels(p, diag):
        x = p[:, None] ^ p[None, :]
        lvl = np.zeros_like(x)
        for bit in range(1, NCHUNK.bit_length()):
            lvl = np.where(x >= (1 << (bit - 1)), bit, lvl)
        lv = np.where(p[:, None] > p[None, :], lvl, -1)
        return np.where(same, np.where(diag, 0, -1), lv).astype(np.int32)

    lv_f = levels(ct, t[None, :] <= t[:, None])
    lv_b = levels(NCHUNK - 1 - ct, t[None, :] >= t[:, None])
    return tri_f, tri_b, lv_f, lv_b


@functools.lru_cache(maxsize=None)
def _filter_positions(seq_len):
    f32 = np.float32
    j = np.arange(-seq_len, seq_len)
    p = np.abs(j)
    valid = (j > -seq_len)
    pc = np.minimum(p, seq_len - 1)
    t = np.linspace(0.0, 1.0, seq_len, dtype=f32)[pc]
    wpos = (f32(2.0 * math.pi / seq_len) * np.arange(seq_len, dtype=f32))[pc]
    bands = np.linspace(1e-4, FILT_BANDS - 1, FILT_BANDS, dtype=f32)
    arg = (bands[None, :] * wpos[:, None]).astype(f32)
    z = np.zeros((2 * seq_len, LANE), f32)
    z[:, 0] = t
    z[:, 1:1 + FILT_BANDS] = np.cos(arg)
    z[:, 1 + FILT_BANDS:FILT_EMB] = -np.sin(arg)
    z[:, FILT_EMB] = valid.astype(f32)
    return z


@functools.lru_cache(maxsize=None)
def _decay_rates():
    max_decay = math.log(DECAY_TARGET) / DECAY_FAST
    min_decay = math.log(DECAY_TARGET) / DECAY_SLOW
    return np.abs(np.linspace(min_decay, max_decay, D_B, dtype=np.float32))[None, :]


def _mod_kernel(c_ref, w_ref, b_ref, o_ref):
    c = c_ref[...]
    s = (c * _sigmoid(c)).astype(BF16)
    o_ref[...] = _dot(s, w_ref[...].astype(BF16)) + b_ref[...]


def _modulation(cond8, ada_w, ada_b):
    tn = 1536
    n = N_MOD * D_MODEL
    return pl.pallas_call(
        _mod_kernel,
        grid=(n // tn,),
        in_specs=[pl.BlockSpec((8, D_MODEL), lambda j: (0, 0)),
                  pl.BlockSpec((D_MODEL, tn), lambda j: (0, j)),
                  pl.BlockSpec((1, tn), lambda j: (0, j))],
        out_specs=pl.BlockSpec((8, tn), lambda j: (0, j)),
        out_shape=jax.ShapeDtypeStruct((8, n), F32),
        compiler_params=_params("parallel"),
        name="modulation",
    )(cond8, ada_w, ada_b)


def _lower_bounds(lbl_ref):
    l0 = lbl_ref[0]
    l1 = lbl_ref[1]
    m = jnp.maximum(l0, l1)
    e0 = jnp.exp(l0 - m)
    e1 = jnp.exp(l1 - m)
    return e0 / (e0 + e1)


def _inproj_kernel(x_ref, mod_ref, w_ref, lbl_ref, of_ref, ob_ref, h_ref):
    j = pl.program_id(1)

    @pl.when(j == 0)
    def _():
        h = x_ref[...] * (1.0 + mod_ref[0, 1:2, :]) + mod_ref[0, 0:1, :]
        h_ref[...] = h.astype(BF16)

    def project(o_ref, act):
        for r in range(x_ref.shape[0] // PROJ_ROWS):
            rows = pl.ds(r * PROJ_ROWS, PROJ_ROWS)
            o_ref[rows, :] = act(_dot(h_ref[rows, :], w_ref[...])).astype(o_ref.dtype)

    silu = lambda a: a * _sigmoid(a)
    step = STEP_COLS.index

    @pl.when(jnp.logical_or(j == step(CB_FF), j == step(CB_FB)))
    def _():
        lb2 = _lower_bounds(lbl_ref)
        lb = jnp.where(j == step(CB_FF), lb2[0:1, :], lb2[1:2, :])
        project(of_ref, lambda a: jnp.log(lb + (1.0 - lb) * _sigmoid(a)))

    @pl.when(jnp.logical_or(j == step(CB_Q), j == step(CB_G)))
    def _():
        project(ob_ref, silu)

    @pl.when(jnp.logical_or(j == step(CB_GA), j == step(CB_GB)))
    def _():
        project(ob_ref, _sigmoid)

    @pl.when(j >= step(CB_I))
    def _():
        project(ob_ref, lambda a: a)


def _weight_col(j):
    col = jnp.int32(STEP_COLS[-1])
    for step in reversed(range(len(STEP_COLS) - 1)):
        col = jnp.where(j == step, STEP_COLS[step], col)
    return col


def _inproj(x, mod3, w_bf, lb_logits, cond_row):
    tm = 2048
    t = x.shape[0]
    n_b16 = len(STEP_COLS) - N_F32_COLS
    return pl.pallas_call(
        _inproj_kernel,
        grid=(t // tm, len(STEP_COLS)),
        in_specs=[pl.BlockSpec((tm, D_MODEL), lambda i, j: (i, 0)),
                  pl.BlockSpec((1, N_MOD, D_MODEL), lambda i, j: (cond_row(i * tm), 0, 0)),
                  pl.BlockSpec((D_MODEL, D_MODEL), lambda i, j: (0, _weight_col(j))),
                  pl.BlockSpec((2, 2, D_MODEL), lambda i, j: (0, 0, 0))],
        out_specs=[pl.BlockSpec((tm, D_MODEL), lambda i, j: (i, jnp.minimum(j, N_F32_COLS - 1))),
                   pl.BlockSpec((tm, D_MODEL), lambda i, j: (i, jnp.maximum(j - N_F32_COLS, 0)))],
        out_shape=[jax.ShapeDtypeStruct((t, N_F32_COLS * D_MODEL), F32),
                   jax.ShapeDtypeStruct((t, n_b16 * D_MODEL), BF16)],
        scratch_shapes=[pltpu.VMEM((tm, D_MODEL), BF16)],
        compiler_params=_params("parallel", "arbitrary"),
        name="inproj",
    )(x, mod3, w_bf, lb_logits)


def _chunk_cumsum(lf, tri):
    lf_hi = lf.astype(BF16)
    lf_lo = (lf - lf_hi.astype(F32)).astype(BF16)
    return _dot(tri, lf_hi) + _dot(tri, lf_lo)


def _wavefront(units, newest_first=False):
    waiting, active = list(units), []
    while waiting or active:
        if waiting:
            active.append(waiting.pop(0))
        for unit in (reversed(list(active)) if newest_first else list(active)):
            try:
                next(unit)
            except StopIteration:
                active.remove(unit)


def _hgrn_direction(load, lv, reverse, use_state, finish):
    q, lf, b, v, st = load()
    k = 1.0 - jnp.exp(lf)
    qe = q.astype(F32) * jnp.exp(b)
    k0 = k * jnp.exp(-b)
    yield

    order = [NCHUNK - 1 - i for i in range(NCHUNK)] if reverse else list(range(NCHUNK))
    chunk_of = {p: i for i, p in enumerate(order)}
    sl = lambda i: slice(i * CHUNK, (i + 1) * CHUNK)
    last_row = lambda i: (i * CHUNK) if reverse else (i * CHUNK + CHUNK - 1)
    c = [None] * NCHUNK
    for p in range(NCHUNK):
        r = last_row(chunk_of[p])
        c[p] = b[r:r + 1, :]
    cum = [jnp.zeros_like(c[0])]
    for p in range(NCHUNK):
        cum.append(cum[p] + c[p])
    total = cum[NCHUNK]

    qe_c, ke_c = {}, {}
    for p in range(NCHUNK):
        i = chunk_of[p]
        qe_c[p] = qe[sl(i), :]
        ke_c[p] = k0[sl(i), :] * jnp.exp(c[p])

    def assemble(parts):
        return jnp.concatenate([parts[order[i]] for i in range(NCHUNK)], axis=0).astype(BF16)

    cph = NCHUNK // 2
    nlev = NCHUNK.bit_length() - 1

    def half_rows(hh):
        first = (1 - hh) if reverse else hh
        return slice(first * cph * CHUNK, (first + 1) * cph * CHUNK)

    def assemble_half(parts, hh):
        ps = range(hh * cph, (hh + 1) * cph)
        return jnp.concatenate([parts[p] for p in (reversed(ps) if reverse else ps)], axis=0).astype(BF16)

    zero = jnp.zeros((CHUNK, LANE), F32)
    qe_bf, k0_bf = qe.astype(BF16), k0.astype(BF16)
    yield
    s_half = [jnp.where(lv == 0, _dot_nt(qe_bf[half_rows(hh), :], k0_bf[half_rows(hh), :]), 0.0)
              for hh in range(2)]
    for lev in range(1, nlev):
        yield
        mid = 1 << (lev - 1)
        qp, kp = {}, {}
        for p in range(NCHUNK):
            pm = ((p >> lev) << lev) + mid
            if p >= pm:
                qp[p] = qe_c[p] * jnp.exp(cum[p] - cum[pm])
                kp[p] = zero
            else:
                qp[p] = zero
                kp[p] = ke_c[p] * jnp.exp(cum[pm] - cum[p + 1])
        for hh in range(2):
            s_lev = _dot_nt(assemble_half(qp, hh), assemble_half(kp, hh))
            s_half[hh] = jnp.where(lv == lev, s_lev, s_half[hh])
    yield
    q_top = assemble_half({p: qe_c[p] * jnp.exp(cum[p] - cum[cph]) for p in range(cph, NCHUNK)}, 1)
    k_top = assemble_half({p: ke_c[p] * jnp.exp(cum[cph] - cum[p + 1]) for p in range(cph)}, 0)
    s_top = _dot_nt(q_top, k_top)
    yield

    v_bf = v.astype(BF16)
    v_half = [v_bf[half_rows(hh), :] for hh in range(2)]
    out_half = [_dot(s_half[0].astype(BF16), v_half[0]),
                _dot(jnp.concatenate([s_top, s_half[1]], axis=1).astype(BF16),
                     jnp.concatenate(v_half, axis=0))]
    out = jnp.concatenate(out_half[::-1] if reverse else out_half, axis=0)
    yield
    if use_state:
        q_start = assemble({p: qe_c[p] * jnp.exp(cum[p]) for p in range(NCHUNK)})
        out = out + _dot_nt(q_start, st.astype(BF16))
    k_end = assemble({p: ke_c[p] * jnp.exp(total - cum[p + 1]) for p in range(NCHUNK)})
    upd = _dot_tn(v_bf, k_end)
    new_st = st * jnp.exp(total) + upd if use_state else upd
    finish(out, new_st)


def _hgrn_kernel(*refs, nb, zero_init, emit_state):
    it = iter(refs)
    qf_ref, lff_ref, vf_ref, qb_ref, lfb_ref, vb_ref = [next(it) for _ in range(6)]
    s0_ref = None if zero_init else next(it)
    lvf_ref, lvb_ref, trif_ref, trib_ref = [next(it) for _ in range(4)]
    of_ref = next(it)
    ob_ref = next(it) if nb > 1 else None
    so_ref = next(it) if emit_state else None
    st_ref = next(it)
    i = pl.program_id(2)

    use_state = not (zero_init and nb == 1)
    if use_state:
        @pl.when(i == 0)
        def _():
            for d in range(2):
                for h in range(HPS):
                    st_ref[d, h] = jnp.zeros((DV, DK), F32) if zero_init else s0_ref[0, 0, d, h].T

    lf_f = lff_ref[...]
    lf_b = lfb_ref[...]
    b_f = _chunk_cumsum(lf_f, trif_ref[...])
    b_b = _chunk_cumsum(lf_b, trib_ref[...])
    lv_f = lvf_ref[...]
    lv_b = lvb_ref[...]
    def unit(d, h):
        hs = slice(h * LANE, (h + 1) * LANE)
        q_ref, lf, b, v_ref, o_ref = ((qf_ref, lf_f, b_f, vf_ref, of_ref), (qb_ref, lf_b, b_b, vb_ref, ob_ref))[d]

        def finish(out, new_st):
            if nb > 1:
                o_ref[:, hs] = out
                st_ref[d, h] = new_st
            elif d == 0:
                of_ref[:, hs] = out
            else:
                of_ref[:, hs] += out
            if emit_state and nb == 1:
                so_ref[0, 0, d, h] = new_st.T
            elif emit_state:
                @pl.when(i == nb - 1)
                def _():
                    so_ref[0, 0, d, h] = new_st.T

        return _hgrn_direction(lambda: (q_ref[:, hs], lf[:, hs], b[:, hs], v_ref[:, hs], st_ref[d, h]),
                               (lv_f, lv_b)[d], d == 1, use_state, finish)

    _wavefront([unit(d, h) for d in range(2) for h in range(HPS)], newest_first=use_state)


def _hgrn(pf, pb, state, nseq, nb, emit_state):
    zero_init = state is None
    t = pf.shape[0]
    tri_f, tri_b, lv_f, lv_b = _scan_consts()
    wid = HPS * LANE
    per = D_MODEL // wid
    fwd = lambda cb: pl.BlockSpec((BLK, wid), lambda b, h, i, cb=cb: (b * nb + i, cb * per + h))
    bwd = lambda cb: pl.BlockSpec((BLK, wid), lambda b, h, i, cb=cb: (b * nb + nb - 1 - i, cb * per + h))
    const = lambda n=BLK: pl.BlockSpec((n, n), lambda b, h, i: (0, 0))
    hl = BLK // 2
    st_spec = pl.BlockSpec((1, 1, 2, HPS, DK, DV), lambda b, h, i: (b, 0, 0, h, 0, 0))
    in_specs = [fwd(OB_Q), fwd(OF_FF), fwd(OB_I), bwd(OB_Q), bwd(OF_FB), bwd(OB_I)]
    args = [pb, pf, pb, pb, pf, pb]
    if not zero_init:
        in_specs.append(st_spec)
        args.append(state)
    in_specs += [const(hl), const(hl), const(), const()]
    args += [jnp.asarray(lv_f[:hl, :hl]), jnp.asarray(lv_b[:hl, :hl]),
             jnp.asarray(tri_f, BF16), jnp.asarray(tri_b, BF16)]
    out_specs = [pl.BlockSpec((BLK, wid), lambda b, h, i: (b * nb + i, h))]
    out_shape = [jax.ShapeDtypeStruct((t, D_MODEL), F32)]
    if nb > 1:
        out_specs.append(pl.BlockSpec((BLK, wid), lambda b, h, i: (b * nb + nb - 1 - i, h)))
        out_shape.append(jax.ShapeDtypeStruct((t, D_MODEL), F32))
    if emit_state:
        out_specs.append(st_spec)
        out_shape.append(jax.ShapeDtypeStruct((nseq, DEPTH, 2, H_A, DK, DV), F32))
    return pl.pallas_call(
        functools.partial(_hgrn_kernel, nb=nb, zero_init=zero_init, emit_state=emit_state),
        grid=(nseq, H_A // HPS, nb),
        in_specs=in_specs,
        out_specs=out_specs,
        out_shape=out_shape,
        scratch_shapes=[pltpu.VMEM((2, HPS, DV, DK), F32)],
        compiler_params=_params("parallel", "parallel", "arbitrary"),
        name="hgrn_scan",
    )(*args)


def _split_bf16(x):
    hi = x.astype(BF16)
    return hi, (x - hi.astype(F32)).astype(BF16)


def _dot3(a, b):
    a_hi, a_lo = _split_bf16(a)
    b_hi, b_lo = _split_bf16(b)
    return _dot(a_hi, b_hi) + (_dot(a_hi, b_lo) + _dot(a_lo, b_hi))


def _filter_kernel(z_ref, zt_ref, w1_ref, b1_ref, w2_ref, b2_ref, w3_ref, b3_ref, fq_ref, w4_ref, dec_ref,
                   fhi_ref, flo_ref, sgn_ref, o_ref, prev_ref, *, blocks):
    pad = jnp.zeros((LANE - FILT_ORDER, BLK), F32)

    def layer(x, w_ref, b_ref, k):
        h = jnp.sin(fq_ref[:, k:k + 1] * (_dot_hi(w_ref[...], x) + b_ref[...]))
        return jnp.concatenate([h, pad], axis=0)

    @pl.when(pl.program_id(0) == 0)
    def _():
        prev_ref[...] = jnp.zeros_like(prev_ref)

    spectra = [prev_ref[...]]

    def tap_block(blk):
        pos = slice(blk * BLK, (blk + 1) * BLK)
        h = layer(zt_ref[:, pos], w1_ref, b1_ref, 0)
        yield
        h = layer(h, w2_ref, b2_ref, 1)
        yield
        h = layer(h, w3_ref, b3_ref, 2)
        yield
        a = _dot3(h.T, w4_ref[...])
        yield
        zp = z_ref[pos, :]
        window = jnp.exp(-zp[:, 0:1] * dec_ref[...]) + DECAY_SHIFT
        a = a * window * zp[:, FILT_EMB:FILT_EMB + 1]
        a_hi, a_lo = _split_bf16(a)
        yield
        f_hi = fhi_ref[...]
        ah = _dot(f_hi, a_hi) + (_dot(f_hi, a_lo) + _dot(flo_ref[...], a_hi))
        yield
        o_ref[blk] = ah + sgn_ref[...] * spectra[blk]
        spectra.append(ah)

    _wavefront([tap_block(blk) for blk in range(blocks)], newest_first=True)
    prev_ref[...] = spectra[blocks]


def _filter_spectra(seq_len, w1, b1, w2, b2, w3, b3, freq, w4):
    nb = seq_len // BLK
    fwd_dft, _, sgn = _dft_consts()
    zpos_np = _filter_positions(seq_len)
    zpos = jnp.asarray(zpos_np)
    zpos_t = jnp.asarray(np.ascontiguousarray(zpos_np.T))
    f_hi, f_lo = _split_bf16(jnp.asarray(fwd_dft))
    wt = lambda w: jnp.pad(w, ((0, LANE - w.shape[0]), (0, 0))).T
    colv = lambda b: b[:, None]
    w1p, w2p, w3p = wt(w1), wt(w2), wt(w3)
    b1p, b2p, b3p = colv(b1), colv(b2), colv(b3)
    fqp = freq.T
    w4p = jnp.pad(w4, ((0, LANE - w4.shape[0]), (0, 0)))
    small = lambda shape: pl.BlockSpec(shape, lambda s: (0, 0))
    blocks = math.gcd(nb, FILT_BLOCKS)
    return pl.pallas_call(
        functools.partial(_filter_kernel, blocks=blocks),
        grid=(2 * nb // blocks,),
        in_specs=[pl.BlockSpec((blocks * BLK, LANE), lambda s: (s, 0)),
                  pl.BlockSpec((LANE, blocks * BLK), lambda s: (0, s)),
                  small((FILT_ORDER, LANE)), small((FILT_ORDER, 1)), small((FILT_ORDER, LANE)),
                  small((FILT_ORDER, 1)), small((FILT_ORDER, LANE)), small((FILT_ORDER, 1)),
                  small((FILT_ORDER, 3)),
                  pl.BlockSpec((LANE, D_B), lambda s: (0, jnp.where(s * blocks < nb, 1, 0))),
                  small((1, D_B)), small((NFREQ, BLK)), small((NFREQ, BLK)), small((NFREQ, 1))],
        out_specs=pl.BlockSpec((blocks, NFREQ, D_B), lambda s: (s, 0, 0)),
        out_shape=jax.ShapeDtypeStruct((2 * nb, NFREQ, D_B), F32),
        scratch_shapes=[pltpu.VMEM((NFREQ, D_B), F32)],
        compiler_params=_params("arbitrary"),
        name="hyena_filter",
    )(zpos, zpos_t, w1p, b1p, w2p, b2p, w3p, b3p, fqp, w4p, jnp.asarray(_decay_rates()),
      f_hi, f_lo, jnp.asarray(sgn))


def _short_conv_gate(u0, u1, uv, w_refs, b_refs, row_len):
    sublane = lax.broadcasted_iota(jnp.int32, (SUBLANES, 1), 0)

    def zero_rows(x, offset):
        slabs = []
        for g in range(BLK // SUBLANES):
            slab = x[g * SUBLANES:(g + 1) * SUBLANES, :]
            hit = (offset - g * SUBLANES) % row_len
            if hit < SUBLANES:
                slab = jnp.where(sublane == hit, 0.0, slab)
            slabs.append(slab)
        return jnp.concatenate(slabs, axis=0)

    def conv(u, w_ref, b_ref):
        u = u.astype(F32)
        up = zero_rows(pltpu.roll(u, 1, 0), 0)
        dn = zero_rows(pltpu.roll(u, BLK - 1, 0), row_len - 1)
        return up * w_ref[0:1, :] + u * w_ref[1:2, :] + dn * w_ref[2:3, :] + b_ref[...]

    return (conv(u0, w_refs[0], b_refs[0]),
            conv(uv, w_refs[2], b_refs[2]) * conv(u1, w_refs[1], b_refs[1]))


def _hy_single_kernel(x0_ref, x1_ref, v_ref, w0_ref, w1_ref, wv_ref, b0_ref, b1_ref, bv_ref, f_ref, kh_ref,
                      skip_ref, g_ref, o_ref, *, row_len):
    row0 = lax.broadcasted_iota(jnp.int32, (BLK, 1), 0) == 0

    def sequence(s):
        rows = pl.ds(s * BLK, BLK)
        x0, z = _short_conv_gate(x0_ref[rows, :], x1_ref[rows, :], v_ref[rows, :], (w0_ref, w1_ref, wv_ref),
                                 (b0_ref, b1_ref, bv_ref), row_len)
        z_bf = z.astype(BF16)
        yield
        zh = _dot(f_ref[...], z_bf)
        yield
        zr, zi = zh[:BLK, :], zh[BLK:, :]
        kr, ki = kh_ref[0, :BLK, :], kh_ref[0, BLK:, :]
        p = zr * kr
        q = zi * ki
        r = zr * ki + zi * kr
        yh = jnp.concatenate([jnp.where(row0, p, p - q), jnp.where(row0, q, r)], axis=0).astype(BF16)
        yield
        y = _dot(g_ref[...], yh)
        yield
        o_ref[rows, :] = (x0 * (y + z * skip_ref[...])).astype(o_ref.dtype)

    _wavefront([sequence(s) for s in range(o_ref.shape[0] // BLK)])


def _hyena_single(pb, khat, conv_w, conv_b, skip, row_len):
    t = pb.shape[0]
    fwd_dft, inv_dft, _ = _dft_consts()
    rows = HY_SEQS * BLK
    col = lambda cb: pl.BlockSpec((rows, D_B), lambda i, cb=cb: (i, cb))
    wcol = lambda r, k: pl.BlockSpec((r, D_B), lambda i, k=k: (0, k))
    conv_b = conv_b[None, :]
    return pl.pallas_call(
        functools.partial(_hy_single_kernel, row_len=row_len),
        grid=(t // rows,),
        in_specs=[col(OB_X0), col(OB_X1), col(OB_V),
                  wcol(3, 0), wcol(3, 1), wcol(3, 2), wcol(1, 0), wcol(1, 1), wcol(1, 2),
                  pl.BlockSpec((NFREQ, BLK), lambda i: (0, 0)),
                  pl.BlockSpec((1, NFREQ, D_B), lambda i: (1, 0, 0)),
                  pl.BlockSpec((1, D_B), lambda i: (0, 0)),
                  pl.BlockSpec((BLK, NFREQ), lambda i: (0, 0))],
        out_specs=pl.BlockSpec((rows, D_B), lambda i: (i, 0)),
        out_shape=jax.ShapeDtypeStruct((t, D_B), BF16),
        compiler_params=_params("parallel"),
        name="hyena_single",
    )(pb, pb, pb, conv_w, conv_w, conv_w, conv_b, conv_b, conv_b, jnp.asarray(fwd_dft).astype(BF16), khat,
      skip[None, :], jnp.asarray(inv_dft).astype(BF16))


HY_DT = 2 * LANE
ROWG = 16


def _hy_multi_kernel(x0_ref, x1_ref, v_ref, w0_ref, w1_ref, wv_ref, b0_ref, b1_ref, bv_ref, f_ref, kh_ref,
                     skip_ref, g_ref, o_ref, zh_ref, z_ref, x0s_ref, yh_ref, *, nb, row_len):
    dt = o_ref.shape[1]

    def front(blk, carry):
        rows = pl.ds(pl.multiple_of(blk * BLK, BLK), BLK)
        x0, z = _short_conv_gate(x0_ref[rows, :], x1_ref[rows, :], v_ref[rows, :],
                                 (w0_ref, w1_ref, wv_ref), (b0_ref, b1_ref, bv_ref), row_len)
        x0s_ref[rows, :] = x0
        z_ref[rows, :] = z
        zh_ref[blk] = _dot(f_ref[...], z.astype(BF16))
        return carry

    lax.fori_loop(0, nb, front, 0)

    row_in_group = lax.broadcasted_iota(jnp.int32, (ROWG, 1), 0)

    def back(i, carry):
        def row_group(rg, carry2):
            re = pl.ds(pl.multiple_of(rg * ROWG, ROWG), ROWG)
            im = pl.ds(pl.multiple_of(BLK + rg * ROWG, ROWG), ROWG)

            def body(j, acc):
                p, q, r = acc
                kidx = i - j + nb
                zr = zh_ref[j, re, :]
                zi = zh_ref[j, im, :]
                kr = kh_ref[kidx, re, :]
                ki = kh_ref[kidx, im, :]
                return (p + zr * kr, q + zi * ki, r + (zr * ki + zi * kr))

            zeros = jnp.zeros((ROWG, dt), F32)
            p, q, r = lax.fori_loop(0, nb, body, (zeros, zeros, zeros), unroll=True)
            row0 = (row_in_group + rg * ROWG) == 0
            yh_ref[re, :] = jnp.where(row0, p, p - q).astype(BF16)
            yh_ref[im, :] = jnp.where(row0, q, r).astype(BF16)
            return carry2

        lax.fori_loop(0, BLK // ROWG, row_group, 0)
        rows = pl.ds(pl.multiple_of(i * BLK, BLK), BLK)
        y = _dot(g_ref[...], yh_ref[...]) + z_ref[rows, :] * skip_ref[...]
        o_ref[rows, :] = (x0s_ref[rows, :] * y).astype(o_ref.dtype)
        return carry

    lax.fori_loop(0, nb, back, 0)


def _hyena_multi(pb, khat, conv_w, conv_b, skip, nseq, nb, row_len):
    t = pb.shape[0]
    seq_len = nb * BLK
    dt = HY_DT
    per = D_B // dt
    fwd_dft, inv_dft, _ = _dft_consts()
    seq = lambda cb: pl.BlockSpec((seq_len, dt), lambda d, b, cb=cb: (b, cb * per + d))
    wcol = lambda r, k: pl.BlockSpec((r, dt), lambda d, b, k=k: (0, k * per + d))
    conv_b = conv_b[None, :]
    return pl.pallas_call(
        functools.partial(_hy_multi_kernel, nb=nb, row_len=row_len),
        grid=(per, nseq),
        in_specs=[seq(OB_X0), seq(OB_X1), seq(OB_V),
                  wcol(3, 0), wcol(3, 1), wcol(3, 2), wcol(1, 0), wcol(1, 1), wcol(1, 2),
                  pl.BlockSpec((NFREQ, BLK), lambda d, b: (0, 0)),
                  pl.BlockSpec((2 * nb, NFREQ, dt), lambda d, b: (0, 0, d), pipeline_mode=pl.Buffered(1)),
                  pl.BlockSpec((1, dt), lambda d, b: (0, d)),
                  pl.BlockSpec((BLK, NFREQ), lambda d, b: (0, 0))],
        out_specs=pl.BlockSpec((seq_len, dt), lambda d, b: (b, d)),
        out_shape=jax.ShapeDtypeStruct((t, D_B), BF16),
        scratch_shapes=[pltpu.VMEM((nb, NFREQ, dt), F32), pltpu.VMEM((seq_len, dt), F32),
                        pltpu.VMEM((seq_len, dt), F32), pltpu.VMEM((NFREQ, dt), BF16)],
        compiler_params=_params("parallel", "parallel"),
        name="hyena_multi",
    )(pb, pb, pb, conv_w, conv_w, conv_w, conv_b, conv_b, conv_b, jnp.asarray(fwd_dft).astype(BF16), khat,
      skip[None, :], jnp.asarray(inv_dft).astype(BF16))


def _layer_norm(y, g, b):
    mu = jnp.mean(y, axis=-1, keepdims=True)
    yc = y - mu
    var = jnp.mean(yc * yc, axis=-1, keepdims=True)
    return yc * lax.rsqrt(var + LN_EPS) * g + b


def _post_kernel(*refs, n_dir):
    (g_ref, ga_ref, gb_ref, hy_ref, x_ref, mod_ref, nw_ref, pa_ref, pb_ref, wo_ref, lg_ref, lb_ref,
     wg_ref, wu_ref, wo2_ref, lg2_ref, lb2_ref, o_ref) = refs[n_dir:]
    nw = nw_ref[...]

    def row_group(r):
        rows = pl.ds(r * POST_ROWS, POST_ROWS)
        b = _dot(hy_ref[rows, :].astype(BF16), pb_ref[...])
        o = refs[0][rows, :]
        for d_ref in refs[1:n_dir]:
            o = o + d_ref[rows, :]
        parts = []
        for h in range(H_A):
            oh = o[:, h * DV:(h + 1) * DV]
            ms = jnp.mean(oh * oh, axis=-1, keepdims=True)
            parts.append(oh * lax.rsqrt(ms + RMS_EPS) * nw)
        oa = (jnp.concatenate(parts, axis=1) * g_ref[rows, :]).astype(BF16)
        yield
        a = _dot(oa, pa_ref[...])
        yield
        merged = (ga_ref[rows, :] * a + gb_ref[rows, :] * b).astype(BF16)
        yield
        mix = _dot(merged, wo_ref[...])
        yield
        y = ALPHA * x_ref[rows, :] + mod_ref[0, 2:3, :] * mix
        x1 = _layer_norm(y, lg_ref[...], lb_ref[...])
        h = (x1 * (1.0 + mod_ref[0, 4:5, :]) + mod_ref[0, 3:4, :]).astype(BF16)
        yield
        gt = _dot(h, wg_ref[...])
        up = _dot(h, wu_ref[...])
        yield
        act = (gt * _sigmoid(gt) * up).astype(BF16)
        yield
        ff = _dot(act, wo2_ref[...])
        yield
        y2 = ALPHA * x1 + mod_ref[0, 5:6, :] * ff
        o_ref[rows, :] = _layer_norm(y2, lg2_ref[...], lb2_ref[...])

    _wavefront([row_group(r) for r in range(x_ref.shape[0] // POST_ROWS)])


def _post(o_dirs, pb16, o_hy, x, mod3, cond_row, norm_w, pa, pb, wo, ln_g, ln_b,
          ffn_w_in, ffn_w_out, ln2_g, ln2_b):
    tm = 512
    t = x.shape[0]
    n_dir = len(o_dirs)
    once = lambda shape, idx: pl.BlockSpec(shape, idx, pipeline_mode=pl.Buffered(1))
    tok = pl.BlockSpec((tm, D_MODEL), lambda i: (i, 0))
    col = lambda cb: pl.BlockSpec((tm, D_MODEL), lambda i, cb=cb: (i, cb))
    mat = once((D_MODEL, D_MODEL), lambda i: (0, 0))
    vec = pl.BlockSpec((1, D_MODEL), lambda i: (0, 0))
    return pl.pallas_call(
        functools.partial(_post_kernel, n_dir=n_dir),
        grid=(t // tm,),
        in_specs=[tok] * n_dir + [col(OB_G), col(OB_GA), col(OB_GB), tok, tok,
                  pl.BlockSpec((1, N_MOD, D_MODEL), lambda i: (cond_row(i * tm), 0, 0)),
                  pl.BlockSpec((1, DV), lambda i: (0, 0)),
                  mat, mat, mat, vec, vec,
                  once((D_MODEL, D_FF), lambda i: (0, 0)),
                  once((D_MODEL, D_FF), lambda i: (0, 1)),
                  once((D_FF, D_MODEL), lambda i: (0, 0)),
                  vec, vec],
        out_specs=tok,
        out_shape=jax.ShapeDtypeStruct((t, D_MODEL), F32),
        compiler_params=_params("parallel"),
        name="merge_ffn",
    )(*o_dirs, pb16, pb16, pb16, o_hy, x, mod3, norm_w[None, :], pa, pb, wo, ln_g[None, :], ln_b[None, :],
      ffn_w_in, ffn_w_in, ffn_w_out, ln2_g[None, :], ln2_b[None, :])


def _trunk(x, mod3, cond_row, nseq, seq_len, row_len, state, emit_state, w, lb_logits):
    nb = seq_len // BLK
    pf, pb = _inproj(x, mod3, w["w_in"], lb_logits, cond_row)
    hg = _hgrn(pf, pb, state, nseq, nb, emit_state)
    n_dir = 2 if nb > 1 else 1
    khat = _filter_spectra(seq_len, w["filt_w1"], w["filt_b1"], w["filt_w2"], w["filt_b2"],
                           w["filt_w3"], w["filt_b3"], w["filt_freq"], w["filt_w4"])
    if nb == 1:
        o_hy = _hyena_single(pb, khat, w["hy_conv_w"], w["hy_conv_b"], w["hy_skip"], row_len)
    else:
        o_hy = _hyena_multi(pb, khat, w["hy_conv_w"], w["hy_conv_b"], w["hy_skip"], nseq, nb, row_len)
    x2 = _post(hg[:n_dir], pb, o_hy, x, mod3, cond_row, w["hgrn_norm_w"], w["proj_a"], w["proj_b"],
               w["w_out"], w["ln1_g"], w["ln1_b"], w["ffn_w_in"], w["ffn_w_out"], w["ln2_g"], w["ln2_b"])
    return x2, (hg[n_dir] if emit_state else None)


def kernel(x_prompt, x_sample, state_hgrn, c, c_ctx, ada_w, ada_b, w_in, hgrn_lb_logits, hgrn_norm_w,
           hy_conv_w, hy_conv_b, filt_w1, filt_b1, filt_w2, filt_b2, filt_w3, filt_b3, filt_freq, filt_w4,
           hy_skip, proj_a, proj_b, w_out, ln1_g, ln1_b, ffn_w_in, ffn_w_out, ln2_g, ln2_b):
    assert ada_w.shape[0] == DEPTH == 1
    batch, seq, _ = x_prompt.shape
    dec_batch, dec_seq, _ = x_sample.shape
    assert seq % BLK == 0 and dec_seq % BLK == 0 and BLK % GRID_W == 0 and dec_batch + 1 <= 8

    w = dict(w_in=w_in[0].astype(BF16), hy_conv_w=hy_conv_w[0], hy_conv_b=hy_conv_b[0],
             filt_w1=filt_w1[0], filt_b1=filt_b1[0], filt_w2=filt_w2[0], filt_b2=filt_b2[0],
             filt_w3=filt_w3[0], filt_b3=filt_b3[0], filt_freq=filt_freq[0], filt_w4=filt_w4[0],
             hy_skip=hy_skip[0], hgrn_norm_w=hgrn_norm_w[0], proj_a=proj_a[0].astype(BF16),
             proj_b=proj_b[0].astype(BF16), w_out=w_out[0].astype(BF16), ln1_g=ln1_g[0], ln1_b=ln1_b[0],
             ffn_w_in=ffn_w_in[0].astype(BF16), ffn_w_out=ffn_w_out[0].astype(BF16),
             ln2_g=ln2_g[0], ln2_b=ln2_b[0])

    cond8 = jnp.zeros((8, D_MODEL), F32).at[0].set(c_ctx).at[1:1 + dec_batch].set(c)
    mod3 = _modulation(cond8, ada_w[0], ada_b[0][None, :]).reshape(8, N_MOD, D_MODEL)

    xp = x_prompt.reshape(batch * seq, D_MODEL)
    xs = x_sample.reshape(dec_batch * dec_seq, D_MODEL)
    yp, new_state = _trunk(xp, mod3, lambda tok: 0, batch, seq, seq, None, True, w, hgrn_lb_logits)
    ys, _ = _trunk(xs, mod3, lambda tok: 1 + tok // dec_seq, dec_batch, dec_seq, GRID_W, state_hgrn, False,
                   w, hgrn_lb_logits)
    return (yp.reshape(batch, seq, D_MODEL), ys.reshape(dec_batch, dec_seq, D_MODEL), new_state)
```
